```python
import jax, jax.numpy as jnp
from jax import lax
import numpy as np

D_MODEL = 2048
BATCH = 16
SEQ = 2048
DEPTH = 4

CHUNK = 64
N_MIXERS = 3
N_A = (DEPTH + 2) // 3
N_B = (DEPTH + 1) // 3
N_C = DEPTH // 3

A_HEADS = 16
A_HEAD_DIM = D_MODEL // A_HEADS
A_WIDTH = A_HEADS * A_HEAD_DIM
LEFT_CHUNKS = 8
BAND = (LEFT_CHUNKS + 1) * CHUNK
REL_CLIP = 256
N_REL = 2 * REL_CLIP + 1

RG_WIDTH = 5 * D_MODEL // 4
RG_BLOCKS = 16
RG_BLOCK = RG_WIDTH // RG_BLOCKS
CONV_WIDTH = 4
RG_C = 8.0

C_HEADS = 16
C_HEAD_DIM = D_MODEL // C_HEADS
C_WIDTH = C_HEADS * C_HEAD_DIM
Q_BLOCK = 128

RMS_EPS = 1e-6
NEG_INF = -1e30

kernel_name = "hybrid_chunked_attn_rglru_fox_trunk"


def rmsnorm(x, g):
    xf = x.astype(jnp.float32)
    y = xf * lax.rsqrt(jnp.mean(xf * xf, axis=-1, keepdims=True) + RMS_EPS)
    return (y * g.astype(jnp.float32)).astype(x.dtype)


def chunked_rel_attention(q, k, v, rel_bias):
    B, S, H, dh = q.shape
    n_chunks = S // CHUNK
    pad = LEFT_CHUNKS * CHUNK
    kp = jnp.pad(k, ((0, 0), (pad, 0), (0, 0), (0, 0)))
    vp = jnp.pad(v, ((0, 0), (pad, 0), (0, 0), (0, 0)))
    dist = pad + jnp.arange(CHUNK)[:, None] - jnp.arange(BAND)[None, :]
    idx = jnp.clip(dist, -REL_CLIP, REL_CLIP) + REL_CLIP
    bias = rel_bias.astype(jnp.float32)[:, idx]
    scale = A_HEAD_DIM ** -0.5
    band_offsets = jnp.arange(BAND)

    def one_chunk(c):
        start = c * CHUNK
        qb = lax.dynamic_slice_in_dim(q, start, CHUNK, axis=1)
        kb = lax.dynamic_slice_in_dim(kp, start, BAND, axis=1)
        vb = lax.dynamic_slice_in_dim(vp, start, BAND, axis=1)
        s = jnp.einsum('bqhd,bkhd->bhqk', qb, kb).astype(jnp.float32) * scale + bias
        valid = (start - pad + band_offsets) >= 0
        s = jnp.where(valid[None, None, None, :], s, NEG_INF)
        p = jax.nn.softmax(s, axis=-1).astype(vb.dtype)
        return jnp.einsum('bhqk,bkhd->bqhd', p, vb)

    out = lax.map(one_chunk, jnp.arange(n_chunks))
    return out.transpose(1, 0, 2, 3, 4).reshape(B, S, H * dh)


def mixer_a(h, w_in, rel_bias, w_out):
    B, S, _ = h.shape
    proj = jnp.einsum('bsd,de->bse', h, w_in)
    q, k, v, g = jnp.split(proj, 4, axis=-1)
    shp = (B, S, A_HEADS, A_HEAD_DIM)
    o = chunked_rel_attention(q.reshape(shp), k.reshape(shp), v.reshape(shp), rel_bias)
    return jnp.einsum('bse,ed->bsd', o * jax.nn.silu(g), w_out)


def causal_depthwise_conv(x, w, b):
    S = x.shape[1]
    xp = jnp.pad(x, ((0, 0), (CONV_WIDTH - 1, 0), (0, 0)))
    y = xp[:, 0:S] * w[0]
    for t in range(1, CONV_WIDTH):
        y = y + xp[:, t:t + S] * w[t]
    return y + b


def rg_lru(x, w_a, b_a, w_x, b_x, lam):
    B, S, _ = x.shape
    xf = x.astype(jnp.float32)
    xb = xf.reshape(B, S, RG_BLOCKS, RG_BLOCK)
    r = jax.nn.sigmoid(jnp.einsum('bsnd,nde->bsne', xb, w_a.astype(jnp.float32)) + b_a.astype(jnp.float32))
    i = jax.nn.sigmoid(jnp.einsum('bsnd,nde->bsne', xb, w_x.astype(jnp.float32)) + b_x.astype(jnp.float32))
    r = r.reshape(B, S, RG_WIDTH)
    i = i.reshape(B, S, RG_WIDTH)
    log_a = -RG_C * jax.nn.softplus(-lam.astype(jnp.float32)) * r
    a = jnp.exp(log_a)
    u = jnp.sqrt(-jnp.expm1(2.0 * log_a)) * (i * xf)

    def combine(left, right):
        a1, b1 = left
        a2, b2 = right
        return a1 * a2, a2 * b1 + b2

    _, hs = lax.associative_scan(combine, (a, u), axis=1)
    return hs.astype(x.dtype)


def mixer_b(h, w_in, conv_w, conv_b, gate_a_w, gate_a_b, gate_x_w, gate_x_b, lam, w_out):
    proj = jnp.einsum('bsd,de->bse', h, w_in)
    xr, g = jnp.split(proj, 2, axis=-1)
    xr = causal_depthwise_conv(xr, conv_w, conv_b)
    hr = rg_lru(xr, gate_a_w, gate_a_b, gate_x_w, gate_x_b, lam)
    return jnp.einsum('bse,ed->bsd', hr * jax.nn.silu(g), w_out)


def forgetting_attention(q, k, v, log_f):
    B, S, H, dh = q.shape
    n_blocks = S // Q_BLOCK
    cum = jnp.cumsum(log_f, axis=1).transpose(0, 2, 1)
    kpos = jnp.arange(S)
    scale = C_HEAD_DIM ** -0.5

    def one_block(blk):
        start = blk * Q_BLOCK
        qb = lax.dynamic_slice_in_dim(q, start, Q_BLOCK, axis=1)
        cq = lax.dynamic_slice_in_dim(cum, start, Q_BLOCK, axis=2)
        s = jnp.einsum('bqhd,bkhd->bhqk', qb, k).astype(jnp.float32) * scale
        s = s + (cq[..., :, None] - cum[..., None, :])
        qpos = start + jnp.arange(Q_BLOCK)
        s = jnp.where((kpos[None, :] <= qpos[:, None])[None, None], s, NEG_INF)
        p = jax.nn.softmax(s, axis=-1).astype(v.dtype)
        return jnp.einsum('bhqk,bkhd->bqhd', p, v)

    out = lax.map(one_block, jnp.arange(n_blocks))
    return out.transpose(1, 0, 2, 3, 4).reshape(B, S, H * dh)


def mixer_c(h, w_in, f_bias, w_out):
    B, S, _ = h.shape
    proj = jnp.einsum('bsd,de->bse', h, w_in)
    q, k, v, g, f_logit = jnp.split(proj, [C_WIDTH, 2 * C_WIDTH, 3 * C_WIDTH, 4 * C_WIDTH], axis=-1)
    shp = (B, S, C_HEADS, C_HEAD_DIM)
    log_f = jax.nn.log_sigmoid(f_logit.astype(jnp.float32) + f_bias.astype(jnp.float32))
    o = forgetting_attention(q.reshape(shp), k.reshape(shp), v.reshape(shp), log_f)
    return jnp.einsum('bse,ed->bsd', o * jax.nn.silu(g), w_out)


def _dense(k, shape, fan_in):
    return jax.random.normal(k, shape, jnp.float32) * (fan_in ** -0.5)


def setup_inputs(seed: int = 0) -> dict:
    key = jax.random.key(seed)
    ks = jax.random.split(key, 20)
    x = jax.random.normal(ks[0], (BATCH, SEQ, D_MODEL), jnp.float32)
    norm_pre = 1.0 + 0.02 * jax.random.normal(ks[1], (DEPTH, D_MODEL), jnp.float32)
    norm_post = 1.0 + 0.02 * jax.random.normal(ks[2], (DEPTH, D_MODEL), jnp.float32)
    a_w_in = _dense(ks[3], (N_A, D_MODEL, 4 * A_WIDTH), D_MODEL)
    a_rel_bias = 0.1 * jax.random.normal(ks[4], (N_A, A_HEADS, N_REL), jnp.float32)
    a_w_out = _dense(ks[5], (N_A, A_WIDTH, D_MODEL), A_WIDTH)
    b_w_in = _dense(ks[6], (N_B, D_MODEL, 2 * RG_WIDTH), D_MODEL)
    b_conv_w = _dense(ks[7], (N_B, CONV_WIDTH, RG_WIDTH), CONV_WIDTH)
    b_conv_b = 0.01 * jax.random.normal(ks[8], (N_B, RG_WIDTH), jnp.float32)
    b_gate_a_w = _dense(ks[9], (N_B, RG_BLOCKS, RG_BLOCK, RG_BLOCK), RG_BLOCK)
    b_gate_a_b = 0.01 * jax.random.normal(ks[10], (N_B, RG_BLOCKS, RG_BLOCK), jnp.float32)
    b_gate_x_w = _dense(ks[11], (N_B, RG_BLOCKS, RG_BLOCK, RG_BLOCK), RG_BLOCK)
    b_gate_x_b = 0.01 * jax.random.normal(ks[12], (N_B, RG_BLOCKS, RG_BLOCK), jnp.float32)
    u = jax.random.uniform(ks[13], (N_B, RG_WIDTH), jnp.float32, minval=0.9, maxval=0.999)
    a0 = u ** (1.0 / RG_C)
    b_lambda = jnp.log(a0) - jnp.log1p(-a0)
    b_w_out = _dense(ks[14], (N_B, RG_WIDTH, D_MODEL), RG_WIDTH)
    c_w_in = _dense(ks[15], (N_C, D_MODEL, 4 * C_WIDTH + C_HEADS), D_MODEL)
    c_f_bias = 3.0 + 0.5 * jax.random.normal(ks[16], (N_C, C_HEADS), jnp.float32)
    c_w_out = _dense(ks[17], (N_C, C_WIDTH, D_MODEL), C_WIDTH)
    return {"x": x, "norm_pre": norm_pre, "norm_post": norm_post,
            "a_w_in": a_w_in, "a_rel_bias": a_rel_bias, "a_w_out": a_w_out,
            "b_w_in": b_w_in, "b_conv_w": b_conv_w, "b_conv_b": b_conv_b,
            "b_gate_a_w": b_gate_a_w, "b_gate_a_b": b_gate_a_b,
            "b_gate_x_w": b_gate_x_w, "b_gate_x_b": b_gate_x_b,
            "b_lambda": b_lambda, "b_w_out": b_w_out,
            "c_w_in": c_w_in, "c_f_bias": c_f_bias, "c_w_out": c_w_out}


def reference(x, norm_pre, norm_post, a_w_in, a_rel_bias, a_w_out, b_w_in, b_conv_w, b_conv_b,
              b_gate_a_w, b_gate_a_b, b_gate_x_w, b_gate_x_b, b_lambda, b_w_out,
              c_w_in, c_f_bias, c_w_out):
    for i in range(DEPTH):
        m = i % N_MIXERS
        j = i // N_MIXERS
        h = rmsnorm(x, norm_pre[i])
        if m == 0:
            y = mixer_a(h, a_w_in[j], a_rel_bias[j], a_w_out[j])
        elif m == 1:
            y = mixer_b(h, b_w_in[j], b_conv_w[j], b_conv_b[j], b_gate_a_w[j], b_gate_a_b[j],
                        b_gate_x_w[j], b_gate_x_b[j], b_lambda[j], b_w_out[j])
        else:
            y = mixer_c(h, c_w_in[j], c_f_bias[j], c_w_out[j])
        x = x + rmsnorm(y, norm_post[i])
    return x
```

```python
import functools

import jax
import jax.numpy as jnp
from jax import lax
from jax.experimental import pallas as pl
from jax.experimental.pallas import tpu as pltpu

RMS_EPS = 1e-6
NEG_INF = -1e30

A_HEADS = 16
A_CHUNK = 64
A_LEFT_CHUNKS = 8
A_REL_CLIP = 256
A_QBLOCK = A_LEFT_CHUNKS * A_CHUNK
A_QTILE = 2 * A_CHUNK
A_KTILE = (A_LEFT_CHUNKS + 2) * A_CHUNK

RG_BLOCKS = 16
RG_GROUP_BLOCKS = 4
CONV_WIDTH = 4
RG_C = 8.0
LRU_TS = 256
LRU_ROWS = 8

C_HEADS = 16
FOX_TQ = 256
FOX_TK = 256
F_PAD = 128

VMEM_LIMIT_BYTES = 56 * 1024 * 1024


def _params(*sem):
    return pltpu.CompilerParams(dimension_semantics=sem, vmem_limit_bytes=VMEM_LIMIT_BYTES)


def _norm_proj_body(x_ref, gam_ref, w_ref, *rest, has_extra):
    if has_extra:
        w2_ref, o_ref, o2_ref, xn_ref = rest
    else:
        o_ref, xn_ref = rest

    @pl.when(pl.program_id(1) == 0)
    def _():
        x = x_ref[...]
        ms = jnp.mean(x * x, axis=-1, keepdims=True)
        xn_ref[...] = (x * lax.rsqrt(ms + RMS_EPS) * gam_ref[...]).astype(jnp.bfloat16)
        if has_extra:
            o2_ref[...] = jnp.dot(xn_ref[...], w2_ref[...], preferred_element_type=jnp.float32)

    o_ref[...] = jnp.dot(xn_ref[...], w_ref[...],
                         preferred_element_type=jnp.float32).astype(o_ref.dtype)


def norm_proj(x, gamma, w, w2=None, *, tm=1024, tn=512):
    m, d = x.shape
    n = w.shape[1]
    tm, tn = min(tm, m), min(tn, n)
    assert m % tm == 0 and n % tn == 0
    in_specs = [pl.BlockSpec((tm, d), lambda i, j: (i, 0)),
                pl.BlockSpec((1, d), lambda i, j: (0, 0)),
                pl.BlockSpec((d, tn), lambda i, j: (0, j))]
    out_shape = [jax.ShapeDtypeStruct((m, n), jnp.bfloat16)]
    out_specs = [pl.BlockSpec((tm, tn), lambda i, j: (i, j))]
    args = [x, gamma.reshape(1, d), w]
    if w2 is not None:
        in_specs.append(pl.BlockSpec((d, w2.shape[1]), lambda i, j: (0, 0)))
        out_shape.append(jax.ShapeDtypeStruct((m, w2.shape[1]), jnp.float32))
        out_specs.append(pl.BlockSpec((tm, w2.shape[1]), lambda i, j: (i, 0)))
        args.append(w2)
    out = pl.pallas_call(
        functools.partial(_norm_proj_body, has_extra=w2 is not None),
        grid=(m // tm, n // tn),
        in_specs=in_specs, out_specs=out_specs, out_shape=out_shape,
        scratch_shapes=[pltpu.VMEM((tm, d), jnp.bfloat16)],
        compiler_params=_params("parallel", "arbitrary"),
        name="norm_proj",
    )(*args)
    return out if w2 is not None else out[0]


def _out_proj_body(a_ref, g_ref, w_ref, x_ref, gam_ref, o_ref):
    g = g_ref[...].astype(jnp.float32)
    a = (a_ref[...].astype(jnp.float32) * (g * jax.nn.sigmoid(g))).astype(jnp.bfloat16)
    y = jnp.dot(a, w_ref[...], preferred_element_type=jnp.float32)
    ms = jnp.mean(y * y, axis=-1, keepdims=True)
    o_ref[...] = x_ref[...] + y * lax.rsqrt(ms + RMS_EPS) * gam_ref[...]


def out_proj(a, proj, g_block, w, x, gamma, *, tm=256):
    m, e = a.shape
    d = w.shape[1]
    tm = min(tm, m)
    assert m % tm == 0
    return pl.pallas_call(
        _out_proj_body,
        grid=(m // tm,),
        in_specs=[pl.BlockSpec((tm, e), lambda i: (i, 0)),
                  pl.BlockSpec((tm, e), lambda i: (i, g_block)),
                  pl.BlockSpec((e, d), lambda i: (0, 0)),
                  pl.BlockSpec((tm, d), lambda i: (i, 0)),
                  pl.BlockSpec((1, d), lambda i: (0, 0))],
        out_specs=pl.BlockSpec((tm, d), lambda i: (i, 0)),
        out_shape=jax.ShapeDtypeStruct((m, d), jnp.float32),
        compiler_params=_params("parallel"),
        name="out_proj",
    )(a, proj, w, x, gamma.reshape(1, d))


def _chunk_attn_body(q_ref, kc_ref, vc_ref, bias_ref, o_ref, kcat_ref, vcat_ref, *, heads, dh):
    i = pl.program_id(1)
    scale = dh ** -0.5

    @pl.when(i == 0)
    def _():
        kcat_ref[0:A_QBLOCK, :] = jnp.zeros((A_QBLOCK, kcat_ref.shape[1]), kcat_ref.dtype)
        vcat_ref[0:A_QBLOCK, :] = jnp.zeros((A_QBLOCK, vcat_ref.shape[1]), vcat_ref.dtype)

    @pl.when(i > 0)
    def _():
        kcat_ref[0:A_QBLOCK, :] = kcat_ref[A_QBLOCK:2 * A_QBLOCK, :]
        vcat_ref[0:A_QBLOCK, :] = vcat_ref[A_QBLOCK:2 * A_QBLOCK, :]

    kcat_ref[A_QBLOCK:2 * A_QBLOCK, :] = kc_ref[0]
    vcat_ref[A_QBLOCK:2 * A_QBLOCK, :] = vc_ref[0]

    def run(first_block):
        for h in range(heads):
            sl = slice(h * dh, (h + 1) * dh)
            for t in range(A_QBLOCK // A_QTILE):
                q = q_ref[0, t * A_QTILE:(t + 1) * A_QTILE, sl]
                k = kcat_ref[t * A_QTILE:t * A_QTILE + A_KTILE, sl]
                v = vcat_ref[t * A_QTILE:t * A_QTILE + A_KTILE, sl]
                s = lax.dot_general(q, k, (((1,), (1,)), ((), ())),
                                    preferred_element_type=jnp.float32)
                s = s * scale + bias_ref[h]
                if first_block:
                    col = lax.broadcasted_iota(jnp.int32, s.shape, 1)
                    s = jnp.where(col >= A_QBLOCK - t * A_QTILE, s, NEG_INF)
                m = jnp.max(s, axis=-1, keepdims=True)
                p = jnp.exp(s - m)
                l = jnp.sum(p, axis=-1, keepdims=True)
                o = jnp.dot(p.astype(jnp.bfloat16), v, preferred_element_type=jnp.float32)
                o_ref[0, t * A_QTILE:(t + 1) * A_QTILE, sl] = (o / l).astype(o_ref.dtype)

    @pl.when(i == 0)
    def _():
        run(True)

    @pl.when(i > 0)
    def _():
        run(False)


def _band_bias_table(rel_bias):
    band = (A_LEFT_CHUNKS + 1) * A_CHUNK
    pad = A_LEFT_CHUNKS * A_CHUNK
    dist = pad + jnp.arange(A_CHUNK)[:, None] - jnp.arange(band)[None, :]
    idx = jnp.clip(dist, -A_REL_CLIP, A_REL_CLIP) + A_REL_CLIP
    bias = rel_bias.astype(jnp.float32)[:, idx]
    neg = jnp.full((rel_bias.shape[0], A_CHUNK, A_CHUNK), NEG_INF, jnp.float32)
    first = jnp.concatenate([bias, neg], axis=2)
    second = jnp.concatenate([neg, bias], axis=2)
    return jnp.concatenate([first, second], axis=1)


def chunk_attention(proj, rel_bias, *, batch, seq):
    heads = A_HEADS
    e = proj.shape[1] // 4
    dh = e // heads
    proj3 = proj.reshape(batch, seq, 4 * e)
    table = _band_bias_table(rel_bias)
    blk = (1, A_QBLOCK, e)
    out = pl.pallas_call(
        functools.partial(_chunk_attn_body, heads=heads, dh=dh),
        grid=(batch, seq // A_QBLOCK),
        in_specs=[pl.BlockSpec(blk, lambda b, i: (b, i, 0)),
                  pl.BlockSpec(blk, lambda b, i: (b, i, 1)),
                  pl.BlockSpec(blk, lambda b, i: (b, i, 2)),
                  pl.BlockSpec(table.shape, lambda b, i: (0, 0, 0))],
        out_specs=pl.BlockSpec(blk, lambda b, i: (b, i, 0)),
        out_shape=jax.ShapeDtypeStruct((batch, seq, e), jnp.bfloat16),
        scratch_shapes=[pltpu.VMEM((2 * A_QBLOCK, e), jnp.bfloat16),
                        pltpu.VMEM((2 * A_QBLOCK, e), jnp.bfloat16)],
        compiler_params=_params("parallel", "arbitrary"),
        name="chunk_attn",
    )(proj3, proj3, proj3, table)
    return out.reshape(batch * seq, e)


def _lru_body(xr_ref, cw_ref, cb_ref, wg_ref, ba_ref, bx_ref, lam_ref, o_ref,
              xbuf_ref, y_ref, a_ref, u_ref, h_ref, *, groups, gw):
    ts = o_ref.shape[1]

    @pl.when(pl.program_id(1) == 0)
    def _():
        xbuf_ref[0:LRU_ROWS, :] = jnp.zeros((LRU_ROWS, xbuf_ref.shape[1]), jnp.float32)
        h_ref[...] = jnp.zeros(h_ref.shape, jnp.float32)

    xbuf_ref[LRU_ROWS:LRU_ROWS + ts, :] = xr_ref[0].astype(jnp.float32)
    y = cb_ref[...] + cw_ref[CONV_WIDTH - 1:CONV_WIDTH, :] * xbuf_ref[LRU_ROWS:LRU_ROWS + ts, :]
    for tap in range(CONV_WIDTH - 1):
        off = LRU_ROWS - (CONV_WIDTH - 1) + tap
        y = y + cw_ref[tap:tap + 1, :] * xbuf_ref[off:off + ts, :]
    y_ref[...] = y
    xbuf_ref[0:LRU_ROWS, :] = xbuf_ref[ts:ts + LRU_ROWS, :]

    for gi in range(groups):
        cs = slice(gi * gw, (gi + 1) * gw)
        yg = y_ref[:, cs]
        pre = jnp.dot(yg.astype(jnp.bfloat16), wg_ref[gi], preferred_element_type=jnp.float32)
        r = jax.nn.sigmoid(pre[:, :gw] + ba_ref[:, cs])
        ig = jax.nn.sigmoid(pre[:, gw:] + bx_ref[:, cs])
        lam = lam_ref[:, cs]
        softplus_neg_lam = jnp.maximum(-lam, 0.0) + jnp.log1p(jnp.exp(-jnp.abs(lam)))
        log_a = (-RG_C * softplus_neg_lam) * r
        a = jnp.exp(log_a)
        mult = jnp.sqrt(-jnp.tanh(log_a) * (1.0 + a * a))
        a_ref[:, cs] = a
        u_ref[:, cs] = mult * (ig * yg)

    row = lax.broadcasted_iota(jnp.int32, (LRU_ROWS, gw), 0)
    for gi in range(groups):
        cs = slice(gi * gw, (gi + 1) * gw)

        def group_step(g, h_prev):
            rs = pl.ds(pl.multiple_of(g * LRU_ROWS, LRU_ROWS), LRU_ROWS)
            a = a_ref[rs, cs]
            u = u_ref[rs, cs]
            for d in (1, 2, 4):
                a_sh = jnp.where(row >= d, pltpu.roll(a, d, 0), 1.0)
                u_sh = jnp.where(row >= d, pltpu.roll(u, d, 0), 0.0)
                u = a * u_sh + u
                a = a * a_sh
            h = a * h_prev + u
            o_ref[0, rs, cs] = h.astype(o_ref.dtype)
            return jnp.broadcast_to(h[LRU_ROWS - 1:LRU_ROWS, :], (LRU_ROWS, gw))

        h_last = lax.fori_loop(0, ts // LRU_ROWS, group_step, h_ref[:, cs])
        h_ref[:, cs] = h_last


def _gate_group_weights(w_a, w_x):
    nb, bs, _ = w_a.shape
    groups = nb // RG_GROUP_BLOCKS
    gw = RG_GROUP_BLOCKS * bs
    eye = jnp.eye(RG_GROUP_BLOCKS, dtype=w_a.dtype)

    def block_diag(w):
        w = w.reshape(groups, RG_GROUP_BLOCKS, bs, bs)
        return jnp.einsum('gnde,nm->gndme', w, eye).reshape(groups, gw, gw)

    return jnp.concatenate([block_diag(w_a), block_diag(w_x)], axis=2).astype(jnp.bfloat16)


def rg_lru(proj, conv_w, conv_b, w_a, b_a, w_x, b_x, lam, *, batch, seq):
    width = proj.shape[1] // 2
    gw = RG_GROUP_BLOCKS * (width // RG_BLOCKS)
    groups = width // gw
    ts = min(LRU_TS, seq)
    assert seq % ts == 0 and gw % 128 == 0
    proj3 = proj.reshape(batch, seq, 2 * width)
    wg = _gate_group_weights(w_a, w_x)
    vec = lambda v: v.reshape(1, width).astype(jnp.float32)
    const2 = lambda b, s: (0, 0)
    out = pl.pallas_call(
        functools.partial(_lru_body, groups=groups, gw=gw),
        grid=(batch, seq // ts),
        in_specs=[pl.BlockSpec((1, ts, width), lambda b, s: (b, s, 0)),
                  pl.BlockSpec((CONV_WIDTH, width), const2),
                  pl.BlockSpec((1, width), const2),
                  pl.BlockSpec(wg.shape, lambda b, s: (0, 0, 0)),
                  pl.BlockSpec((1, width), const2),
                  pl.BlockSpec((1, width), const2),
                  pl.BlockSpec((1, width), const2)],
        out_specs=pl.BlockSpec((1, ts, width), lambda b, s: (b, s, 0)),
        out_shape=jax.ShapeDtypeStruct((batch, seq, width), jnp.bfloat16),
        scratch_shapes=[pltpu.VMEM((ts + LRU_ROWS, width), jnp.float32),
                        pltpu.VMEM((ts, width), jnp.float32),
                        pltpu.VMEM((ts, width), jnp.float32),
                        pltpu.VMEM((ts, width), jnp.float32),
                        pltpu.VMEM((LRU_ROWS, width), jnp.float32)],
        compiler_params=_params("parallel", "arbitrary"),
        name="rg_lru",
    )(proj3, conv_w.astype(jnp.float32), vec(conv_b), wg, vec(b_a), vec(b_x), vec(lam))
    return out.reshape(batch * seq, width)


def _forget_cumsum_body(f_ref, fb_ref, o_ref):
    z = f_ref[0] + fb_ref[...]
    c = jnp.minimum(z, 0.0) - jnp.log1p(jnp.exp(-jnp.abs(z)))
    n = c.shape[0]
    row = lax.broadcasted_iota(jnp.int32, c.shape, 0)
    d = 1
    while d < n:
        c = c + jnp.where(row >= d, pltpu.roll(c, d, 0), 0.0)
        d *= 2
    o_ref[0] = c


def forget_cumsum(f_logit, f_bias_padded, *, batch, seq):
    f3 = f_logit.reshape(batch, seq, F_PAD)
    return pl.pallas_call(
        _forget_cumsum_body,
        grid=(batch,),
        in_specs=[pl.BlockSpec((1, seq, F_PAD), lambda b: (b, 0, 0)),
                  pl.BlockSpec((1, F_PAD), lambda b: (0, 0))],
        out_specs=pl.BlockSpec((1, seq, F_PAD), lambda b: (b, 0, 0)),
        out_shape=jax.ShapeDtypeStruct((batch, seq, F_PAD), jnp.float32),
        compiler_params=_params("parallel"),
        name="forget_cumsum",
    )(f3, f_bias_padded)


def _fox_body(q_ref, k_ref, v_ref, cq_ref, ck_ref, o_ref, m_ref, l_ref, acc_ref, *, heads, dh):
    i = pl.program_id(1)
    tq, tk = FOX_TQ, FOX_TK
    scale = dh ** -0.5

    for h in range(heads):
        sl = slice(h * dh, (h + 1) * dh)
        q = q_ref[0, :, sl]
        cq = cq_ref[0, :, h:h + 1]
        m_ref[...] = jnp.full(m_ref.shape, NEG_INF, jnp.float32)
        l_ref[...] = jnp.zeros(l_ref.shape, jnp.float32)
        acc_ref[...] = jnp.zeros(acc_ref.shape, jnp.float32)

        def kv_step(j, diagonal):
            rows = pl.ds(pl.multiple_of(j * tk, tk), tk)
            k = k_ref[0, rows, sl]
            v = v_ref[0, rows, sl]
            s = lax.dot_general(q, k, (((1,), (1,)), ((), ())),
                                preferred_element_type=jnp.float32)
            s = s * scale + (cq - ck_ref[0, h, pl.ds(j, 1), :])
            if diagonal:
                r = lax.broadcasted_iota(jnp.int32, s.shape, 0)
                c = lax.broadcasted_iota(jnp.int32, s.shape, 1)
                s = jnp.where(c <= r, s, NEG_INF)
            m_prev = m_ref[...]
            m_new = jnp.maximum(m_prev, jnp.max(s, axis=-1, keepdims=True))
            alpha = jnp.exp(m_prev - m_new)
            p = jnp.exp(s - m_new)
            l_ref[...] = alpha * l_ref[...] + jnp.sum(p, axis=-1, keepdims=True)
            acc_ref[...] = alpha * acc_ref[...] + jnp.dot(p.astype(jnp.bfloat16), v,
                                                          preferred_element_type=jnp.float32)
            m_ref[...] = m_new

        def body(j, carry):
            kv_step(j, False)
            return carry

        lax.fori_loop(0, i, body, 0)
        kv_step(i, True)
        o_ref[0, :, sl] = (acc_ref[...] / l_ref[...]).astype(o_ref.dtype)


def forgetting_attention(proj, cum, *, batch, seq):
    heads = C_HEADS
    e = proj.shape[1] // 4
    dh = e // heads
    assert FOX_TQ == FOX_TK and seq % FOX_TQ == 0
    nkv = seq // FOX_TK
    proj3 = proj.reshape(batch, seq, 4 * e)
    cum_keys = cum[:, :, :heads].transpose(0, 2, 1).reshape(batch, heads, nkv, FOX_TK)
    out = pl.pallas_call(
        functools.partial(_fox_body, heads=heads, dh=dh),
        grid=(batch, seq // FOX_TQ),
        in_specs=[pl.BlockSpec((1, FOX_TQ, e), lambda b, i: (b, i, 0)),
                  pl.BlockSpec((1, seq, e), lambda b, i: (b, 0, 1)),
                  pl.BlockSpec((1, seq, e), lambda b, i: (b, 0, 2)),
                  pl.BlockSpec((1, FOX_TQ, F_PAD), lambda b, i: (b, i, 0)),
                  pl.BlockSpec((1, heads, nkv, FOX_TK), lambda b, i: (b, 0, 0, 0))],
        out_specs=pl.BlockSpec((1, FOX_TQ, e), lambda b, i: (b, i, 0)),
        out_shape=jax.ShapeDtypeStruct((batch, seq, e), jnp.bfloat16),
        scratch_shapes=[pltpu.VMEM((FOX_TQ, 1), jnp.float32),
                        pltpu.VMEM((FOX_TQ, 1), jnp.float32),
                        pltpu.VMEM((FOX_TQ, dh), jnp.float32)],
        compiler_params=_params("parallel", "arbitrary"),
        name="fox_attn",
    )(proj3, proj3, proj3, cum, cum_keys)
    return out.reshape(batch * seq, e)


def kernel(x, norm_pre, norm_post, a_w_in, a_rel_bias, a_w_out, b_w_in, b_conv_w, b_conv_b,
           b_gate_a_w, b_gate_a_b, b_gate_x_w, b_gate_x_b, b_lambda, b_w_out,
           c_w_in, c_f_bias, c_w_out):
    batch, seq, d = x.shape
    depth = norm_pre.shape[0]
    bf = lambda w: w.astype(jnp.bfloat16)
    xs = x.reshape(batch * seq, d)
    for i in range(depth):
        m, j = i % 3, i // 3
        if m == 0:
            proj = norm_proj(xs, norm_pre[i], bf(a_w_in[j]))
            o = chunk_attention(proj, a_rel_bias[j], batch=batch, seq=seq)
            xs = out_proj(o, proj, 3, bf(a_w_out[j]), xs, norm_post[i])
        elif m == 1:
            proj = norm_proj(xs, norm_pre[i], bf(b_w_in[j]))
            hs = rg_lru(proj, b_conv_w[j], b_conv_b[j], b_gate_a_w[j], b_gate_a_b[j],
                        b_gate_x_w[j], b_gate_x_b[j], b_lambda[j], batch=batch, seq=seq)
            xs = out_proj(hs, proj, 1, bf(b_w_out[j]), xs, norm_post[i])
        else:
            e4 = 4 * (c_w_out.shape[1])
            w_main = bf(c_w_in[j][:, :e4])
            n_f = c_w_in.shape[2] - e4
            w_f = jnp.pad(bf(c_w_in[j][:, e4:]), ((0, 0), (0, F_PAD - n_f)))
            f_bias = jnp.pad(c_f_bias[j].astype(jnp.float32), (0, F_PAD - n_f)).reshape(1, F_PAD)
            proj, f_logit = norm_proj(xs, norm_pre[i], w_main, w_f)
            cum = forget_cumsum(f_logit, f_bias, batch=batch, seq=seq)
            o = forgetting_attention(proj, cum, batch=batch, seq=seq)
            xs = out_proj(o, proj, 3, bf(c_w_out[j]), xs, norm_post[i])
    return xs.reshape(batch, seq, d)
```

```python
import functools

import jax
import jax.numpy as jnp
from jax import lax
from jax.experimental import pallas as pl
from jax.experimental.pallas import tpu as pltpu

RMS_EPS = 1e-6
NEG_INF = -1e30
LANES = 128

A_HEADS = 16
A_CHUNK = 64
A_LEFT_CHUNKS = 8
A_REL_CLIP = 256
A_QTILE = 2 * A_CHUNK
A_LEFT = A_LEFT_CHUNKS * A_CHUNK
A_KTILE = A_LEFT + A_QTILE

RG_BLOCKS = 16
RG_GROUP_BLOCKS = 4
CONV_WIDTH = 4
RG_C = 8.0
LRU_TS = 256
LRU_ROWS = 8

C_HEADS = 16
FOX_T = 512
N_SPLIT = 3

HEADS_PER_STEP = 2
VMEM_LIMIT_BYTES = 56 * 1024 * 1024


def _params(*sem):
    return pltpu.CompilerParams(dimension_semantics=sem, vmem_limit_bytes=VMEM_LIMIT_BYTES)


def _lane_concat(ref, n):
    return jnp.concatenate([ref[c] for c in range(n)], axis=1)


def _norm_proj_body(x_ref, gam_ref, w_ref, *rest, has_extra):
    if has_extra:
        w2_ref, o_ref, o2_ref, xn_ref = rest
    else:
        o_ref, xn_ref = rest

    @pl.when(pl.program_id(1) == 0)
    def _():
        x = x_ref[...]
        ms = jnp.mean(x * x, axis=-1, keepdims=True)
        xn_ref[...] = (x * lax.rsqrt(ms + RMS_EPS) * gam_ref[...]).astype(jnp.bfloat16)
        if has_extra:
            o2_ref[...] = jnp.dot(xn_ref[...], w2_ref[...], preferred_element_type=jnp.float32)

    res = jnp.dot(xn_ref[...], w_ref[...], preferred_element_type=jnp.float32)
    for c in range(o_ref.shape[0]):
        o_ref[c] = res[:, c * LANES:(c + 1) * LANES].astype(o_ref.dtype)


def norm_proj(x, gamma, w, w2=None, *, tm=1024, tn=512):
    m, d = x.shape
    n = w.shape[1]
    tm, tn = min(tm, m), min(tn, n)
    assert m % tm == 0 and n % tn == 0 and tn % LANES == 0
    in_specs = [pl.BlockSpec((tm, d), lambda i, j: (i, 0)),
                pl.BlockSpec((1, d), lambda i, j: (0, 0)),
                pl.BlockSpec((d, tn), lambda i, j: (0, j))]
    out_shape = [jax.ShapeDtypeStruct((n // LANES, m, LANES), jnp.bfloat16)]
    out_specs = [pl.BlockSpec((tn // LANES, tm, LANES), lambda i, j: (j, i, 0))]
    args = [x, gamma.reshape(1, d), w]
    if w2 is not None:
        in_specs.append(pl.BlockSpec((d, w2.shape[1]), lambda i, j: (0, 0)))
        out_shape.append(jax.ShapeDtypeStruct((m, w2.shape[1]), jnp.float32))
        out_specs.append(pl.BlockSpec((tm, w2.shape[1]), lambda i, j: (i, 0)))
        args.append(w2)
    out = pl.pallas_call(
        functools.partial(_norm_proj_body, has_extra=w2 is not None),
        grid=(m // tm, n // tn),
        in_specs=in_specs, out_specs=out_specs, out_shape=out_shape,
        scratch_shapes=[pltpu.VMEM((tm, d), jnp.bfloat16)],
        compiler_params=_params("parallel", "arbitrary"),
        name="norm_proj",
    )(*args)
    return out if w2 is not None else out[0]


def _out_proj_body(a_ref, g_ref, w_ref, x_ref, gam_ref, o_ref):
    ns = a_ref.shape[0]
    g = _lane_concat(g_ref, ns).astype(jnp.float32)
    a = _lane_concat(a_ref, ns).astype(jnp.float32)
    a = (a * (g * jax.nn.sigmoid(g))).astype(jnp.bfloat16)
    y = jnp.dot(a, w_ref[...], preferred_element_type=jnp.float32)
    ms = jnp.mean(y * y, axis=-1, keepdims=True)
    o_ref[...] = x_ref[...] + y * lax.rsqrt(ms + RMS_EPS) * gam_ref[...]


def out_proj(a, proj, g_block, w, x, gamma, *, tm=256):
    ns, m, _ = a.shape
    e, d = w.shape
    assert ns * LANES == e
    tm = min(tm, m)
    assert m % tm == 0
    return pl.pallas_call(
        _out_proj_body,
        grid=(m // tm,),
        in_specs=[pl.BlockSpec((ns, tm, LANES), lambda i: (0, i, 0)),
                  pl.BlockSpec((ns, tm, LANES), lambda i: (g_block, i, 0)),
                  pl.BlockSpec((e, d), lambda i: (0, 0)),
                  pl.BlockSpec((tm, d), lambda i: (i, 0)),
                  pl.BlockSpec((1, d), lambda i: (0, 0))],
        out_specs=pl.BlockSpec((tm, d), lambda i: (i, 0)),
        out_shape=jax.ShapeDtypeStruct((m, d), jnp.float32),
        compiler_params=_params("parallel"),
        name="out_proj",
    )(a, proj, w, x, gamma.reshape(1, d))


def _chunk_attn_body(q_ref, k_ref, v_ref, bias_ref, o_ref):
    hp, seq, _ = q_ref.shape

    def tile(h, q_rows, k_rows, bias):
        q = q_ref[h, q_rows, :]
        k = k_ref[h, k_rows, :]
        v = v_ref[h, k_rows, :]
        s = lax.dot_general(q, k, (((1,), (1,)), ((), ())), preferred_element_type=jnp.float32)
        s = s + bias
        m = jnp.max(s, axis=-1, keepdims=True)
        p = jnp.exp(s - m)
        l = jnp.sum(p, axis=-1, keepdims=True)
        o = jnp.dot(p.astype(jnp.bfloat16), v, preferred_element_type=jnp.float32)
        o_ref[h, q_rows, :] = (o / l).astype(o_ref.dtype)

    for t in range(A_LEFT // A_QTILE):
        for h in range(hp):
            n_keys = (t + 1) * A_QTILE
            tile(h, pl.ds(t * A_QTILE, A_QTILE), pl.ds(0, n_keys),
                 bias_ref[h, :, A_KTILE - n_keys:])

    def step(t, carry):
        q0 = pl.multiple_of(t * A_QTILE, A_QTILE)
        for h in range(hp):
            k0 = pl.multiple_of(q0 - A_LEFT, A_QTILE)
            tile(h, pl.ds(q0, A_QTILE), pl.ds(k0, A_KTILE), bias_ref[h])
        return carry

    lax.fori_loop(A_LEFT // A_QTILE, seq // A_QTILE, step, 0)


def _band_bias_table(rel_bias):
    band = (A_LEFT_CHUNKS + 1) * A_CHUNK
    dist = A_LEFT + jnp.arange(A_CHUNK)[:, None] - jnp.arange(band)[None, :]
    idx = jnp.clip(dist, -A_REL_CLIP, A_REL_CLIP) + A_REL_CLIP
    bias = jnp.take(rel_bias.astype(jnp.float32), idx.reshape(-1), axis=1)
    bias = bias.reshape(rel_bias.shape[0], A_CHUNK, band)
    neg = jnp.full((rel_bias.shape[0], A_CHUNK, A_CHUNK), NEG_INF, jnp.float32)
    first = jnp.concatenate([bias, neg], axis=2)
    second = jnp.concatenate([neg, bias], axis=2)
    return jnp.concatenate([first, second], axis=1)


def chunk_attention(proj, rel_bias, *, batch, seq):
    heads = A_HEADS
    hp = HEADS_PER_STEP
    assert proj.shape[0] == 4 * heads and seq % A_QTILE == 0 and seq > A_LEFT
    table = _band_bias_table(rel_bias)
    blk = (hp, seq, LANES)
    nhb = heads // hp
    return pl.pallas_call(
        _chunk_attn_body,
        grid=(batch, nhb),
        in_specs=[pl.BlockSpec(blk, lambda b, h: (h, b, 0)),
                  pl.BlockSpec(blk, lambda b, h: (nhb + h, b, 0)),
                  pl.BlockSpec(blk, lambda b, h: (2 * nhb + h, b, 0)),
                  pl.BlockSpec((hp, A_QTILE, A_KTILE), lambda b, h: (h, 0, 0))],
        out_specs=pl.BlockSpec(blk, lambda b, h: (h, b, 0)),
        out_shape=jax.ShapeDtypeStruct((heads, batch * seq, LANES), jnp.bfloat16),
        compiler_params=_params("parallel", "parallel"),
        name="chunk_attn",
    )(proj, proj, proj, table)


def _lru_body(xr_ref, cw_ref, cb_ref, wg_ref, ba_ref, bx_ref, lam_ref, o_ref,
              xbuf_ref, y_ref, a_ref, u_ref, h_ref, *, groups, gw):
    ns, ts, _ = o_ref.shape
    spg = gw // LANES

    @pl.when(pl.program_id(1) == 0)
    def _():
        xbuf_ref[0:LRU_ROWS, :] = jnp.zeros((LRU_ROWS, xbuf_ref.shape[1]), jnp.float32)
        h_ref[...] = jnp.zeros(h_ref.shape, jnp.float32)

    for c in range(ns):
        xbuf_ref[LRU_ROWS:LRU_ROWS + ts, c * LANES:(c + 1) * LANES] = xr_ref[c].astype(jnp.float32)
    y = cb_ref[...] + cw_ref[CONV_WIDTH - 1:CONV_WIDTH, :] * xbuf_ref[LRU_ROWS:LRU_ROWS + ts, :]
    for tap in range(CONV_WIDTH - 1):
        off = LRU_ROWS - (CONV_WIDTH - 1) + tap
        y = y + cw_ref[tap:tap + 1, :] * xbuf_ref[off:off + ts, :]
    y_ref[...] = y
    xbuf_ref[0:LRU_ROWS, :] = xbuf_ref[ts:ts + LRU_ROWS, :]

    for gi in range(groups):
        cs = slice(gi * gw, (gi + 1) * gw)
        yg = y_ref[:, cs]
        pre = jnp.dot(yg.astype(jnp.bfloat16), wg_ref[gi], preferred_element_type=jnp.float32)
        r = jax.nn.sigmoid(pre[:, :gw] + ba_ref[:, cs])
        ig = jax.nn.sigmoid(pre[:, gw:] + bx_ref[:, cs])
        lam = lam_ref[:, cs]
        softplus_neg_lam = jnp.maximum(-lam, 0.0) + jnp.log1p(jnp.exp(-jnp.abs(lam)))
        log_a = (-RG_C * softplus_neg_lam) * r
        a = jnp.exp(log_a)
        mult = jnp.sqrt(-jnp.tanh(log_a) * (1.0 + a * a))
        a_ref[:, cs] = a
        u_ref[:, cs] = mult * (ig * yg)

    row = lax.broadcasted_iota(jnp.int32, (LRU_ROWS, gw), 0)
    for gi in range(groups):
        cs = slice(gi * gw, (gi + 1) * gw)

        def group_step(g, h_prev):
            rs = pl.ds(pl.multiple_of(g * LRU_ROWS, LRU_ROWS), LRU_ROWS)
            a = a_ref[rs, cs]
            u = u_ref[rs, cs]
            for d in (1, 2, 4):
                a_sh = jnp.where(row >= d, pltpu.roll(a, d, 0), 1.0)
                u_sh = jnp.where(row >= d, pltpu.roll(u, d, 0), 0.0)
                u = a * u_sh + u
                a = a * a_sh
            h = a * h_prev + u
            for c in range(spg):
                o_ref[gi * spg + c, rs, :] = h[:, c * LANES:(c + 1) * LANES].astype(o_ref.dtype)
            return jnp.broadcast_to(h[LRU_ROWS - 1:LRU_ROWS, :], (LRU_ROWS, gw))

        h_last = lax.fori_loop(0, ts // LRU_ROWS, group_step, h_ref[:, cs])
        h_ref[:, cs] = h_last


def _gate_group_weights(w_a, w_x):
    nb, bs, _ = w_a.shape
    groups = nb // RG_GROUP_BLOCKS
    gw = RG_GROUP_BLOCKS * bs
    eye = jnp.eye(RG_GROUP_BLOCKS, dtype=w_a.dtype)

    def block_diag(w):
        w = w.reshape(groups, RG_GROUP_BLOCKS, bs, bs)
        return (w[:, :, :, None, :] * eye[None, :, None, :, None]).reshape(groups, gw, gw)

    return jnp.concatenate([block_diag(w_a), block_diag(w_x)], axis=2).astype(jnp.bfloat16)


def rg_lru(proj, conv_w, conv_b, w_a, b_a, w_x, b_x, lam, *, batch, seq):
    ns = proj.shape[0] // 2
    width = ns * LANES
    gw = RG_GROUP_BLOCKS * (width // RG_BLOCKS)
    groups = width // gw
    ts = min(LRU_TS, seq)
    nts = seq // ts
    assert seq % ts == 0 and gw % LANES == 0
    wg = _gate_group_weights(w_a, w_x)
    vec = lambda v: v.reshape(1, width).astype(jnp.float32)
    const2 = lambda b, s: (0, 0)
    return pl.pallas_call(
        functools.partial(_lru_body, groups=groups, gw=gw),
        grid=(batch, nts),
        in_specs=[pl.BlockSpec((ns, ts, LANES), lambda b, s: (0, b * nts + s, 0)),
                  pl.BlockSpec((CONV_WIDTH, width), const2),
                  pl.BlockSpec((1, width), const2),
                  pl.BlockSpec(wg.shape, lambda b, s: (0, 0, 0)),
                  pl.BlockSpec((1, width), const2),
                  pl.BlockSpec((1, width), const2),
                  pl.BlockSpec((1, width), const2)],
        out_specs=pl.BlockSpec((ns, ts, LANES), lambda b, s: (0, b * nts + s, 0)),
        out_shape=jax.ShapeDtypeStruct((ns, batch * seq, LANES), jnp.bfloat16),
        scratch_shapes=[pltpu.VMEM((ts + LRU_ROWS, width), jnp.float32),
                        pltpu.VMEM((ts, width), jnp.float32),
                        pltpu.VMEM((ts, width), jnp.float32),
                        pltpu.VMEM((ts, width), jnp.float32),
                        pltpu.VMEM((LRU_ROWS, width), jnp.float32)],
        compiler_params=_params("parallel", "arbitrary"),
        name="rg_lru",
    )(proj, conv_w.astype(jnp.float32), vec(conv_b), wg, vec(b_a), vec(b_x), vec(lam))


def _split3(x):
    hi = x.astype(jnp.bfloat16).astype(jnp.float32)
    r1 = x - hi
    mid = r1.astype(jnp.bfloat16).astype(jnp.float32)
    lo = (r1 - mid).astype(jnp.bfloat16).astype(jnp.float32)
    return hi, mid, lo


def _forget_gate_body(f_ref, fb_ref, qx_ref, kx_ref, *, heads):
    z = f_ref[0] + fb_ref[...]
    c = jnp.minimum(z, 0.0) - jnp.log1p(jnp.exp(-jnp.abs(z)))
    n = c.shape[0]
    row = lax.broadcasted_iota(jnp.int32, c.shape, 0)
    d = 1
    while d < n:
        c = c + jnp.where(row >= d, pltpu.roll(c, d, 0), 0.0)
        d *= 2
    lane = lax.broadcasted_iota(jnp.int32, c.shape, 1)
    for h in range(heads):
        hi, mid, lo = _split3(jnp.broadcast_to(c[:, h:h + 1], c.shape))
        pieces = jnp.where(lane == 0, hi, jnp.where(lane == 1, mid, lo))
        ones_q = jnp.where(lane < 2 * N_SPLIT, 1.0, 0.0)
        qx_ref[h] = jnp.where(lane < N_SPLIT, pieces, ones_q).astype(qx_ref.dtype)
        neg = jnp.where(lane == N_SPLIT, -hi, jnp.where(lane == N_SPLIT + 1, -mid, -lo))
        zeros_k = jnp.where(lane < N_SPLIT, 1.0, 0.0)
        kx_ref[h] = jnp.where((lane >= N_SPLIT) & (lane < 2 * N_SPLIT), neg, zeros_k).astype(kx_ref.dtype)


def forget_gate_slabs(f_logit, f_bias_padded, *, batch, seq, heads):
    f3 = f_logit.reshape(batch, seq, LANES)
    shape = jax.ShapeDtypeStruct((heads, batch * seq, LANES), jnp.bfloat16)
    spec = pl.BlockSpec((heads, seq, LANES), lambda b: (0, b, 0))
    return pl.pallas_call(
        functools.partial(_forget_gate_body, heads=heads),
        grid=(batch,),
        in_specs=[pl.BlockSpec((1, seq, LANES), lambda b: (b, 0, 0)),
                  pl.BlockSpec((1, LANES), lambda b: (0, 0))],
        out_specs=[spec, spec],
        out_shape=[shape, shape],
        compiler_params=_params("parallel"),
        name="forget_gate",
    )(f3, f_bias_padded)


def _fox_body(q_ref, qx_ref, k_ref, kx_ref, v_ref, o_ref, m_ref, l_ref, acc_ref):
    hp, seq, dh = q_ref.shape
    t = FOX_T

    def kv_step(h, q_rows, j, diagonal):
        rows = pl.ds(pl.multiple_of(j * t, t), t)
        q_aug = jnp.concatenate([q_ref[h, q_rows, :], qx_ref[h, q_rows, :]], axis=1)
        k_aug = jnp.concatenate([k_ref[h, rows, :], kx_ref[h, rows, :]], axis=1)
        s = lax.dot_general(q_aug, k_aug, (((1,), (1,)), ((), ())),
                            preferred_element_type=jnp.float32)
        if diagonal:
            r = lax.broadcasted_iota(jnp.int32, s.shape, 0)
            c = lax.broadcasted_iota(jnp.int32, s.shape, 1)
            s = jnp.where(c <= r, s, NEG_INF)
        m_prev = m_ref[h]
        m_next = jnp.maximum(m_prev, jnp.max(s, axis=1, keepdims=True))
        alpha = jnp.exp(m_prev - m_next)
        p = jnp.exp(s - jnp.tile(m_next, (1, t // LANES)))
        l_ref[h] = alpha * l_ref[h] + jnp.sum(p, axis=1, keepdims=True)
        acc_ref[h] = alpha * acc_ref[h] + jnp.dot(p.astype(jnp.bfloat16), v_ref[h, rows, :],
                                                  preferred_element_type=jnp.float32)
        m_ref[h] = m_next

    def q_block(i, carry):
        q_rows = pl.ds(pl.multiple_of(i * t, t), t)
        m_ref[...] = jnp.full(m_ref.shape, NEG_INF, jnp.float32)
        l_ref[...] = jnp.zeros(l_ref.shape, jnp.float32)
        acc_ref[...] = jnp.zeros(acc_ref.shape, jnp.float32)

        def kv_block(j, c):
            for h in range(hp):
                kv_step(h, q_rows, j, False)
            return c

        lax.fori_loop(0, i, kv_block, 0)
        for h in range(hp):
            kv_step(h, q_rows, i, True)
        for h in range(hp):
            o_ref[h, q_rows, :] = (acc_ref[h] / l_ref[h]).astype(o_ref.dtype)
        return carry

    lax.fori_loop(0, seq // t, q_block, 0)


def forgetting_attention(proj, qx, kx, *, batch, seq):
    heads = C_HEADS
    hp = HEADS_PER_STEP
    assert proj.shape[0] == 4 * heads and seq % FOX_T == 0
    blk = (hp, seq, LANES)
    nhb = heads // hp
    return pl.pallas_call(
        _fox_body,
        grid=(batch, nhb),
        in_specs=[pl.BlockSpec(blk, lambda b, h: (h, b, 0)),
                  pl.BlockSpec(blk, lambda b, h: (h, b, 0)),
                  pl.BlockSpec(blk, lambda b, h: (nhb + h, b, 0)),
                  pl.BlockSpec(blk, lambda b, h: (h, b, 0)),
                  pl.BlockSpec(blk, lambda b, h: (2 * nhb + h, b, 0))],
        out_specs=pl.BlockSpec(blk, lambda b, h: (h, b, 0)),
        out_shape=jax.ShapeDtypeStruct((heads, batch * seq, LANES), jnp.bfloat16),
        scratch_shapes=[pltpu.VMEM((hp, FOX_T, LANES), jnp.float32),
                        pltpu.VMEM((hp, FOX_T, LANES), jnp.float32),
                        pltpu.VMEM((hp, FOX_T, LANES), jnp.float32)],
        compiler_params=_params("parallel", "parallel"),
        name="fox_attn",
    )(proj, qx, proj, kx, proj)


def _scale_q_columns(w, width, dh):
    scale = jnp.where(jnp.arange(w.shape[1]) < width, dh ** -0.5, 1.0).astype(w.dtype)
    return w * scale


def kernel(x, norm_pre, norm_post, a_w_in, a_rel_bias, a_w_out, b_w_in, b_conv_w, b_conv_b,
           b_gate_a_w, b_gate_a_b, b_gate_x_w, b_gate_x_b, b_lambda, b_w_out,
           c_w_in, c_f_bias, c_w_out):
    batch, seq, d = x.shape
    depth = norm_pre.shape[0]
    bf = lambda w: w.astype(jnp.bfloat16)
    xs = x.reshape(batch * seq, d)
    for i in range(depth):
        m, j = i % 3, i // 3
        if m == 0:
            e = a_w_out.shape[1]
            w_in = bf(_scale_q_columns(a_w_in[j], e, e // A_HEADS))
            proj = norm_proj(xs, norm_pre[i], w_in)
            o = chunk_attention(proj, a_rel_bias[j], batch=batch, seq=seq)
            xs = out_proj(o, proj, 3, bf(a_w_out[j]), xs, norm_post[i])
        elif m == 1:
            proj = norm_proj(xs, norm_pre[i], bf(b_w_in[j]))
            hs = rg_lru(proj, b_conv_w[j], b_conv_b[j], b_gate_a_w[j], b_gate_a_b[j],
                        b_gate_x_w[j], b_gate_x_b[j], b_lambda[j], batch=batch, seq=seq)
            xs = out_proj(hs, proj, 1, bf(b_w_out[j]), xs, norm_post[i])
        else:
            e = c_w_out.shape[1]
            n_f = c_w_in.shape[2] - 4 * e
            w_main = bf(_scale_q_columns(c_w_in[j][:, :4 * e], e, e // C_HEADS))
            w_f = jnp.pad(bf(c_w_in[j][:, 4 * e:]), ((0, 0), (0, LANES - n_f)))
            f_bias = jnp.pad(c_f_bias[j].astype(jnp.float32), (0, LANES - n_f)).reshape(1, LANES)
            proj, f_logit = norm_proj(xs, norm_pre[i], w_main, w_f)
            qx, kx = forget_gate_slabs(f_logit, f_bias, batch=batch, seq=seq, heads=n_f)
            o = forgetting_attention(proj, qx, kx, batch=batch, seq=seq)
            xs = out_proj(o, proj, 3, bf(c_w_out[j]), xs, norm_post[i])
    return xs.reshape(batch, seq, d)
```

```python
import functools

import jax
import jax.numpy as jnp
import numpy as np
from jax import lax
from jax.experimental import pallas as pl
from jax.experimental.pallas import tpu as pltpu

RMS_EPS = 1e-6
NEG_INF = -1e30
LANES = 128

A_HEADS = 16
A_CHUNK = 64
A_LEFT_CHUNKS = 8
A_REL_CLIP = 256
A_HEADS_PER_STEP = 2
A_QTILE = 2 * A_CHUNK
A_LEFT = A_LEFT_CHUNKS * A_CHUNK
A_KTILE = A_LEFT + A_QTILE

RG_BLOCKS = 16
RG_GROUP_BLOCKS = 4
CONV_WIDTH = 4
RG_C = 8.0
LRU_TS = 256
LRU_ROWS = 8

C_HEADS = 16
FOX_T = 512
N_SPLIT = 3

HEADS_PER_STEP = 2
OUT_PROJ_ROWS = 256
VMEM_LIMIT_BYTES = 56 * 1024 * 1024


def _params(*sem):
    return pltpu.CompilerParams(dimension_semantics=sem, vmem_limit_bytes=VMEM_LIMIT_BYTES)


def _lane_concat(ref, n):
    return jnp.concatenate([ref[c] for c in range(n)], axis=1)


def _norm_proj_body(x_ref, gam_ref, w_ref, *rest, has_extra):
    if has_extra:
        w2_ref, o_ref, o2_ref, xn_ref = rest
    else:
        o_ref, xn_ref = rest

    @pl.when(pl.program_id(1) == 0)
    def _():
        x = x_ref[...]
        ms = jnp.mean(x * x, axis=-1, keepdims=True)
        xn_ref[...] = (x * lax.rsqrt(ms + RMS_EPS) * gam_ref[...]).astype(jnp.bfloat16)
        if has_extra:
            o2_ref[...] = jnp.dot(xn_ref[...], w2_ref[...], preferred_element_type=jnp.float32)

    res = jnp.dot(xn_ref[...], w_ref[...], preferred_element_type=jnp.float32)
    for c in range(o_ref.shape[0]):
        o_ref[c] = res[:, c * LANES:(c + 1) * LANES].astype(o_ref.dtype)


def norm_proj(x, gamma, w, w2=None, *, tm=1024, tn=1024):
    m, d = x.shape
    n = w.shape[1]
    tm, tn = min(tm, m), min(tn, n)
    assert m % tm == 0 and n % tn == 0 and tn % LANES == 0
    in_specs = [pl.BlockSpec((tm, d), lambda i, j: (i, 0)),
                pl.BlockSpec((1, d), lambda i, j: (0, 0)),
                pl.BlockSpec((d, tn), lambda i, j: (0, j))]
    out_shape = [jax.ShapeDtypeStruct((n // LANES, m, LANES), jnp.bfloat16)]
    out_specs = [pl.BlockSpec((tn // LANES, tm, LANES), lambda i, j: (j, i, 0))]
    args = [x, gamma.reshape(1, d), w]
    if w2 is not None:
        in_specs.append(pl.BlockSpec((d, w2.shape[1]), lambda i, j: (0, 0)))
        out_shape.append(jax.ShapeDtypeStruct((m, w2.shape[1]), jnp.float32))
        out_specs.append(pl.BlockSpec((tm, w2.shape[1]), lambda i, j: (i, 0)))
        args.append(w2)
    out = pl.pallas_call(
        functools.partial(_norm_proj_body, has_extra=w2 is not None),
        grid=(m // tm, n // tn),
        in_specs=in_specs, out_specs=out_specs, out_shape=out_shape,
        scratch_shapes=[pltpu.VMEM((tm, d), jnp.bfloat16)],
        compiler_params=_params("parallel", "arbitrary"),
        name="norm_proj",
    )(*args)
    return out if w2 is not None else out[0]


def _out_proj_body(a_ref, g_ref, w_ref, x_ref, gam_ref, o_ref):
    ns, tm, _ = a_ref.shape
    n_chunks = tm // OUT_PROJ_ROWS

    def gated(c):
        rs = slice(c * OUT_PROJ_ROWS, (c + 1) * OUT_PROJ_ROWS)
        g = jnp.concatenate([g_ref[s, rs, :] for s in range(ns)], axis=1).astype(jnp.float32)
        a = jnp.concatenate([a_ref[s, rs, :] for s in range(ns)], axis=1).astype(jnp.float32)
        return (a * (g * jax.nn.sigmoid(g))).astype(jnp.bfloat16)

    def finish(c, y):
        rs = slice(c * OUT_PROJ_ROWS, (c + 1) * OUT_PROJ_ROWS)
        ms = jnp.mean(y * y, axis=-1, keepdims=True)
        o_ref[rs, :] = x_ref[rs, :] + y * lax.rsqrt(ms + RMS_EPS) * gam_ref[...]

    a_next = gated(0)
    y_prev = None
    for c in range(n_chunks):
        a_cur = a_next
        if c + 1 < n_chunks:
            a_next = gated(c + 1)
        y = jnp.dot(a_cur, w_ref[...], preferred_element_type=jnp.float32)
        if y_prev is not None:
            finish(c - 1, y_prev)
        y_prev = y
    finish(n_chunks - 1, y_prev)


def out_proj(a, proj, g_block, w, x, gamma, *, tm=512):
    ns, m, _ = a.shape
    e, d = w.shape
    assert ns * LANES == e
    tm = min(tm, m)
    assert m % tm == 0 and tm % OUT_PROJ_ROWS == 0
    return pl.pallas_call(
        _out_proj_body,
        grid=(m // tm,),
        in_specs=[pl.BlockSpec((ns, tm, LANES), lambda i: (0, i, 0)),
                  pl.BlockSpec((ns, tm, LANES), lambda i: (g_block, i, 0)),
                  pl.BlockSpec((e, d), lambda i: (0, 0), pipeline_mode=pl.Buffered(1)),
                  pl.BlockSpec((tm, d), lambda i: (i, 0)),
                  pl.BlockSpec((1, d), lambda i: (0, 0))],
        out_specs=pl.BlockSpec((tm, d), lambda i: (i, 0)),
        out_shape=jax.ShapeDtypeStruct((m, d), jnp.float32),
        compiler_params=_params("parallel"),
        name="out_proj",
    )(a, proj, w, x, gamma.reshape(1, d))


def _chunk_attn_body(q_ref, k_ref, v_ref, bias_ref, o_ref, s_ref):
    hp, seq, _ = q_ref.shape
    n_tiles = seq // A_QTILE
    n_cut = A_LEFT // A_QTILE

    def rows(t):
        q0 = pl.multiple_of(t * A_QTILE, A_QTILE)
        k0 = pl.multiple_of(jnp.maximum(q0 - A_LEFT, 0), A_QTILE)
        return pl.ds(q0, A_QTILE), pl.ds(k0, A_KTILE)

    def scores(t, slot):
        q_rows, k_rows = rows(t)
        table = jnp.minimum(t, n_cut)
        for h in range(hp):
            s = lax.dot_general(q_ref[h, q_rows, :], k_ref[h, k_rows, :], (((1,), (1,)), ((), ())),
                                preferred_element_type=jnp.float32)
            s_ref[slot, h] = s + bias_ref[h, table]

    def finish(t, slot):
        q_rows, k_rows = rows(t)
        ps, ls = [], []
        for h in range(hp):
            s = s_ref[slot, h]
            p = jnp.exp(s - jnp.max(s, axis=-1, keepdims=True))
            ls.append(jnp.sum(p, axis=-1, keepdims=True))
            ps.append(p.astype(jnp.bfloat16))
        outs = [jnp.dot(ps[h], v_ref[h, k_rows, :], preferred_element_type=jnp.float32)
                for h in range(hp)]
        for h in range(hp):
            o_ref[h, q_rows, :] = (outs[h] / ls[h]).astype(o_ref.dtype)

    scores(0, 0)

    def step(i, carry):
        t = 2 * i
        scores(t + 1, 1)
        finish(t, 0)
        scores(jnp.minimum(t + 2, n_tiles - 1), 0)
        finish(t + 1, 1)
        return carry

    lax.fori_loop(0, n_tiles // 2, step, 0)


def _band_bias_tables(rel_bias):
    heads = rel_bias.shape[0]
    n_diag = A_QTILE + A_KTILE - 1
    qi = np.arange(A_QTILE)[:, None]
    j = np.arange(A_KTILE)[None, :]
    tables = []
    for offset in range(0, A_LEFT + 1, A_QTILE):
        dist = offset + np.arange(n_diag) - (A_KTILE - 1)
        idx = np.clip(dist, -A_REL_CLIP, A_REL_CLIP) + A_REL_CLIP
        u = rel_bias.astype(jnp.float32)[:, idx]
        flat = jnp.tile(u, (1, A_QTILE + 1))[:, :A_QTILE * (n_diag + 1)]
        table = flat.reshape(heads, A_QTILE, n_diag + 1)[:, :, :A_KTILE][:, :, ::-1]
        band_start = (qi // A_CHUNK) * A_CHUNK + offset - A_LEFT
        visible = (j >= band_start) & (j < band_start + (A_LEFT_CHUNKS + 1) * A_CHUNK)
        tables.append(jnp.where(visible[None], table, NEG_INF))
    return jnp.stack(tables, axis=1)


def chunk_attention(proj, rel_bias, *, batch, seq):
    heads = A_HEADS
    hp = A_HEADS_PER_STEP
    assert proj.shape[0] == 4 * heads and seq % (2 * A_QTILE) == 0 and seq >= A_KTILE
    tables = _band_bias_tables(rel_bias)
    blk = (hp, seq, LANES)
    nhb = heads // hp
    return pl.pallas_call(
        _chunk_attn_body,
        grid=(batch, nhb),
        in_specs=[pl.BlockSpec(blk, lambda b, h: (h, b, 0)),
                  pl.BlockSpec(blk, lambda b, h: (nhb + h, b, 0)),
                  pl.BlockSpec(blk, lambda b, h: (2 * nhb + h, b, 0)),
                  pl.BlockSpec((hp,) + tables.shape[1:], lambda b, h: (h, 0, 0, 0))],
        out_specs=pl.BlockSpec(blk, lambda b, h: (h, b, 0)),
        out_shape=jax.ShapeDtypeStruct((heads, batch * seq, LANES), jnp.bfloat16),
        scratch_shapes=[pltpu.VMEM((2, hp, A_QTILE, A_KTILE), jnp.float32)],
        compiler_params=_params("parallel", "parallel"),
        name="chunk_attn",
    )(proj, proj, proj, tables)


def _lru_body(xr_ref, cw_ref, cb_ref, wg_ref, ba_ref, bx_ref, lam_ref, o_ref,
              xbuf_ref, y_ref, a_ref, u_ref, h_ref, *, groups, gw):
    ns, ts, _ = o_ref.shape
    spg = gw // LANES

    @pl.when(pl.program_id(1) == 0)
    def _():
        xbuf_ref[0:LRU_ROWS, :] = jnp.zeros((LRU_ROWS, xbuf_ref.shape[1]), jnp.float32)
        h_ref[...] = jnp.zeros(h_ref.shape, jnp.float32)

    for c in range(ns):
        xbuf_ref[LRU_ROWS:LRU_ROWS + ts, c * LANES:(c + 1) * LANES] = xr_ref[c].astype(jnp.float32)
    y = cb_ref[...] + cw_ref[CONV_WIDTH - 1:CONV_WIDTH, :] * xbuf_ref[LRU_ROWS:LRU_ROWS + ts, :]
    for tap in range(CONV_WIDTH - 1):
        off = LRU_ROWS - (CONV_WIDTH - 1) + tap
        y = y + cw_ref[tap:tap + 1, :] * xbuf_ref[off:off + ts, :]
    y_ref[...] = y
    xbuf_ref[0:LRU_ROWS, :] = xbuf_ref[ts:ts + LRU_ROWS, :]

    for gi in range(groups):
        cs = slice(gi * gw, (gi + 1) * gw)
        yg = y_ref[:, cs]
        pre = jnp.dot(yg.astype(jnp.bfloat16), wg_ref[gi], preferred_element_type=jnp.float32)
        r = jax.nn.sigmoid(pre[:, :gw] + ba_ref[:, cs])
        ig = jax.nn.sigmoid(pre[:, gw:] + bx_ref[:, cs])
        lam = lam_ref[:, cs]
        softplus_neg_lam = jnp.maximum(-lam, 0.0) + jnp.log1p(jnp.exp(-jnp.abs(lam)))
        log_a = (-RG_C * softplus_neg_lam) * r
        a = jnp.exp(log_a)
        mult = jnp.sqrt(-jnp.tanh(log_a) * (1.0 + a * a))
        a_ref[:, cs] = a
        u_ref[:, cs] = mult * (ig * yg)

    row = lax.broadcasted_iota(jnp.int32, (LRU_ROWS, gw), 0)
    for gi in range(groups):
        cs = slice(gi * gw, (gi + 1) * gw)

        def group_step(g, h_prev):
            rs = pl.ds(pl.multiple_of(g * LRU_ROWS, LRU_ROWS), LRU_ROWS)
            a = a_ref[rs, cs]
            u = u_ref[rs, cs]
            for d in (1, 2, 4):
                a_sh = jnp.where(row >= d, pltpu.roll(a, d, 0), 1.0)
                u_sh = jnp.where(row >= d, pltpu.roll(u, d, 0), 0.0)
                u = a * u_sh + u
                a = a * a_sh
            h = a * h_prev + u
            for c in range(spg):
                o_ref[gi * spg + c, rs, :] = h[:, c * LANES:(c + 1) * LANES].astype(o_ref.dtype)
            return jnp.broadcast_to(h[LRU_ROWS - 1:LRU_ROWS, :], (LRU_ROWS, gw))

        h_last = lax.fori_loop(0, ts // LRU_ROWS, group_step, h_ref[:, cs])
        h_ref[:, cs] = h_last


def _gate_group_weights(w_a, w_x):
    nb, bs, _ = w_a.shape
    groups = nb // RG_GROUP_BLOCKS
    gw = RG_GROUP_BLOCKS * bs
    eye = jnp.eye(RG_GROUP_BLOCKS, dtype=w_a.dtype)

    def block_diag(w):
        w = w.reshape(groups, RG_GROUP_BLOCKS, bs, bs)
        return (w[:, :, :, None, :] * eye[None, :, None, :, None]).reshape(groups, gw, gw)

    return jnp.concatenate([block_diag(w_a), block_diag(w_x)], axis=2).astype(jnp.bfloat16)


def rg_lru(proj, conv_w, conv_b, w_a, b_a, w_x, b_x, lam, *, batch, seq):
    ns = proj.shape[0] // 2
    width = ns * LANES
    gw = RG_GROUP_BLOCKS * (width // RG_BLOCKS)
    groups = width // gw
    ts = min(LRU_TS, seq)
    nts = seq // ts
    assert seq % ts == 0 and gw % LANES == 0
    wg = _gate_group_weights(w_a, w_x)
    vec = lambda v: v.reshape(1, width).astype(jnp.float32)
    const2 = lambda b, s: (0, 0)
    return pl.pallas_call(
        functools.partial(_lru_body, groups=groups, gw=gw),
        grid=(batch, nts),
        in_specs=[pl.BlockSpec((ns, ts, LANES), lambda b, s: (0, b * nts + s, 0)),
                  pl.BlockSpec((CONV_WIDTH, width), const2),
                  pl.BlockSpec((1, width), const2),
                  pl.BlockSpec(wg.shape, lambda b, s: (0, 0, 0)),
                  pl.BlockSpec((1, width), const2),
                  pl.BlockSpec((1, width), const2),
                  pl.BlockSpec((1, width), const2)],
        out_specs=pl.BlockSpec((ns, ts, LANES), lambda b, s: (0, b * nts + s, 0)),
        out_shape=jax.ShapeDtypeStruct((ns, batch * seq, LANES), jnp.bfloat16),
        scratch_shapes=[pltpu.VMEM((ts + LRU_ROWS, width), jnp.float32),
                        pltpu.VMEM((ts, width), jnp.float32),
                        pltpu.VMEM((ts, width), jnp.float32),
                        pltpu.VMEM((ts, width), jnp.float32),
                        pltpu.VMEM((LRU_ROWS, width), jnp.float32)],
        compiler_params=_params("parallel", "arbitrary"),
        name="rg_lru",
    )(proj, conv_w.astype(jnp.float32), vec(conv_b), wg, vec(b_a), vec(b_x), vec(lam))


def _split3(x):
    hi = x.astype(jnp.bfloat16).astype(jnp.float32)
    r1 = x - hi
    mid = r1.astype(jnp.bfloat16).astype(jnp.float32)
    lo = (r1 - mid).astype(jnp.bfloat16).astype(jnp.float32)
    return hi, mid, lo


def _forget_gate_body(f_ref, fb_ref, qx_ref, kx_ref, *, heads):
    z = f_ref[0] + fb_ref[...]
    c = jnp.minimum(z, 0.0) - jnp.log1p(jnp.exp(-jnp.abs(z)))
    n = c.shape[0]
    row = lax.broadcasted_iota(jnp.int32, c.shape, 0)
    d = 1
    while d < n:
        c = c + jnp.where(row >= d, pltpu.roll(c, d, 0), 0.0)
        d *= 2
    lane = lax.broadcasted_iota(jnp.int32, c.shape, 1)
    for h in range(heads):
        hi, mid, lo = _split3(jnp.broadcast_to(c[:, h:h + 1], c.shape))
        pieces = jnp.where(lane == 0, hi, jnp.where(lane == 1, mid, lo))
        ones_q = jnp.where(lane < 2 * N_SPLIT, 1.0, 0.0)
        qx_ref[h] = jnp.where(lane < N_SPLIT, pieces, ones_q).astype(qx_ref.dtype)
        neg = jnp.where(lane == N_SPLIT, -hi, jnp.where(lane == N_SPLIT + 1, -mid, -lo))
        zeros_k = jnp.where(lane < N_SPLIT, 1.0, 0.0)
        kx_ref[h] = jnp.where((lane >= N_SPLIT) & (lane < 2 * N_SPLIT), neg, zeros_k).astype(kx_ref.dtype)


def forget_gate_slabs(f_logit, f_bias_padded, *, batch, seq, heads):
    f3 = f_logit.reshape(batch, seq, LANES)
    shape = jax.ShapeDtypeStruct((heads, batch * seq, LANES), jnp.bfloat16)
    spec = pl.BlockSpec((heads, seq, LANES), lambda b: (0, b, 0))
    return pl.pallas_call(
        functools.partial(_forget_gate_body, heads=heads),
        grid=(batch,),
        in_specs=[pl.BlockSpec((1, seq, LANES), lambda b: (b, 0, 0)),
                  pl.BlockSpec((1, LANES), lambda b: (0, 0))],
        out_specs=[spec, spec],
        out_shape=[shape, shape],
        compiler_params=_params("parallel"),
        name="forget_gate",
    )(f3, f_bias_padded)


def _fox_body(q_ref, qx_ref, k_ref, kx_ref, v_ref, o_ref, s_ref, m_ref, l_ref, acc_ref):
    hp, seq, dh = q_ref.shape
    t = FOX_T
    steps = [(i, j) for i in range(seq // t) for j in range(i + 1)]

    def rows(b):
        return slice(b * t, (b + 1) * t)

    def scores(n, slot):
        i, j = steps[n]
        for h in range(hp):
            q_aug = jnp.concatenate([q_ref[h, rows(i), :], qx_ref[h, rows(i), :]], axis=1)
            k_aug = jnp.concatenate([k_ref[h, rows(j), :], kx_ref[h, rows(j), :]], axis=1)
            s = lax.dot_general(q_aug, k_aug, (((1,), (1,)), ((), ())),
                                preferred_element_type=jnp.float32)
            if i == j:
                r = lax.broadcasted_iota(jnp.int32, s.shape, 0)
                c = lax.broadcasted_iota(jnp.int32, s.shape, 1)
                s = jnp.where(c <= r, s, NEG_INF)
            s_ref[slot, h] = s

    def finish(n, slot):
        i, j = steps[n]
        ps, alphas = [], []
        for h in range(hp):
            s = s_ref[slot, h]
            m_cur = jnp.max(s, axis=1, keepdims=True)
            if j == 0:
                m_next = jnp.broadcast_to(m_cur, (t, LANES))
            else:
                m_prev = m_ref[h]
                m_next = jnp.maximum(m_prev, m_cur)
                alphas.append(jnp.exp(m_prev - m_next))
            m_ref[h] = m_next
            p = jnp.exp(s - jnp.tile(m_next, (1, t // LANES)))
            l_cur = jnp.sum(p, axis=1, keepdims=True)
            if j == 0:
                l_ref[h] = jnp.broadcast_to(l_cur, (t, LANES))
            else:
                l_ref[h] = l_cur + alphas[h] * l_ref[h]
            ps.append(p.astype(jnp.bfloat16))
        pvs = [jnp.dot(ps[h], v_ref[h, rows(j), :], preferred_element_type=jnp.float32)
               for h in range(hp)]
        for h in range(hp):
            acc = pvs[h] if j == 0 else alphas[h] * acc_ref[h] + pvs[h]
            if j == i:
                o_ref[h, rows(i), :] = (acc / l_ref[h]).astype(o_ref.dtype)
            else:
                acc_ref[h] = acc

    scores(0, 0)
    for n in range(len(steps)):
        if n + 1 < len(steps):
            scores(n + 1, (n + 1) % 2)
        finish(n, n % 2)


def forgetting_attention(proj, qx, kx, *, batch, seq):
    heads = C_HEADS
    hp = HEADS_PER_STEP
    assert proj.shape[0] == 4 * heads and seq % FOX_T == 0
    blk = (hp, seq, LANES)
    nhb = heads // hp
    return pl.pallas_call(
        _fox_body,
        grid=(batch, nhb),
        in_specs=[pl.BlockSpec(blk, lambda b, h: (h, b, 0)),
                  pl.BlockSpec(blk, lambda b, h: (h, b, 0)),
                  pl.BlockSpec(blk, lambda b, h: (nhb + h, b, 0)),
                  pl.BlockSpec(blk, lambda b, h: (h, b, 0)),
                  pl.BlockSpec(blk, lambda b, h: (2 * nhb + h, b, 0))],
        out_specs=pl.BlockSpec(blk, lambda b, h: (h, b, 0)),
        out_shape=jax.ShapeDtypeStruct((heads, batch * seq, LANES), jnp.bfloat16),
        scratch_shapes=[pltpu.VMEM((2, hp, FOX_T, FOX_T), jnp.float32),
                        pltpu.VMEM((hp, FOX_T, LANES), jnp.float32),
                        pltpu.VMEM((hp, FOX_T, LANES), jnp.float32),
                        pltpu.VMEM((hp, FOX_T, LANES), jnp.float32)],
        compiler_params=_params("parallel", "parallel"),
        name="fox_attn",
    )(proj, qx, proj, kx, proj)


def _scale_q_columns(w, width, dh):
    scale = jnp.where(jnp.arange(w.shape[1]) < width, dh ** -0.5, 1.0).astype(w.dtype)
    return w * scale


def kernel(x, norm_pre, norm_post, a_w_in, a_rel_bias, a_w_out, b_w_in, b_conv_w, b_conv_b,
           b_gate_a_w, b_gate_a_b, b_gate_x_w, b_gate_x_b, b_lambda, b_w_out,
           c_w_in, c_f_bias, c_w_out):
    batch, seq, d = x.shape
    depth = norm_pre.shape[0]
    bf = lambda w: w.astype(jnp.bfloat16)
    xs = x.reshape(batch * seq, d)
    for i in range(depth):
        m, j = i % 3, i // 3
        if m == 0:
            e = a_w_out.shape[1]
            w_in = bf(_scale_q_columns(a_w_in[j], e, e // A_HEADS))
            proj = norm_proj(xs, norm_pre[i], w_in)
            o = chunk_attention(proj, a_rel_bias[j], batch=batch, seq=seq)
            xs = out_proj(o, proj, 3, bf(a_w_out[j]), xs, norm_post[i])
        elif m == 1:
            proj = norm_proj(xs, norm_pre[i], bf(b_w_in[j]))
            hs = rg_lru(proj, b_conv_w[j], b_conv_b[j], b_gate_a_w[j], b_gate_a_b[j],
                        b_gate_x_w[j], b_gate_x_b[j], b_lambda[j], batch=batch, seq=seq)
            xs = out_proj(hs, proj, 1, bf(b_w_out[j]), xs, norm_post[i])
        else:
            e = c_w_out.shape[1]
            n_f = c_w_in.shape[2] - 4 * e
            w_main = bf(_scale_q_columns(c_w_in[j][:, :4 * e], e, e // C_HEADS))
            w_f = jnp.pad(bf(c_w_in[j][:, 4 * e:]), ((0, 0), (0, LANES - n_f)))
            f_bias = jnp.pad(c_f_bias[j].astype(jnp.float32), (0, LANES - n_f)).reshape(1, LANES)
            proj, f_logit = norm_proj(xs, norm_pre[i], w_main, w_f)
            qx, kx = forget_gate_slabs(f_logit, f_bias, batch=batch, seq=seq, heads=n_f)
            o = forgetting_attention(proj, qx, kx, batch=batch, seq=seq)
            xs = out_proj(o, proj, 3, bf(c_w_out[j]), xs, norm_post[i])
    return xs.reshape(batch, seq, d)
```

```python
import functools

import jax
import jax.numpy as jnp
import numpy as np
from jax import lax
from jax.experimental import pallas as pl
from jax.experimental.pallas import tpu as pltpu

RMS_EPS = 1e-6
NEG_INF = -1e30
LANES = 128

A_HEADS = 16
A_CHUNK = 64
A_LEFT_CHUNKS = 8
A_REL_CLIP = 256
A_HEADS_PER_STEP = 2
A_QTILE = 2 * A_CHUNK
A_LEFT = A_LEFT_CHUNKS * A_CHUNK
A_KTILE = A_LEFT + A_QTILE

RG_BLOCKS = 16
RG_GROUP_BLOCKS = 4
CONV_WIDTH = 4
RG_C = 8.0
LRU_TS = 256
LRU_ROWS = 8

C_HEADS = 16
FOX_T = 512
N_SPLIT = 3

HEADS_PER_STEP = 2
OUT_PROJ_ROWS = 256
VMEM_LIMIT_BYTES = 56 * 1024 * 1024


def _params(*sem):
    return pltpu.CompilerParams(dimension_semantics=sem, vmem_limit_bytes=VMEM_LIMIT_BYTES)


def _lane_concat(ref, n):
    return jnp.concatenate([ref[c] for c in range(n)], axis=1)


def _norm_proj_body(x_ref, gam_ref, w_ref, *rest, has_extra):
    if has_extra:
        w2_ref, o_ref, o2_ref, xn_ref = rest
    else:
        o_ref, xn_ref = rest

    @pl.when(pl.program_id(1) == 0)
    def _():
        x = x_ref[...]
        ms = jnp.mean(x * x, axis=-1, keepdims=True)
        xn_ref[...] = (x * lax.rsqrt(ms + RMS_EPS) * gam_ref[...]).astype(jnp.bfloat16)
        if has_extra:
            o2_ref[...] = jnp.dot(xn_ref[...], w2_ref[...], preferred_element_type=jnp.float32)

    res = jnp.dot(xn_ref[...], w_ref[...], preferred_element_type=jnp.float32)
    for c in range(o_ref.shape[0]):
        o_ref[c] = res[:, c * LANES:(c + 1) * LANES].astype(o_ref.dtype)


def norm_proj(x, gamma, w, w2=None, *, tm=1024, tn=1024):
    m, d = x.shape
    n = w.shape[1]
    tm, tn = min(tm, m), min(tn, n)
    assert m % tm == 0 and n % tn == 0 and tn % LANES == 0
    in_specs = [pl.BlockSpec((tm, d), lambda i, j: (i, 0)),
                pl.BlockSpec((1, d), lambda i, j: (0, 0)),
                pl.BlockSpec((d, tn), lambda i, j: (0, j))]
    out_shape = [jax.ShapeDtypeStruct((n // LANES, m, LANES), jnp.bfloat16)]
    out_specs = [pl.BlockSpec((tn // LANES, tm, LANES), lambda i, j: (j, i, 0))]
    args = [x, gamma.reshape(1, d), w]
    if w2 is not None:
        in_specs.append(pl.BlockSpec((d, w2.shape[1]), lambda i, j: (0, 0)))
        out_shape.append(jax.ShapeDtypeStruct((m, w2.shape[1]), jnp.float32))
        out_specs.append(pl.BlockSpec((tm, w2.shape[1]), lambda i, j: (i, 0)))
        args.append(w2)
    out = pl.pallas_call(
        functools.partial(_norm_proj_body, has_extra=w2 is not None),
        grid=(m // tm, n // tn),
        in_specs=in_specs, out_specs=out_specs, out_shape=out_shape,
        scratch_shapes=[pltpu.VMEM((tm, d), jnp.bfloat16)],
        compiler_params=_params("parallel", "arbitrary"),
        name="norm_proj",
    )(*args)
    return out if w2 is not None else out[0]


def _out_proj_body(a_ref, g_ref, w_ref, x_ref, gam_ref, o_ref):
    ns, tm, _ = a_ref.shape
    n_chunks = tm // OUT_PROJ_ROWS

    def gated(c):
        rs = slice(c * OUT_PROJ_ROWS, (c + 1) * OUT_PROJ_ROWS)
        g = jnp.concatenate([g_ref[s, rs, :] for s in range(ns)], axis=1).astype(jnp.float32)
        a = jnp.concatenate([a_ref[s, rs, :] for s in range(ns)], axis=1).astype(jnp.float32)
        return (a * (g * jax.nn.sigmoid(g))).astype(jnp.bfloat16)

    def finish(c, y):
        rs = slice(c * OUT_PROJ_ROWS, (c + 1) * OUT_PROJ_ROWS)
        ms = jnp.mean(y * y, axis=-1, keepdims=True)
        o_ref[rs, :] = x_ref[rs, :] + y * lax.rsqrt(ms + RMS_EPS) * gam_ref[...]

    a_next = gated(0)
    y_prev = None
    for c in range(n_chunks):
        a_cur = a_next
        if c + 1 < n_chunks:
            a_next = gated(c + 1)
        y = jnp.dot(a_cur, w_ref[...], preferred_element_type=jnp.float32)
        if y_prev is not None:
            finish(c - 1, y_prev)
        y_prev = y
    finish(n_chunks - 1, y_prev)


def out_proj(a, proj, g_block, w, x, gamma, *, tm=512):
    ns, m, _ = a.shape
    e, d = w.shape
    assert ns * LANES == e
    tm = min(tm, m)
    assert m % tm == 0 and tm % OUT_PROJ_ROWS == 0
    return pl.pallas_call(
        _out_proj_body,
        grid=(m // tm,),
        in_specs=[pl.BlockSpec((ns, tm, LANES), lambda i: (0, i, 0)),
                  pl.BlockSpec((ns, tm, LANES), lambda i: (g_block, i, 0)),
                  pl.BlockSpec((e, d), lambda i: (0, 0), pipeline_mode=pl.Buffered(1)),
                  pl.BlockSpec((tm, d), lambda i: (i, 0)),
                  pl.BlockSpec((1, d), lambda i: (0, 0))],
        out_specs=pl.BlockSpec((tm, d), lambda i: (i, 0)),
        out_shape=jax.ShapeDtypeStruct((m, d), jnp.float32),
        compiler_params=_params("parallel"),
        name="out_proj",
    )(a, proj, w, x, gamma.reshape(1, d))


def _chunk_attn_body(q_ref, k_ref, v_ref, bias_ref, o_ref, s_ref):
    hp, seq, _ = q_ref.shape
    n_tiles = seq // A_QTILE
    n_cut = A_LEFT // A_QTILE

    def rows(t):
        q0 = pl.multiple_of(t * A_QTILE, A_QTILE)
        k0 = pl.multiple_of(jnp.maximum(q0 - A_LEFT, 0), A_QTILE)
        return pl.ds(q0, A_QTILE), pl.ds(k0, A_KTILE)

    def scores(t, slot):
        q_rows, k_rows = rows(t)
        table = jnp.minimum(t, n_cut)
        for h in range(hp):
            s = lax.dot_general(q_ref[h, q_rows, :], k_ref[h, k_rows, :], (((1,), (1,)), ((), ())),
                                preferred_element_type=jnp.float32)
            s_ref[slot, h] = s + bias_ref[h, table]

    def finish(t, slot):
        q_rows, k_rows = rows(t)
        ps, ls = [], []
        for h in range(hp):
            s = s_ref[slot, h]
            p = jnp.exp(s - jnp.max(s, axis=-1, keepdims=True))
            ls.append(jnp.sum(p, axis=-1, keepdims=True))
            ps.append(p.astype(jnp.bfloat16))
        outs = [jnp.dot(ps[h], v_ref[h, k_rows, :], preferred_element_type=jnp.float32)
                for h in range(hp)]
        for h in range(hp):
            o_ref[h, q_rows, :] = (outs[h] / ls[h]).astype(o_ref.dtype)

    scores(0, 0)

    def step(i, carry):
        t = 2 * i
        scores(t + 1, 1)
        finish(t, 0)
        scores(jnp.minimum(t + 2, n_tiles - 1), 0)
        finish(t + 1, 1)
        return carry

    lax.fori_loop(0, n_tiles // 2, step, 0)


def _bias_tables_body(u_ref, o_ref):
    n_tab, qt, kt = o_ref.shape[1:]
    qi = lax.broadcasted_iota(jnp.int32, (qt, kt), 0)
    j = lax.broadcasted_iota(jnp.int32, (qt, kt), 1)
    chunk_start = (qi // A_CHUNK) * A_CHUNK
    for i in range(n_tab):
        diag = jnp.broadcast_to(u_ref[0, i:i + 1, :], (qt, u_ref.shape[2]))
        table = pltpu.roll(diag, 0, 1, stride=1, stride_axis=0)[:, :kt]
        band_start = chunk_start + (i * qt - A_LEFT)
        visible = (j >= band_start) & (j < band_start + (A_LEFT_CHUNKS + 1) * A_CHUNK)
        o_ref[0, i] = jnp.where(visible, table, NEG_INF)


def _band_bias_tables(rel_bias):
    heads = rel_bias.shape[0]
    n_tab = A_LEFT // A_QTILE + 1
    n_diag = pl.cdiv(A_QTILE + A_KTILE - 1, LANES) * LANES
    m = np.arange(n_diag)
    key_minus_query = np.where(m < A_KTILE, m, m - n_diag)
    offsets = A_QTILE * np.arange(n_tab)[:, None]
    idx = np.clip(offsets - key_minus_query[None, :], -A_REL_CLIP, A_REL_CLIP) + A_REL_CLIP
    diags = rel_bias.astype(jnp.float32)[:, idx]
    return pl.pallas_call(
        _bias_tables_body,
        grid=(heads,),
        in_specs=[pl.BlockSpec((1, n_tab, n_diag), lambda h: (h, 0, 0))],
        out_specs=pl.BlockSpec((1, n_tab, A_QTILE, A_KTILE), lambda h: (h, 0, 0, 0)),
        out_shape=jax.ShapeDtypeStruct((heads, n_tab, A_QTILE, A_KTILE), jnp.float32),
        compiler_params=_params("parallel"),
        name="bias_tables",
    )(diags)


def chunk_attention(proj, rel_bias, *, batch, seq):
    heads = A_HEADS
    hp = A_HEADS_PER_STEP
    assert proj.shape[0] == 4 * heads and seq % (2 * A_QTILE) == 0 and seq >= A_KTILE
    tables = _band_bias_tables(rel_bias)
    blk = (hp, seq, LANES)
    nhb = heads // hp
    return pl.pallas_call(
        _chunk_attn_body,
        grid=(batch, nhb),
        in_specs=[pl.BlockSpec(blk, lambda b, h: (h, b, 0)),
                  pl.BlockSpec(blk, lambda b, h: (nhb + h, b, 0)),
                  pl.BlockSpec(blk, lambda b, h: (2 * nhb + h, b, 0)),
                  pl.BlockSpec((hp,) + tables.shape[1:], lambda b, h: (h, 0, 0, 0))],
        out_specs=pl.BlockSpec(blk, lambda b, h: (h, b, 0)),
        out_shape=jax.ShapeDtypeStruct((heads, batch * seq, LANES), jnp.bfloat16),
        scratch_shapes=[pltpu.VMEM((2, hp, A_QTILE, A_KTILE), jnp.float32)],
        compiler_params=_params("parallel", "parallel"),
        name="chunk_attn",
    )(proj, proj, proj, tables)


def _lru_body(xr_ref, cw_ref, cb_ref, wg_ref, ba_ref, bx_ref, lam_ref, o_ref,
              xbuf_ref, y_ref, a_ref, u_ref, h_ref, *, groups, gw):
    ns, ts, _ = o_ref.shape
    spg = gw // LANES

    @pl.when(pl.program_id(1) == 0)
    def _():
        xbuf_ref[0:LRU_ROWS, :] = jnp.zeros((LRU_ROWS, xbuf_ref.shape[1]), jnp.float32)
        h_ref[...] = jnp.zeros(h_ref.shape, jnp.float32)

    for c in range(ns):
        xbuf_ref[LRU_ROWS:LRU_ROWS + ts, c * LANES:(c + 1) * LANES] = xr_ref[c].astype(jnp.float32)
    y = cb_ref[...] + cw_ref[CONV_WIDTH - 1:CONV_WIDTH, :] * xbuf_ref[LRU_ROWS:LRU_ROWS + ts, :]
    for tap in range(CONV_WIDTH - 1):
        off = LRU_ROWS - (CONV_WIDTH - 1) + tap
        y = y + cw_ref[tap:tap + 1, :] * xbuf_ref[off:off + ts, :]
    y_ref[...] = y
    xbuf_ref[0:LRU_ROWS, :] = xbuf_ref[ts:ts + LRU_ROWS, :]

    for gi in range(groups):
        cs = slice(gi * gw, (gi + 1) * gw)
        yg = y_ref[:, cs]
        pre = jnp.dot(yg.astype(jnp.bfloat16), wg_ref[gi], preferred_element_type=jnp.float32)
        r = jax.nn.sigmoid(pre[:, :gw] + ba_ref[:, cs])
        ig = jax.nn.sigmoid(pre[:, gw:] + bx_ref[:, cs])
        lam = lam_ref[:, cs]
        softplus_neg_lam = jnp.maximum(-lam, 0.0) + jnp.log1p(jnp.exp(-jnp.abs(lam)))
        log_a = (-RG_C * softplus_neg_lam) * r
        a = jnp.exp(log_a)
        mult = jnp.sqrt(-jnp.tanh(log_a) * (1.0 + a * a))
        a_ref[:, cs] = a
        u_ref[:, cs] = mult * (ig * yg)

    row = lax.broadcasted_iota(jnp.int32, (LRU_ROWS, gw), 0)
    for gi in range(groups):
        cs = slice(gi * gw, (gi + 1) * gw)

        def group_step(g, h_prev):
            rs = pl.ds(pl.multiple_of(g * LRU_ROWS, LRU_ROWS), LRU_ROWS)
            a = a_ref[rs, cs]
            u = u_ref[rs, cs]
            for d in (1, 2, 4):
                a_sh = jnp.where(row >= d, pltpu.roll(a, d, 0), 1.0)
                u_sh = jnp.where(row >= d, pltpu.roll(u, d, 0), 0.0)
                u = a * u_sh + u
                a = a * a_sh
            h = a * h_prev + u
            for c in range(spg):
                o_ref[gi * spg + c, rs, :] = h[:, c * LANES:(c + 1) * LANES].astype(o_ref.dtype)
            return jnp.broadcast_to(h[LRU_ROWS - 1:LRU_ROWS, :], (LRU_ROWS, gw))

        h_last = lax.fori_loop(0, ts // LRU_ROWS, group_step, h_ref[:, cs])
        h_ref[:, cs] = h_last


def _gate_group_weights(w_a, w_x):
    nb, bs, _ = w_a.shape
    groups = nb // RG_GROUP_BLOCKS
    gw = RG_GROUP_BLOCKS * bs
    eye = jnp.eye(RG_GROUP_BLOCKS, dtype=w_a.dtype)

    def block_diag(w):
        w = w.reshape(groups, RG_GROUP_BLOCKS, bs, bs)
        return (w[:, :, :, None, :] * eye[None, :, None, :, None]).reshape(groups, gw, gw)

    return jnp.concatenate([block_diag(w_a), block_diag(w_x)], axis=2).astype(jnp.bfloat16)


def rg_lru(proj, conv_w, conv_b, w_a, b_a, w_x, b_x, lam, *, batch, seq):
    ns = proj.shape[0] // 2
    width = ns * LANES
    gw = RG_GROUP_BLOCKS * (width // RG_BLOCKS)
    groups = width // gw
    ts = min(LRU_TS, seq)
    nts = seq // ts
    assert seq % ts == 0 and gw % LANES == 0
    wg = _gate_group_weights(w_a, w_x)
    vec = lambda v: v.reshape(1, width).astype(jnp.float32)
    const2 = lambda b, s: (0, 0)
    return pl.pallas_call(
        functools.partial(_lru_body, groups=groups, gw=gw),
        grid=(batch, nts),
        in_specs=[pl.BlockSpec((ns, ts, LANES), lambda b, s: (0, b * nts + s, 0)),
                  pl.BlockSpec((CONV_WIDTH, width), const2),
                  pl.BlockSpec((1, width), const2),
                  pl.BlockSpec(wg.shape, lambda b, s: (0, 0, 0)),
                  pl.BlockSpec((1, width), const2),
                  pl.BlockSpec((1, width), const2),
                  pl.BlockSpec((1, width), const2)],
        out_specs=pl.BlockSpec((ns, ts, LANES), lambda b, s: (0, b * nts + s, 0)),
        out_shape=jax.ShapeDtypeStruct((ns, batch * seq, LANES), jnp.bfloat16),
        scratch_shapes=[pltpu.VMEM((ts + LRU_ROWS, width), jnp.float32),
                        pltpu.VMEM((ts, width), jnp.float32),
                        pltpu.VMEM((ts, width), jnp.float32),
                        pltpu.VMEM((ts, width), jnp.float32),
                        pltpu.VMEM((LRU_ROWS, width), jnp.float32)],
        compiler_params=_params("parallel", "arbitrary"),
        name="rg_lru",
    )(proj, conv_w.astype(jnp.float32), vec(conv_b), wg, vec(b_a), vec(b_x), vec(lam))


FOX_PIECE_LANES = 16 * N_SPLIT


def _split3(x):
    hi = x.astype(jnp.bfloat16).astype(jnp.float32)
    r1 = x - hi
    mid = r1.astype(jnp.bfloat16).astype(jnp.float32)
    lo = (r1 - mid).astype(jnp.bfloat16).astype(jnp.float32)
    return hi, mid, lo


def _forget_gate_body(f_ref, fb_ref, qx_ref, kx_ref, *, heads):
    z = f_ref[0] + fb_ref[...]
    c = jnp.minimum(z, 0.0) - jnp.log1p(jnp.exp(-jnp.abs(z)))
    n = c.shape[0]
    row = lax.broadcasted_iota(jnp.int32, c.shape, 0)
    d = 1
    while d < n:
        c = c + jnp.where(row >= d, pltpu.roll(c, d, 0), 0.0)
        d *= 2
    lane = lax.broadcasted_iota(jnp.int32, c.shape, 1)
    hi, mid, lo = _split3(c)
    pieces = jnp.where(lane < heads, hi,
                       jnp.where(lane < 2 * heads, pltpu.roll(mid, heads, 1),
                                 jnp.where(lane < 3 * heads, pltpu.roll(lo, 2 * heads, 1), 0.0)))
    qx_ref[...] = pieces.astype(qx_ref.dtype)
    kx_ref[...] = pltpu.roll(-pieces, FOX_PIECE_LANES, 1).astype(kx_ref.dtype)


def forget_gate_slabs(f_logit, f_bias_padded, *, batch, seq, heads):
    assert heads * N_SPLIT == FOX_PIECE_LANES and 2 * FOX_PIECE_LANES <= LANES
    f3 = f_logit.reshape(batch, seq, LANES)
    shape = jax.ShapeDtypeStruct((batch * seq, LANES), jnp.bfloat16)
    spec = pl.BlockSpec((seq, LANES), lambda b: (b, 0))
    return pl.pallas_call(
        functools.partial(_forget_gate_body, heads=heads),
        grid=(batch,),
        in_specs=[pl.BlockSpec((1, seq, LANES), lambda b: (b, 0, 0)),
                  pl.BlockSpec((1, LANES), lambda b: (0, 0))],
        out_specs=[spec, spec],
        out_shape=[shape, shape],
        compiler_params=_params("parallel"),
        name="forget_gate",
    )(f3, f_bias_padded)


def _fox_body(q_ref, qx_ref, k_ref, kx_ref, v_ref, o_ref, s_ref, m_ref, l_ref, acc_ref):
    hp, seq, dh = q_ref.shape
    t = FOX_T
    heads = FOX_PIECE_LANES // N_SPLIT
    steps = [(i, j) for i in range(seq // t) for j in range(i + 1)]
    lane = lax.broadcasted_iota(jnp.int32, (1, LANES), 1)

    def augment(x_ref, px_ref, h, r, first_lane):
        rel = lane - (first_lane + pl.program_id(1) * hp + h)
        select = (rel == 0) | (rel == heads) | (rel == 2 * heads)
        extra = jnp.where(select, jnp.ones((), px_ref.dtype), px_ref[r, :])
        return jnp.concatenate([x_ref[h, r, :], extra], axis=1)

    def rows(b):
        return slice(b * t, (b + 1) * t)

    def scores(n, slot):
        i, j = steps[n]
        for h in range(hp):
            q_aug = augment(q_ref, qx_ref, h, rows(i), FOX_PIECE_LANES)
            k_aug = augment(k_ref, kx_ref, h, rows(j), 0)
            s = lax.dot_general(q_aug, k_aug, (((1,), (1,)), ((), ())),
                                preferred_element_type=jnp.float32)
            if i == j:
                r = lax.broadcasted_iota(jnp.int32, s.shape, 0)
                c = lax.broadcasted_iota(jnp.int32, s.shape, 1)
                s = jnp.where(c <= r, s, NEG_INF)
            s_ref[slot, h] = s

    def finish(n, slot):
        i, j = steps[n]
        ps, alphas = [], []
        for h in range(hp):
            s = s_ref[slot, h]
            m_cur = jnp.max(s, axis=1, keepdims=True)
            if j == 0:
                m_next = jnp.broadcast_to(m_cur, (t, LANES))
            else:
                m_prev = m_ref[h]
                m_next = jnp.maximum(m_prev, m_cur)
                alphas.append(jnp.exp(m_prev - m_next))
            m_ref[h] = m_next
            p = jnp.exp(s - jnp.tile(m_next, (1, t // LANES)))
            l_cur = jnp.sum(p, axis=1, keepdims=True)
            if j == 0:
                l_ref[h] = jnp.broadcast_to(l_cur, (t, LANES))
            else:
                l_ref[h] = l_cur + alphas[h] * l_ref[h]
            ps.append(p.astype(jnp.bfloat16))
        pvs = [jnp.dot(ps[h], v_ref[h, rows(j), :], preferred_element_type=jnp.float32)
               for h in range(hp)]
        for h in range(hp):
            acc = pvs[h] if j == 0 else alphas[h] * acc_ref[h] + pvs[h]
            if j == i:
                o_ref[h, rows(i), :] = (acc / l_ref[h]).astype(o_ref.dtype)
            else:
                acc_ref[h] = acc

    scores(0, 0)
    for n in range(len(steps)):
        if n + 1 < len(steps):
            scores(n + 1, (n + 1) % 2)
        finish(n, n % 2)


def forgetting_attention(proj, qx, kx, *, batch, seq):
    heads = C_HEADS
    hp = HEADS_PER_STEP
    assert proj.shape[0] == 4 * heads and seq % FOX_T == 0
    blk = (hp, seq, LANES)
    nhb = heads // hp
    return pl.pallas_call(
        _fox_body,
        grid=(batch, nhb),
        in_specs=[pl.BlockSpec(blk, lambda b, h: (h, b, 0)),
                  pl.BlockSpec((seq, LANES), lambda b, h: (b, 0)),
                  pl.BlockSpec(blk, lambda b, h: (nhb + h, b, 0)),
                  pl.BlockSpec((seq, LANES), lambda b, h: (b, 0)),
                  pl.BlockSpec(blk, lambda b, h: (2 * nhb + h, b, 0))],
        out_specs=pl.BlockSpec(blk, lambda b, h: (h, b, 0)),
        out_shape=jax.ShapeDtypeStruct((heads, batch * seq, LANES), jnp.bfloat16),
        scratch_shapes=[pltpu.VMEM((2, hp, FOX_T, FOX_T), jnp.float32),
                        pltpu.VMEM((hp, FOX_T, LANES), jnp.float32),
                        pltpu.VMEM((hp, FOX_T, LANES), jnp.float32),
                        pltpu.VMEM((hp, FOX_T, LANES), jnp.float32)],
        compiler_params=_params("parallel", "parallel"),
        name="fox_attn",
    )(proj, qx, proj, kx, proj)


def _scale_q_columns(w, width, dh):
    scale = jnp.where(jnp.arange(w.shape[1]) < width, dh ** -0.5, 1.0).astype(w.dtype)
    return w * scale


def kernel(x, norm_pre, norm_post, a_w_in, a_rel_bias, a_w_out, b_w_in, b_conv_w, b_conv_b,
           b_gate_a_w, b_gate_a_b, b_gate_x_w, b_gate_x_b, b_lambda, b_w_out,
           c_w_in, c_f_bias, c_w_out):
    batch, seq, d = x.shape
    depth = norm_pre.shape[0]
    bf = lambda w: w.astype(jnp.bfloat16)
    xs = x.reshape(batch * seq, d)
    for i in range(depth):
        m, j = i % 3, i // 3
        if m == 0:
            e = a_w_out.shape[1]
            w_in = bf(_scale_q_columns(a_w_in[j], e, e // A_HEADS))
            proj = norm_proj(xs, norm_pre[i], w_in)
            o = chunk_attention(proj, a_rel_bias[j], batch=batch, seq=seq)
            xs = out_proj(o, proj, 3, bf(a_w_out[j]), xs, norm_post[i])
        elif m == 1:
            proj = norm_proj(xs, norm_pre[i], bf(b_w_in[j]))
            hs = rg_lru(proj, b_conv_w[j], b_conv_b[j], b_gate_a_w[j], b_gate_a_b[j],
                        b_gate_x_w[j], b_gate_x_b[j], b_lambda[j], batch=batch, seq=seq)
            xs = out_proj(hs, proj, 1, bf(b_w_out[j]), xs, norm_post[i])
        else:
            e = c_w_out.shape[1]
            n_f = c_w_in.shape[2] - 4 * e
            w_main = bf(_scale_q_columns(c_w_in[j][:, :4 * e], e, e // C_HEADS))
            w_f = jnp.pad(bf(c_w_in[j][:, 4 * e:]), ((0, 0), (0, LANES - n_f)))
            f_bias = jnp.pad(c_f_bias[j].astype(jnp.float32), (0, LANES - n_f)).reshape(1, LANES)
            proj, f_logit = norm_proj(xs, norm_pre[i], w_main, w_f)
            qx, kx = forget_gate_slabs(f_logit, f_bias, batch=batch, seq=seq, heads=n_f)
            o = forgetting_attention(proj, qx, kx, batch=batch, seq=seq)
            xs = out_proj(o, proj, 3, bf(c_w_out[j]), xs, norm_post[i])
    return xs.reshape(batch, seq, d)
```

```python
import functools

import jax
import jax.numpy as jnp
import numpy as np
from jax import lax
from jax.experimental import pallas as pl
from jax.experimental.pallas import tpu as pltpu

RMS_EPS = 1e-6
NEG_INF = -1e30
F32_TINY = float(np.finfo(np.float32).tiny)
LANES = 128

A_HEADS = 16
A_CHUNK = 64
A_LEFT_CHUNKS = 8
A_REL_CLIP = 256
A_HEADS_PER_STEP = 2
A_QTILE = 2 * A_CHUNK
A_LEFT = A_LEFT_CHUNKS * A_CHUNK
A_KTILE = A_LEFT + A_QTILE

RG_BLOCKS = 16
RG_GROUP_BLOCKS = 4
CONV_WIDTH = 4
RG_C = 8.0
LRU_TS = 32
LRU_GATE_ROWS = 64

C_HEADS = 16
FOX_T = 512
N_SPLIT = 3

HEADS_PER_STEP = 2
OUT_PROJ_ROWS = 256
VMEM_LIMIT_BYTES = 56 * 1024 * 1024


def _params(*sem):
    return pltpu.CompilerParams(dimension_semantics=sem, vmem_limit_bytes=VMEM_LIMIT_BYTES)


def _norm_proj_body(x_ref, gam_ref, w_ref, *rest, has_extra):
    if has_extra:
        w2_ref, o_ref, o2_ref, xn_ref = rest
    else:
        o_ref, xn_ref = rest

    @pl.when(pl.program_id(1) == 0)
    def _():
        x = x_ref[...]
        ms = jnp.mean(x * x, axis=-1, keepdims=True)
        xn_ref[...] = (x * lax.rsqrt(ms + RMS_EPS) * gam_ref[...]).astype(jnp.bfloat16)
        if has_extra:
            o2_ref[...] = jnp.dot(xn_ref[...], w2_ref[...], preferred_element_type=jnp.float32)

    res = jnp.dot(xn_ref[...], w_ref[...], preferred_element_type=jnp.float32)
    for c in range(o_ref.shape[0]):
        o_ref[c] = res[:, c * LANES:(c + 1) * LANES].astype(o_ref.dtype)


def norm_proj(x, gamma, w, w2=None, *, time_major_batch=None, tm=1024, tn=1024):
    m, d = x.shape
    n = w.shape[1]
    tm, tn = min(tm, m), min(tn, n)
    assert m % tm == 0 and n % tn == 0 and tn % LANES == 0
    in_specs = [pl.BlockSpec((tm, d), lambda i, j: (i, 0)),
                pl.BlockSpec((1, d), lambda i, j: (0, 0)),
                pl.BlockSpec((d, tn), lambda i, j: (0, j))]
    if time_major_batch is None:
        out_shape = [jax.ShapeDtypeStruct((n // LANES, m, LANES), jnp.bfloat16)]
        out_specs = [pl.BlockSpec((tn // LANES, tm, LANES), lambda i, j: (j, i, 0))]
    else:
        seq = m // time_major_batch
        spb = seq // tm
        assert seq % tm == 0
        out_shape = [jax.ShapeDtypeStruct((n // LANES, seq, time_major_batch * LANES), jnp.bfloat16)]
        out_specs = [pl.BlockSpec((tn // LANES, tm, LANES), lambda i, j: (j, i % spb, i // spb))]
    args = [x, gamma.reshape(1, d), w]
    if w2 is not None:
        in_specs.append(pl.BlockSpec((d, w2.shape[1]), lambda i, j: (0, 0)))
        out_shape.append(jax.ShapeDtypeStruct((m, w2.shape[1]), jnp.float32))
        out_specs.append(pl.BlockSpec((tm, w2.shape[1]), lambda i, j: (i, 0)))
        args.append(w2)
    out = pl.pallas_call(
        functools.partial(_norm_proj_body, has_extra=w2 is not None),
        grid=(m // tm, n // tn),
        in_specs=in_specs, out_specs=out_specs, out_shape=out_shape,
        scratch_shapes=[pltpu.VMEM((tm, d), jnp.bfloat16)],
        compiler_params=_params("parallel", "arbitrary"),
        name="norm_proj",
    )(*args)
    out = [out[0].reshape(n // LANES, m, LANES)] + list(out[1:])
    return out if w2 is not None else out[0]


def _out_proj_body(a_ref, g_ref, w_ref, x_ref, gam_ref, o_ref):
    ns, tm, _ = a_ref.shape
    n_chunks = tm // OUT_PROJ_ROWS

    def gated(c):
        rs = slice(c * OUT_PROJ_ROWS, (c + 1) * OUT_PROJ_ROWS)
        g = jnp.concatenate([g_ref[s, rs, :] for s in range(ns)], axis=1).astype(jnp.float32)
        a = jnp.concatenate([a_ref[s, rs, :] for s in range(ns)], axis=1).astype(jnp.float32)
        half_g = 0.5 * g
        return (a * (half_g * (1.0 + jnp.tanh(half_g)))).astype(jnp.bfloat16)

    def finish(c, y):
        rs = slice(c * OUT_PROJ_ROWS, (c + 1) * OUT_PROJ_ROWS)
        ms = jnp.mean(y * y, axis=-1, keepdims=True)
        o_ref[rs, :] = x_ref[rs, :] + y * lax.rsqrt(ms + RMS_EPS) * gam_ref[...]

    a_next = gated(0)
    y_prev = None
    for c in range(n_chunks):
        a_cur = a_next
        if c + 1 < n_chunks:
            a_next = gated(c + 1)
        y = jnp.dot(a_cur, w_ref[...], preferred_element_type=jnp.float32)
        if y_prev is not None:
            finish(c - 1, y_prev)
        y_prev = y
    finish(n_chunks - 1, y_prev)


def out_proj(a, proj, g_block, w, x, gamma, *, time_major_batch=None, tm=512):
    ns, m, _ = a.shape
    e, d = w.shape
    assert ns * LANES == e
    tm = min(tm, m)
    assert m % tm == 0 and tm % OUT_PROJ_ROWS == 0
    if time_major_batch is None:
        a_map = lambda i: (0, i, 0)
        g_map = lambda i: (g_block, i, 0)
    else:
        seq = m // time_major_batch
        spb = seq // tm
        assert seq % tm == 0
        a = a.reshape(ns, seq, time_major_batch * LANES)
        proj = proj.reshape(proj.shape[0], seq, time_major_batch * LANES)
        a_map = lambda i: (0, i % spb, i // spb)
        g_map = lambda i: (g_block, i % spb, i // spb)
    return pl.pallas_call(
        _out_proj_body,
        grid=(m // tm,),
        in_specs=[pl.BlockSpec((ns, tm, LANES), a_map),
                  pl.BlockSpec((ns, tm, LANES), g_map),
                  pl.BlockSpec((e, d), lambda i: (0, 0), pipeline_mode=pl.Buffered(1)),
                  pl.BlockSpec((tm, d), lambda i: (i, 0)),
                  pl.BlockSpec((1, d), lambda i: (0, 0))],
        out_specs=pl.BlockSpec((tm, d), lambda i: (i, 0)),
        out_shape=jax.ShapeDtypeStruct((m, d), jnp.float32),
        compiler_params=_params("parallel"),
        name="out_proj",
    )(a, proj, w, x, gamma.reshape(1, d))


def _chunk_attn_body(q_ref, k_ref, v_ref, bias_ref, o_ref, s_ref):
    hp, seq, _ = q_ref.shape
    n_tiles = seq // A_QTILE
    n_cut = A_LEFT // A_QTILE

    def rows(t):
        q0 = pl.multiple_of(t * A_QTILE, A_QTILE)
        k0 = pl.multiple_of(jnp.maximum(q0 - A_LEFT, 0), A_QTILE)
        return pl.ds(q0, A_QTILE), pl.ds(k0, A_KTILE)

    def scores(t, slot):
        q_rows, k_rows = rows(t)
        table = jnp.minimum(t, n_cut)
        for h in range(hp):
            s = lax.dot_general(q_ref[h, q_rows, :], k_ref[h, k_rows, :], (((1,), (1,)), ((), ())),
                                preferred_element_type=jnp.float32)
            s_ref[slot, h] = s + bias_ref[h, table]

    def finish(t, slot):
        q_rows, k_rows = rows(t)
        ps, ls = [], []
        for h in range(hp):
            s = s_ref[slot, h]
            p = jnp.exp(s - jnp.max(s, axis=-1, keepdims=True))
            ls.append(jnp.sum(p, axis=-1, keepdims=True))
            ps.append(p.astype(jnp.bfloat16))
        outs = [jnp.dot(ps[h], v_ref[h, k_rows, :], preferred_element_type=jnp.float32)
                for h in range(hp)]
        for h in range(hp):
            o_ref[h, q_rows, :] = (outs[h] / ls[h]).astype(o_ref.dtype)

    scores(0, 0)

    def step(i, carry):
        t = 2 * i
        scores(t + 1, 1)
        finish(t, 0)
        scores(jnp.minimum(t + 2, n_tiles - 1), 0)
        finish(t + 1, 1)
        return carry

    lax.fori_loop(0, n_tiles // 2, step, 0)


def _bias_tables_body(u_ref, o_ref):
    n_tab, qt, kt = o_ref.shape[1:]
    qi = lax.broadcasted_iota(jnp.int32, (qt, kt), 0)
    j = lax.broadcasted_iota(jnp.int32, (qt, kt), 1)
    chunk_start = (qi // A_CHUNK) * A_CHUNK
    for i in range(n_tab):
        diag = jnp.broadcast_to(u_ref[0, i:i + 1, :], (qt, u_ref.shape[2]))
        table = pltpu.roll(diag, 0, 1, stride=1, stride_axis=0)[:, :kt]
        band_start = chunk_start + (i * qt - A_LEFT)
        visible = (j >= band_start) & (j < band_start + (A_LEFT_CHUNKS + 1) * A_CHUNK)
        o_ref[0, i] = jnp.where(visible, table, NEG_INF)


def _band_bias_tables(rel_bias):
    heads = rel_bias.shape[0]
    n_tab = A_LEFT // A_QTILE + 1
    n_diag = pl.cdiv(A_QTILE + A_KTILE - 1, LANES) * LANES
    m = np.arange(n_diag)
    key_minus_query = np.where(m < A_KTILE, m, m - n_diag)
    offsets = A_QTILE * np.arange(n_tab)[:, None]
    idx = np.clip(offsets - key_minus_query[None, :], -A_REL_CLIP, A_REL_CLIP) + A_REL_CLIP
    diags = rel_bias.astype(jnp.float32)[:, idx]
    return pl.pallas_call(
        _bias_tables_body,
        grid=(heads,),
        in_specs=[pl.BlockSpec((1, n_tab, n_diag), lambda h: (h, 0, 0))],
        out_specs=pl.BlockSpec((1, n_tab, A_QTILE, A_KTILE), lambda h: (h, 0, 0, 0)),
        out_shape=jax.ShapeDtypeStruct((heads, n_tab, A_QTILE, A_KTILE), jnp.float32),
        compiler_params=_params("parallel"),
        name="bias_tables",
    )(diags)


def chunk_attention(proj, rel_bias, *, batch, seq):
    heads = A_HEADS
    hp = A_HEADS_PER_STEP
    assert proj.shape[0] == 4 * heads and seq % (2 * A_QTILE) == 0 and seq >= A_KTILE
    tables = _band_bias_tables(rel_bias)
    blk = (hp, seq, LANES)
    nhb = heads // hp
    return pl.pallas_call(
        _chunk_attn_body,
        grid=(batch, nhb),
        in_specs=[pl.BlockSpec(blk, lambda b, h: (h, b, 0)),
                  pl.BlockSpec(blk, lambda b, h: (nhb + h, b, 0)),
                  pl.BlockSpec(blk, lambda b, h: (2 * nhb + h, b, 0)),
                  pl.BlockSpec((hp,) + tables.shape[1:], lambda b, h: (h, 0, 0, 0))],
        out_specs=pl.BlockSpec(blk, lambda b, h: (h, b, 0)),
        out_shape=jax.ShapeDtypeStruct((heads, batch * seq, LANES), jnp.bfloat16),
        scratch_shapes=[pltpu.VMEM((2, hp, A_QTILE, A_KTILE), jnp.float32)],
        compiler_params=_params("parallel", "parallel"),
        name="chunk_attn",
    )(proj, proj, proj, tables)


def _lru_body(xr_ref, cw_ref, cb_ref, wg_ref, ba_ref, bx_ref, lam_ref, o_ref,
              xbuf_ref, y_ref, pre_ref, a_ref, u_ref, h_ref, *, groups, gw, nb):
    ns, rows, _ = o_ref.shape
    hist = CONV_WIDTH * nb

    @pl.when(pl.program_id(0) == 0)
    def _():
        xbuf_ref[0:hist, :] = jnp.zeros((hist, xbuf_ref.shape[1]), jnp.float32)
        h_ref[...] = jnp.zeros(h_ref.shape, jnp.float32)

    for c in range(ns):
        xbuf_ref[hist:hist + rows, c * LANES:(c + 1) * LANES] = xr_ref[c].astype(jnp.float32)

    def conv_and_gate_matmul(gi, slot):
        cs = slice(gi * gw, (gi + 1) * gw)
        y = cb_ref[:, cs] + cw_ref[CONV_WIDTH - 1:CONV_WIDTH, cs] * xbuf_ref[hist:hist + rows, cs]
        for tap in range(CONV_WIDTH - 1):
            off = hist - (CONV_WIDTH - 1 - tap) * nb
            y = y + cw_ref[tap:tap + 1, cs] * xbuf_ref[off:off + rows, cs]
        y_ref[slot] = y
        pre_ref[slot] = jnp.dot(y.astype(jnp.bfloat16), wg_ref[gi], preferred_element_type=jnp.float32)

    def gates(gi, slot):
        cs = slice(gi * gw, (gi + 1) * gw)
        half_ba = 0.5 * ba_ref[:, cs]
        half_bx = 0.5 * bx_ref[:, cs]
        lam = lam_ref[:, cs]
        softplus_neg_lam = jnp.maximum(-lam, 0.0) + jnp.log1p(jnp.exp(-jnp.abs(lam)))
        half_coef = (-0.5 * RG_C) * softplus_neg_lam
        for k in range(rows // LRU_GATE_ROWS):
            rk = slice(k * LRU_GATE_ROWS, (k + 1) * LRU_GATE_ROWS)
            tanh_r = jnp.tanh(pre_ref[slot, rk, :gw] + half_ba)
            tanh_i = jnp.tanh(pre_ref[slot, rk, gw:] + half_bx)
            log_a = half_coef * tanh_r + half_coef
            a = jnp.exp(log_a)
            quarter = (-0.25 * jnp.tanh(log_a)) * (1.0 + a * a)
            half_mult = quarter * lax.rsqrt(jnp.maximum(quarter, F32_TINY))
            a_ref[rk, cs] = a
            u_ref[rk, cs] = half_mult * ((tanh_i + 1.0) * y_ref[slot, rk, :])

    conv_and_gate_matmul(0, 0)
    for gi in range(groups):
        if gi + 1 < groups:
            conv_and_gate_matmul(gi + 1, (gi + 1) % 2)
        gates(gi, gi % 2)

    xbuf_ref[0:hist, :] = xbuf_ref[rows:rows + hist, :]

    def time_step(t, h):
        rs = pl.ds(pl.multiple_of(t * nb, nb), nb)
        h = a_ref[rs, :] * h + u_ref[rs, :]
        for c in range(ns):
            o_ref[c, rs, :] = h[:, c * LANES:(c + 1) * LANES].astype(o_ref.dtype)
        return h

    h_ref[...] = lax.fori_loop(0, rows // nb, time_step, h_ref[...])


def _gate_group_weights(w_a, w_x):
    nb, bs, _ = w_a.shape
    groups = nb // RG_GROUP_BLOCKS
    gw = RG_GROUP_BLOCKS * bs
    eye = jnp.eye(RG_GROUP_BLOCKS, dtype=w_a.dtype)

    def block_diag(w):
        w = w.reshape(groups, RG_GROUP_BLOCKS, bs, bs)
        return (w[:, :, :, None, :] * (0.5 * eye)[None, :, None, :, None]).reshape(groups, gw, gw)

    return jnp.concatenate([block_diag(w_a), block_diag(w_x)], axis=2).astype(jnp.bfloat16)


def rg_lru(proj, conv_w, conv_b, w_a, b_a, w_x, b_x, lam, *, batch, seq):
    ns = proj.shape[0] // 2
    width = ns * LANES
    gw = RG_GROUP_BLOCKS * (width // RG_BLOCKS)
    groups = width // gw
    ts = min(LRU_TS, seq)
    rows = ts * batch
    assert seq % ts == 0 and gw % LANES == 0 and ts >= CONV_WIDTH and batch % 16 == 0
    wg = _gate_group_weights(w_a, w_x)
    vec = lambda v: v.reshape(1, width).astype(jnp.float32)
    const2 = lambda s: (0, 0)
    return pl.pallas_call(
        functools.partial(_lru_body, groups=groups, gw=gw, nb=batch),
        grid=(seq // ts,),
        in_specs=[pl.BlockSpec((ns, rows, LANES), lambda s: (0, s, 0)),
                  pl.BlockSpec((CONV_WIDTH, width), const2),
                  pl.BlockSpec((1, width), const2),
                  pl.BlockSpec(wg.shape, lambda s: (0, 0, 0), pipeline_mode=pl.Buffered(1)),
                  pl.BlockSpec((1, width), const2),
                  pl.BlockSpec((1, width), const2),
                  pl.BlockSpec((1, width), const2)],
        out_specs=pl.BlockSpec((ns, rows, LANES), lambda s: (0, s, 0)),
        out_shape=jax.ShapeDtypeStruct((ns, batch * seq, LANES), jnp.bfloat16),
        scratch_shapes=[pltpu.VMEM((rows + CONV_WIDTH * batch, width), jnp.float32),
                        pltpu.VMEM((2, rows, gw), jnp.float32),
                        pltpu.VMEM((2, rows, 2 * gw), jnp.float32),
                        pltpu.VMEM((rows, width), jnp.float32),
                        pltpu.VMEM((rows, width), jnp.float32),
                        pltpu.VMEM((batch, width), jnp.float32)],
        compiler_params=_params("arbitrary"),
        name="rg_lru",
    )(proj, conv_w.astype(jnp.float32), vec(conv_b), wg, vec(b_a), vec(b_x), vec(lam))


FOX_PIECE_LANES = 16 * N_SPLIT


def _split3(x):
    hi = x.astype(jnp.bfloat16).astype(jnp.float32)
    r1 = x - hi
    mid = r1.astype(jnp.bfloat16).astype(jnp.float32)
    lo = (r1 - mid).astype(jnp.bfloat16).astype(jnp.float32)
    return hi, mid, lo


def _forget_gate_body(f_ref, fb_ref, qx_ref, kx_ref, *, heads):
    z = f_ref[0] + fb_ref[...]
    c = jnp.minimum(z, 0.0) - jnp.log1p(jnp.exp(-jnp.abs(z)))
    n = c.shape[0]
    row = lax.broadcasted_iota(jnp.int32, c.shape, 0)
    d = 1
    while d < n:
        c = c + jnp.where(row >= d, pltpu.roll(c, d, 0), 0.0)
        d *= 2
    lane = lax.broadcasted_iota(jnp.int32, c.shape, 1)
    hi, mid, lo = _split3(c)
    pieces = jnp.where(lane < heads, hi,
                       jnp.where(lane < 2 * heads, pltpu.roll(mid, heads, 1),
                                 jnp.where(lane < 3 * heads, pltpu.roll(lo, 2 * heads, 1), 0.0)))
    qx_ref[...] = pieces.astype(qx_ref.dtype)
    kx_ref[...] = pltpu.roll(-pieces, FOX_PIECE_LANES, 1).astype(kx_ref.dtype)


def forget_gate_slabs(f_logit, f_bias_padded, *, batch, seq, heads):
    assert heads * N_SPLIT == FOX_PIECE_LANES and 2 * FOX_PIECE_LANES <= LANES
    f3 = f_logit.reshape(batch, seq, LANES)
    shape = jax.ShapeDtypeStruct((batch * seq, LANES), jnp.bfloat16)
    spec = pl.BlockSpec((seq, LANES), lambda b: (b, 0))
    return pl.pallas_call(
        functools.partial(_forget_gate_body, heads=heads),
        grid=(batch,),
        in_specs=[pl.BlockSpec((1, seq, LANES), lambda b: (b, 0, 0)),
                  pl.BlockSpec((1, LANES), lambda b: (0, 0))],
        out_specs=[spec, spec],
        out_shape=[shape, shape],
        compiler_params=_params("parallel"),
        name="forget_gate",
    )(f3, f_bias_padded)


def _fox_body(q_ref, qx_ref, k_ref, kx_ref, v_ref, o_ref, s_ref, m_ref, l_ref, acc_ref):
    hp, seq, dh = q_ref.shape
    t = FOX_T
    heads = FOX_PIECE_LANES // N_SPLIT
    steps = [(i, j) for i in range(seq // t) for j in range(i + 1)]
    lane = lax.broadcasted_iota(jnp.int32, (1, LANES), 1)

    def augment(x_ref, px_ref, h, r, first_lane):
        rel = lane - (first_lane + pl.program_id(1) * hp + h)
        select = (rel == 0) | (rel == heads) | (rel == 2 * heads)
        extra = jnp.where(select, jnp.ones((), px_ref.dtype), px_ref[r, :])
        return jnp.concatenate([x_ref[h, r, :], extra], axis=1)

    def rows(b):
        return slice(b * t, (b + 1) * t)

    def scores(n, slot):
        i, j = steps[n]
        for h in range(hp):
            q_aug = augment(q_ref, qx_ref, h, rows(i), FOX_PIECE_LANES)
            k_aug = augment(k_ref, kx_ref, h, rows(j), 0)
            s = lax.dot_general(q_aug, k_aug, (((1,), (1,)), ((), ())),
                                preferred_element_type=jnp.float32)
            if i == j:
                r = lax.broadcasted_iota(jnp.int32, s.shape, 0)
                c = lax.broadcasted_iota(jnp.int32, s.shape, 1)
                s = jnp.where(c <= r, s, NEG_INF)
            s_ref[slot, h] = s

    def finish(n, slot):
        i, j = steps[n]
        ps, alphas = [], []
        for h in range(hp):
            s = s_ref[slot, h]
            m_cur = jnp.max(s, axis=1, keepdims=True)
            if j == 0:
                m_next = jnp.broadcast_to(m_cur, (t, LANES))
            else:
                m_prev = m_ref[h]
                m_next = jnp.maximum(m_prev, m_cur)
                alphas.append(jnp.exp(m_prev - m_next))
            m_ref[h] = m_next
            p = jnp.exp(s - jnp.tile(m_next, (1, t // LANES)))
            l_cur = jnp.sum(p, axis=1, keepdims=True)
            if j == 0:
                l_ref[h] = jnp.broadcast_to(l_cur, (t, LANES))
            else:
                l_ref[h] = l_cur + alphas[h] * l_ref[h]
            ps.append(p.astype(jnp.bfloat16))
        pvs = [jnp.dot(ps[h], v_ref[h, rows(j), :], preferred_element_type=jnp.float32)
               for h in range(hp)]
        for h in range(hp):
            acc = pvs[h] if j == 0 else alphas[h] * acc_ref[h] + pvs[h]
            if j == i:
                o_ref[h, rows(i), :] = (acc / l_ref[h]).astype(o_ref.dtype)
            else:
                acc_ref[h] = acc

    scores(0, 0)
    for n in range(len(steps)):
        if n + 1 < len(steps):
            scores(n + 1, (n + 1) % 2)
        finish(n, n % 2)


def forgetting_attention(proj, qx, kx, *, batch, seq):
    heads = C_HEADS
    hp = HEADS_PER_STEP
    assert proj.shape[0] == 4 * heads and seq % FOX_T == 0
    blk = (hp, seq, LANES)
    nhb = heads // hp
    return pl.pallas_call(
        _fox_body,
        grid=(batch, nhb),
        in_specs=[pl.BlockSpec(blk, lambda b, h: (h, b, 0)),
                  pl.BlockSpec((seq, LANES), lambda b, h: (b, 0)),
                  pl.BlockSpec(blk, lambda b, h: (nhb + h, b, 0)),
                  pl.BlockSpec((seq, LANES), lambda b, h: (b, 0)),
                  pl.BlockSpec(blk, lambda b, h: (2 * nhb + h, b, 0))],
        out_specs=pl.BlockSpec(blk, lambda b, h: (h, b, 0)),
        out_shape=jax.ShapeDtypeStruct((heads, batch * seq, LANES), jnp.bfloat16),
        scratch_shapes=[pltpu.VMEM((2, hp, FOX_T, FOX_T), jnp.float32),
                        pltpu.VMEM((hp, FOX_T, LANES), jnp.float32),
                        pltpu.VMEM((hp, FOX_T, LANES), jnp.float32),
                        pltpu.VMEM((hp, FOX_T, LANES), jnp.float32)],
        compiler_params=_params("parallel", "parallel"),
        name="fox_attn",
    )(proj, qx, proj, kx, proj)


def _scale_q_columns(w, width, dh):
    scale = jnp.where(jnp.arange(w.shape[1]) < width, dh ** -0.5, 1.0).astype(w.dtype)
    return w * scale


def kernel(x, norm_pre, norm_post, a_w_in, a_rel_bias, a_w_out, b_w_in, b_conv_w, b_conv_b,
           b_gate_a_w, b_gate_a_b, b_gate_x_w, b_gate_x_b, b_lambda, b_w_out,
           c_w_in, c_f_bias, c_w_out):
    batch, seq, d = x.shape
    depth = norm_pre.shape[0]
    bf = lambda w: w.astype(jnp.bfloat16)
    xs = x.reshape(batch * seq, d)
    for i in range(depth):
        m, j = i % 3, i // 3
        if m == 0:
            e = a_w_out.shape[1]
            w_in = bf(_scale_q_columns(a_w_in[j], e, e // A_HEADS))
            proj = norm_proj(xs, norm_pre[i], w_in)
            o = chunk_attention(proj, a_rel_bias[j], batch=batch, seq=seq)
            xs = out_proj(o, proj, 3, bf(a_w_out[j]), xs, norm_post[i])
        elif m == 1:
            proj = norm_proj(xs, norm_pre[i], bf(b_w_in[j]), time_major_batch=batch)
            hs = rg_lru(proj, b_conv_w[j], b_conv_b[j], b_gate_a_w[j], b_gate_a_b[j],
                        b_gate_x_w[j], b_gate_x_b[j], b_lambda[j], batch=batch, seq=seq)
            xs = out_proj(hs, proj, 1, bf(b_w_out[j]), xs, norm_post[i], time_major_batch=batch)
        else:
            e = c_w_out.shape[1]
            n_f = c_w_in.shape[2] - 4 * e
            w_main = bf(_scale_q_columns(c_w_in[j][:, :4 * e], e, e // C_HEADS))
            w_f = jnp.pad(bf(c_w_in[j][:, 4 * e:]), ((0, 0), (0, LANES - n_f)))
            f_bias = jnp.pad(c_f_bias[j].astype(jnp.float32), (0, LANES - n_f)).reshape(1, LANES)
            proj, f_logit = norm_proj(xs, norm_pre[i], w_main, w_f)
            qx, kx = forget_gate_slabs(f_logit, f_bias, batch=batch, seq=seq, heads=n_f)
            o = forgetting_attention(proj, qx, kx, batch=batch, seq=seq)
            xs = out_proj(o, proj, 3, bf(c_w_out[j]), xs, norm_post[i])
    return xs.reshape(batch, seq, d)
```

```python
import functools

import jax
import jax.numpy as jnp
import numpy as np
from jax import lax
from jax.experimental import pallas as pl
from jax.experimental.pallas import tpu as pltpu

RMS_EPS = 1e-6
NEG_INF = -1e30
F32_TINY = float(np.finfo(np.float32).tiny)
LANES = 128

A_HEADS = 16
A_CHUNK = 64
A_LEFT_CHUNKS = 8
A_REL_CLIP = 256
A_HEADS_PER_STEP = 2
A_QTILE = 2 * A_CHUNK
A_LEFT = A_LEFT_CHUNKS * A_CHUNK
A_KTILE = A_LEFT + A_QTILE

RG_BLOCKS = 16
RG_GROUP_BLOCKS = 4
CONV_WIDTH = 4
RG_C = 8.0
LRU_TS = 32
LRU_GATE_ROWS = 64
LRU_SUB_STEPS = 16

C_HEADS = 16
FOX_T = 512
N_SPLIT = 3

HEADS_PER_STEP = 2
OUT_PROJ_ROWS = 256
VMEM_LIMIT_BYTES = 56 * 1024 * 1024


def _params(*sem):
    return pltpu.CompilerParams(dimension_semantics=sem, vmem_limit_bytes=VMEM_LIMIT_BYTES)


def _norm_proj_body(x_ref, gam_ref, w_ref, *rest, has_extra):
    if has_extra:
        w2_ref, o_ref, o2_ref, xn_ref = rest
    else:
        o_ref, xn_ref = rest

    @pl.when(pl.program_id(1) == 0)
    def _():
        x = x_ref[...]
        ms = jnp.mean(x * x, axis=-1, keepdims=True)
        xn_ref[...] = (x * lax.rsqrt(ms + RMS_EPS) * gam_ref[...]).astype(jnp.bfloat16)
        if has_extra:
            o2_ref[...] = jnp.dot(xn_ref[...], w2_ref[...], preferred_element_type=jnp.float32)

    res = jnp.dot(xn_ref[...], w_ref[...], preferred_element_type=jnp.float32)
    for c in range(o_ref.shape[0]):
        o_ref[c] = res[:, c * LANES:(c + 1) * LANES].astype(o_ref.dtype)


def norm_proj(x, gamma, w, w2=None, *, tm=1024, tn=1024):
    m, d = x.shape
    n = w.shape[1]
    tm, tn = min(tm, m), min(tn, n)
    assert m % tm == 0 and n % tn == 0 and tn % LANES == 0
    in_specs = [pl.BlockSpec((tm, d), lambda i, j: (i, 0)),
                pl.BlockSpec((1, d), lambda i, j: (0, 0)),
                pl.BlockSpec((d, tn), lambda i, j: (0, j))]
    out_shape = [jax.ShapeDtypeStruct((n // LANES, m, LANES), jnp.bfloat16)]
    out_specs = [pl.BlockSpec((tn // LANES, tm, LANES), lambda i, j: (j, i, 0))]
    args = [x, gamma.reshape(1, d), w]
    if w2 is not None:
        in_specs.append(pl.BlockSpec((d, w2.shape[1]), lambda i, j: (0, 0)))
        out_shape.append(jax.ShapeDtypeStruct((m, w2.shape[1]), jnp.float32))
        out_specs.append(pl.BlockSpec((tm, w2.shape[1]), lambda i, j: (i, 0)))
        args.append(w2)
    out = pl.pallas_call(
        functools.partial(_norm_proj_body, has_extra=w2 is not None),
        grid=(m // tm, n // tn),
        in_specs=in_specs, out_specs=out_specs, out_shape=out_shape,
        scratch_shapes=[pltpu.VMEM((tm, d), jnp.bfloat16)],
        compiler_params=_params("parallel", "arbitrary"),
        name="norm_proj",
    )(*args)
    return out if w2 is not None else out[0]


def _out_proj_body(a_ref, g_ref, w_ref, x_ref, gam_ref, o_ref):
    ns, tm, _ = a_ref.shape
    n_chunks = tm // OUT_PROJ_ROWS

    def gated(c):
        rs = slice(c * OUT_PROJ_ROWS, (c + 1) * OUT_PROJ_ROWS)
        g = jnp.concatenate([g_ref[s, rs, :] for s in range(ns)], axis=1).astype(jnp.float32)
        a = jnp.concatenate([a_ref[s, rs, :] for s in range(ns)], axis=1).astype(jnp.float32)
        half_g = 0.5 * g
        return (a * (half_g * (1.0 + jnp.tanh(half_g)))).astype(jnp.bfloat16)

    def finish(c, y):
        rs = slice(c * OUT_PROJ_ROWS, (c + 1) * OUT_PROJ_ROWS)
        ms = jnp.mean(y * y, axis=-1, keepdims=True)
        o_ref[rs, :] = x_ref[rs, :] + y * lax.rsqrt(ms + RMS_EPS) * gam_ref[...]

    a_next = gated(0)
    y_prev = None
    for c in range(n_chunks):
        a_cur = a_next
        if c + 1 < n_chunks:
            a_next = gated(c + 1)
        y = jnp.dot(a_cur, w_ref[...], preferred_element_type=jnp.float32)
        if y_prev is not None:
            finish(c - 1, y_prev)
        y_prev = y
    finish(n_chunks - 1, y_prev)


def out_proj(a, proj, g_block, w, x, gamma, *, tm=512):
    ns, m, _ = a.shape
    e, d = w.shape
    assert ns * LANES == e
    tm = min(tm, m)
    assert m % tm == 0 and tm % OUT_PROJ_ROWS == 0
    return pl.pallas_call(
        _out_proj_body,
        grid=(m // tm,),
        in_specs=[pl.BlockSpec((ns, tm, LANES), lambda i: (0, i, 0)),
                  pl.BlockSpec((ns, tm, LANES), lambda i: (g_block, i, 0)),
                  pl.BlockSpec((e, d), lambda i: (0, 0), pipeline_mode=pl.Buffered(1)),
                  pl.BlockSpec((tm, d), lambda i: (i, 0)),
                  pl.BlockSpec((1, d), lambda i: (0, 0))],
        out_specs=pl.BlockSpec((tm, d), lambda i: (i, 0)),
        out_shape=jax.ShapeDtypeStruct((m, d), jnp.float32),
        compiler_params=_params("parallel"),
        name="out_proj",
    )(a, proj, w, x, gamma.reshape(1, d))


def _chunk_attn_body(q_ref, k_ref, v_ref, bias_ref, o_ref, s_ref):
    hp, seq, _ = q_ref.shape
    n_tiles = seq // A_QTILE
    n_cut = A_LEFT // A_QTILE

    def rows(t):
        q0 = pl.multiple_of(t * A_QTILE, A_QTILE)
        k0 = pl.multiple_of(jnp.maximum(q0 - A_LEFT, 0), A_QTILE)
        return pl.ds(q0, A_QTILE), pl.ds(k0, A_KTILE)

    def scores(t, slot):
        q_rows, k_rows = rows(t)
        table = jnp.minimum(t, n_cut)
        for h in range(hp):
            s = lax.dot_general(q_ref[h, q_rows, :], k_ref[h, k_rows, :], (((1,), (1,)), ((), ())),
                                preferred_element_type=jnp.float32)
            s_ref[slot, h] = s + bias_ref[h, table]

    def finish(t, slot):
        q_rows, k_rows = rows(t)
        ps, ls = [], []
        for h in range(hp):
            s = s_ref[slot, h]
            p = jnp.exp(s - jnp.max(s, axis=-1, keepdims=True))
            ls.append(jnp.sum(p, axis=-1, keepdims=True))
            ps.append(p.astype(jnp.bfloat16))
        outs = [jnp.dot(ps[h], v_ref[h, k_rows, :], preferred_element_type=jnp.float32)
                for h in range(hp)]
        for h in range(hp):
            o_ref[h, q_rows, :] = (outs[h] / ls[h]).astype(o_ref.dtype)

    scores(0, 0)

    def step(i, carry):
        t = 2 * i
        scores(t + 1, 1)
        finish(t, 0)
        scores(jnp.minimum(t + 2, n_tiles - 1), 0)
        finish(t + 1, 1)
        return carry

    lax.fori_loop(0, n_tiles // 2, step, 0)


def _bias_tables_body(u_ref, o_ref):
    n_tab, qt, kt = o_ref.shape[1:]
    qi = lax.broadcasted_iota(jnp.int32, (qt, kt), 0)
    j = lax.broadcasted_iota(jnp.int32, (qt, kt), 1)
    chunk_start = (qi // A_CHUNK) * A_CHUNK
    for i in range(n_tab):
        diag = jnp.broadcast_to(u_ref[0, i:i + 1, :], (qt, u_ref.shape[2]))
        table = pltpu.roll(diag, 0, 1, stride=1, stride_axis=0)[:, :kt]
        band_start = chunk_start + (i * qt - A_LEFT)
        visible = (j >= band_start) & (j < band_start + (A_LEFT_CHUNKS + 1) * A_CHUNK)
        o_ref[0, i] = jnp.where(visible, table, NEG_INF)


def _band_bias_tables(rel_bias):
    heads = rel_bias.shape[0]
    n_tab = A_LEFT // A_QTILE + 1
    n_diag = pl.cdiv(A_QTILE + A_KTILE - 1, LANES) * LANES
    m = np.arange(n_diag)
    key_minus_query = np.where(m < A_KTILE, m, m - n_diag)
    offsets = A_QTILE * np.arange(n_tab)[:, None]
    idx = np.clip(offsets - key_minus_query[None, :], -A_REL_CLIP, A_REL_CLIP) + A_REL_CLIP
    diags = rel_bias.astype(jnp.float32)[:, idx]
    return pl.pallas_call(
        _bias_tables_body,
        grid=(heads,),
        in_specs=[pl.BlockSpec((1, n_tab, n_diag), lambda h: (h, 0, 0))],
        out_specs=pl.BlockSpec((1, n_tab, A_QTILE, A_KTILE), lambda h: (h, 0, 0, 0)),
        out_shape=jax.ShapeDtypeStruct((heads, n_tab, A_QTILE, A_KTILE), jnp.float32),
        compiler_params=_params("parallel"),
        name="bias_tables",
    )(diags)


def chunk_attention(proj, rel_bias, *, batch, seq):
    heads = A_HEADS
    hp = A_HEADS_PER_STEP
    assert proj.shape[0] == 4 * heads and seq % (2 * A_QTILE) == 0 and seq >= A_KTILE
    tables = _band_bias_tables(rel_bias)
    blk = (hp, seq, LANES)
    nhb = heads // hp
    return pl.pallas_call(
        _chunk_attn_body,
        grid=(batch, nhb),
        in_specs=[pl.BlockSpec(blk, lambda b, h: (h, b, 0)),
                  pl.BlockSpec(blk, lambda b, h: (nhb + h, b, 0)),
                  pl.BlockSpec(blk, lambda b, h: (2 * nhb + h, b, 0)),
                  pl.BlockSpec((hp,) + tables.shape[1:], lambda b, h: (h, 0, 0, 0))],
        out_specs=pl.BlockSpec(blk, lambda b, h: (h, b, 0)),
        out_shape=jax.ShapeDtypeStruct((heads, batch * seq, LANES), jnp.bfloat16),
        scratch_shapes=[pltpu.VMEM((2, hp, A_QTILE, A_KTILE), jnp.float32)],
        compiler_params=_params("parallel", "parallel"),
        name="chunk_attn",
    )(proj, proj, proj, tables)


def _lru_body(xr_ref, cw_ref, cb_ref, wg_ref, ba_ref, bx_ref, lam_ref, o_ref,
              xbuf_ref, y_ref, pre_ref, a_ref, u_ref, hs_ref, h_ref, *, groups, gw):
    ns, nb, ts, _ = o_ref.shape
    rows = ts * nb
    hist = CONV_WIDTH * nb
    sub = LRU_SUB_STEPS
    pr = sub * nb

    @pl.when(pl.program_id(0) == 0)
    def _():
        xbuf_ref[0:hist, :] = jnp.zeros((hist, xbuf_ref.shape[1]), jnp.float32)
        h_ref[...] = jnp.zeros(h_ref.shape, jnp.float32)

    r_idx = lax.broadcasted_iota(jnp.int32, (pr, pr), 0)
    k_idx = lax.broadcasted_iota(jnp.int32, (pr, pr), 1)
    perm = jnp.where(k_idx == (r_idx % nb) * sub + r_idx // nb, 1.0, 0.0).astype(jnp.bfloat16)

    for part in range(ts // sub):
        steps = slice(part * sub, (part + 1) * sub)
        x_bm = jnp.concatenate(
            [jnp.concatenate([xr_ref[c, b, steps, :] for b in range(nb)], axis=0) for c in range(ns)],
            axis=1)
        xbuf_ref[hist + part * pr:hist + (part + 1) * pr, :] = jnp.dot(
            perm, x_bm, preferred_element_type=jnp.float32)

    def conv_and_gate_matmul(gi, slot):
        cs = slice(gi * gw, (gi + 1) * gw)
        y = cb_ref[:, cs] + cw_ref[CONV_WIDTH - 1:CONV_WIDTH, cs] * xbuf_ref[hist:hist + rows, cs]
        for tap in range(CONV_WIDTH - 1):
            off = hist - (CONV_WIDTH - 1 - tap) * nb
            y = y + cw_ref[tap:tap + 1, cs] * xbuf_ref[off:off + rows, cs]
        y_ref[slot] = y
        pre_ref[slot] = jnp.dot(y.astype(jnp.bfloat16), wg_ref[gi], preferred_element_type=jnp.float32)

    def gates(gi, slot):
        cs = slice(gi * gw, (gi + 1) * gw)
        half_ba = 0.5 * ba_ref[:, cs]
        half_bx = 0.5 * bx_ref[:, cs]
        lam = lam_ref[:, cs]
        softplus_neg_lam = jnp.maximum(-lam, 0.0) + jnp.log1p(jnp.exp(-jnp.abs(lam)))
        half_coef = (-0.5 * RG_C) * softplus_neg_lam
        for k in range(rows // LRU_GATE_ROWS):
            rk = slice(k * LRU_GATE_ROWS, (k + 1) * LRU_GATE_ROWS)
            tanh_r = jnp.tanh(pre_ref[slot, rk, :gw] + half_ba)
            tanh_i = jnp.tanh(pre_ref[slot, rk, gw:] + half_bx)
            log_a = half_coef * tanh_r + half_coef
            a = jnp.exp(log_a)
            quarter = (-0.25 * jnp.tanh(log_a)) * (1.0 + a * a)
            half_mult = quarter * lax.rsqrt(jnp.maximum(quarter, F32_TINY))
            a_ref[rk, cs] = a
            u_ref[rk, cs] = half_mult * ((tanh_i + 1.0) * y_ref[slot, rk, :])

    conv_and_gate_matmul(0, 0)
    for gi in range(groups):
        if gi + 1 < groups:
            conv_and_gate_matmul(gi + 1, (gi + 1) % 2)
        gates(gi, gi % 2)

    xbuf_ref[0:hist, :] = xbuf_ref[rows:rows + hist, :]

    def time_step(t, h):
        rs = pl.ds(pl.multiple_of(t * nb, nb), nb)
        h = a_ref[rs, :] * h + u_ref[rs, :]
        hs_ref[rs, :] = h.astype(hs_ref.dtype)
        return h

    h_ref[...] = lax.fori_loop(0, ts, time_step, h_ref[...])

    for part in range(ts // sub):
        steps = slice(part * sub, (part + 1) * sub)
        hs_bm = jnp.dot(perm, hs_ref[part * pr:(part + 1) * pr, :],
                        preferred_element_type=jnp.float32).astype(o_ref.dtype)
        for c in range(ns):
            for b in range(nb):
                o_ref[c, b, steps, :] = hs_bm[b * sub:(b + 1) * sub, c * LANES:(c + 1) * LANES]


def _gate_group_weights(w_a, w_x):
    nb, bs, _ = w_a.shape
    groups = nb // RG_GROUP_BLOCKS
    gw = RG_GROUP_BLOCKS * bs
    eye = jnp.eye(RG_GROUP_BLOCKS, dtype=w_a.dtype)

    def block_diag(w):
        w = w.reshape(groups, RG_GROUP_BLOCKS, bs, bs)
        return (w[:, :, :, None, :] * (0.5 * eye)[None, :, None, :, None]).reshape(groups, gw, gw)

    return jnp.concatenate([block_diag(w_a), block_diag(w_x)], axis=2).astype(jnp.bfloat16)


def rg_lru(proj, conv_w, conv_b, w_a, b_a, w_x, b_x, lam, *, batch, seq):
    ns = proj.shape[0] // 2
    width = ns * LANES
    gw = RG_GROUP_BLOCKS * (width // RG_BLOCKS)
    groups = width // gw
    ts = min(LRU_TS, seq)
    rows = ts * batch
    assert seq % ts == 0 and gw % LANES == 0 and ts >= CONV_WIDTH and ts % LRU_SUB_STEPS == 0
    assert batch == LRU_SUB_STEPS and rows % LRU_GATE_ROWS == 0
    wg = _gate_group_weights(w_a, w_x)
    vec = lambda v: v.reshape(1, width).astype(jnp.float32)
    const2 = lambda s: (0, 0)
    blk = pl.BlockSpec((ns, batch, ts, LANES), lambda s: (0, 0, s, 0))
    out = pl.pallas_call(
        functools.partial(_lru_body, groups=groups, gw=gw),
        grid=(seq // ts,),
        in_specs=[blk,
                  pl.BlockSpec((CONV_WIDTH, width), const2),
                  pl.BlockSpec((1, width), const2),
                  pl.BlockSpec(wg.shape, lambda s: (0, 0, 0), pipeline_mode=pl.Buffered(1)),
                  pl.BlockSpec((1, width), const2),
                  pl.BlockSpec((1, width), const2),
                  pl.BlockSpec((1, width), const2)],
        out_specs=blk,
        out_shape=jax.ShapeDtypeStruct((ns, batch, seq, LANES), jnp.bfloat16),
        scratch_shapes=[pltpu.VMEM((rows + CONV_WIDTH * batch, width), jnp.float32),
                        pltpu.VMEM((2, rows, gw), jnp.float32),
                        pltpu.VMEM((2, rows, 2 * gw), jnp.float32),
                        pltpu.VMEM((rows, width), jnp.float32),
                        pltpu.VMEM((rows, width), jnp.float32),
                        pltpu.VMEM((rows, width), jnp.bfloat16),
                        pltpu.VMEM((batch, width), jnp.float32)],
        compiler_params=_params("arbitrary"),
        name="rg_lru",
    )(proj.reshape(2 * ns, batch, seq, LANES), conv_w.astype(jnp.float32), vec(conv_b), wg,
      vec(b_a), vec(b_x), vec(lam))
    return out.reshape(ns, batch * seq, LANES)


FOX_PIECE_LANES = 16 * N_SPLIT


def _split3(x):
    hi = x.astype(jnp.bfloat16).astype(jnp.float32)
    r1 = x - hi
    mid = r1.astype(jnp.bfloat16).astype(jnp.float32)
    lo = (r1 - mid).astype(jnp.bfloat16).astype(jnp.float32)
    return hi, mid, lo


def _forget_gate_body(f_ref, fb_ref, qx_ref, kx_ref, *, heads):
    z = f_ref[0] + fb_ref[...]
    c = jnp.minimum(z, 0.0) - jnp.log1p(jnp.exp(-jnp.abs(z)))
    n = c.shape[0]
    row = lax.broadcasted_iota(jnp.int32, c.shape, 0)
    d = 1
    while d < n:
        c = c + jnp.where(row >= d, pltpu.roll(c, d, 0), 0.0)
        d *= 2
    lane = lax.broadcasted_iota(jnp.int32, c.shape, 1)
    hi, mid, lo = _split3(c)
    pieces = jnp.where(lane < heads, hi,
                       jnp.where(lane < 2 * heads, pltpu.roll(mid, heads, 1),
                                 jnp.where(lane < 3 * heads, pltpu.roll(lo, 2 * heads, 1), 0.0)))
    qx_ref[...] = pieces.astype(qx_ref.dtype)
    kx_ref[...] = pltpu.roll(-pieces, FOX_PIECE_LANES, 1).astype(kx_ref.dtype)


def forget_gate_slabs(f_logit, f_bias_padded, *, batch, seq, heads):
    assert heads * N_SPLIT == FOX_PIECE_LANES and 2 * FOX_PIECE_LANES <= LANES
    f3 = f_logit.reshape(batch, seq, LANES)
    shape = jax.ShapeDtypeStruct((batch * seq, LANES), jnp.bfloat16)
    spec = pl.BlockSpec((seq, LANES), lambda b: (b, 0))
    return pl.pallas_call(
        functools.partial(_forget_gate_body, heads=heads),
        grid=(batch,),
        in_specs=[pl.BlockSpec((1, seq, LANES), lambda b: (b, 0, 0)),
                  pl.BlockSpec((1, LANES), lambda b: (0, 0))],
        out_specs=[spec, spec],
        out_shape=[shape, shape],
        compiler_params=_params("parallel"),
        name="forget_gate",
    )(f3, f_bias_padded)


def _fox_body(q_ref, qx_ref, k_ref, kx_ref, v_ref, o_ref, s_ref, m_ref, l_ref, acc_ref):
    hp, seq, dh = q_ref.shape
    t = FOX_T
    heads = FOX_PIECE_LANES // N_SPLIT
    steps = [(i, j) for i in range(seq // t) for j in range(i + 1)]
    lane = lax.broadcasted_iota(jnp.int32, (1, LANES), 1)

    def augment(x_ref, px_ref, h, r, first_lane):
        rel = lane - (first_lane + pl.program_id(1) * hp + h)
        select = (rel == 0) | (rel == heads) | (rel == 2 * heads)
        extra = jnp.where(select, jnp.ones((), px_ref.dtype), px_ref[r, :])
        return jnp.concatenate([x_ref[h, r, :], extra], axis=1)

    def rows(b):
        return slice(b * t, (b + 1) * t)

    def scores(n, slot):
        i, j = steps[n]
        for h in range(hp):
            q_aug = augment(q_ref, qx_ref, h, rows(i), FOX_PIECE_LANES)
            k_aug = augment(k_ref, kx_ref, h, rows(j), 0)
            s = lax.dot_general(q_aug, k_aug, (((1,), (1,)), ((), ())),
                                preferred_element_type=jnp.float32)
            if i == j:
                r = lax.broadcasted_iota(jnp.int32, s.shape, 0)
                c = lax.broadcasted_iota(jnp.int32, s.shape, 1)
                s = jnp.where(c <= r, s, NEG_INF)
            s_ref[slot, h] = s

    def finish(n, slot):
        i, j = steps[n]
        ps, alphas = [], []
        for h in range(hp):
            s = s_ref[slot, h]
            m_cur = jnp.max(s, axis=1, keepdims=True)
            if j == 0:
                m_next = jnp.broadcast_to(m_cur, (t, LANES))
            else:
                m_prev = m_ref[h]
                m_next = jnp.maximum(m_prev, m_cur)
                alphas.append(jnp.exp(m_prev - m_next))
            m_ref[h] = m_next
            p = jnp.exp(s - jnp.tile(m_next, (1, t // LANES)))
            l_cur = jnp.sum(p, axis=1, keepdims=True)
            if j == 0:
                l_ref[h] = jnp.broadcast_to(l_cur, (t, LANES))
            else:
                l_ref[h] = l_cur + alphas[h] * l_ref[h]
            ps.append(p.astype(jnp.bfloat16))
        pvs = [jnp.dot(ps[h], v_ref[h, rows(j), :], preferred_element_type=jnp.float32)
               for h in range(hp)]
        for h in range(hp):
            acc = pvs[h] if j == 0 else alphas[h] * acc_ref[h] + pvs[h]
            if j == i:
                o_ref[h, rows(i), :] = (acc / l_ref[h]).astype(o_ref.dtype)
            else:
                acc_ref[h] = acc

    scores(0, 0)
    for n in range(len(steps)):
        if n + 1 < len(steps):
            scores(n + 1, (n + 1) % 2)
        finish(n, n % 2)


def forgetting_attention(proj, qx, kx, *, batch, seq):
    heads = C_HEADS
    hp = HEADS_PER_STEP
    assert proj.shape[0] == 4 * heads and seq % FOX_T == 0
    blk = (hp, seq, LANES)
    nhb = heads // hp
    return pl.pallas_call(
        _fox_body,
        grid=(batch, nhb),
        in_specs=[pl.BlockSpec(blk, lambda b, h: (h, b, 0)),
                  pl.BlockSpec((seq, LANES), lambda b, h: (b, 0)),
                  pl.BlockSpec(blk, lambda b, h: (nhb + h, b, 0)),
                  pl.BlockSpec((seq, LANES), lambda b, h: (b, 0)),
                  pl.BlockSpec(blk, lambda b, h: (2 * nhb + h, b, 0))],
        out_specs=pl.BlockSpec(blk, lambda b, h: (h, b, 0)),
        out_shape=jax.ShapeDtypeStruct((heads, batch * seq, LANES), jnp.bfloat16),
        scratch_shapes=[pltpu.VMEM((2, hp, FOX_T, FOX_T), jnp.float32),
                        pltpu.VMEM((hp, FOX_T, LANES), jnp.float32),
                        pltpu.VMEM((hp, FOX_T, LANES), jnp.float32),
                        pltpu.VMEM((hp, FOX_T, LANES), jnp.float32)],
        compiler_params=_params("parallel", "parallel"),
        name="fox_attn",
    )(proj, qx, proj, kx, proj)


def _scale_q_columns(w, width, dh):
    scale = jnp.where(jnp.arange(w.shape[1]) < width, dh ** -0.5, 1.0).astype(w.dtype)
    return w * scale


def kernel(x, norm_pre, norm_post, a_w_in, a_rel_bias, a_w_out, b_w_in, b_conv_w, b_conv_b,
           b_gate_a_w, b_gate_a_b, b_gate_x_w, b_gate_x_b, b_lambda, b_w_out,
           c_w_in, c_f_bias, c_w_out):
    batch, seq, d = x.shape
    depth = norm_pre.shape[0]
    bf = lambda w: w.astype(jnp.bfloat16)
    xs = x.reshape(batch * seq, d)
    for i in range(depth):
        m, j = i % 3, i // 3
        if m == 0:
            e = a_w_out.shape[1]
            w_in = bf(_scale_q_columns(a_w_in[j], e, e // A_HEADS))
            proj = norm_proj(xs, norm_pre[i], w_in)
            o = chunk_attention(proj, a_rel_bias[j], batch=batch, seq=seq)
            xs = out_proj(o, proj, 3, bf(a_w_out[j]), xs, norm_post[i])
        elif m == 1:
            proj = norm_proj(xs, norm_pre[i], bf(b_w_in[j]))
            hs = rg_lru(proj, b_conv_w[j], b_conv_b[j], b_gate_a_w[j], b_gate_a_b[j],
                        b_gate_x_w[j], b_gate_x_b[j], b_lambda[j], batch=batch, seq=seq)
            xs = out_proj(hs, proj, 1, bf(b_w_out[j]), xs, norm_post[i])
        else:
            e = c_w_out.shape[1]
            n_f = c_w_in.shape[2] - 4 * e
            w_main = bf(_scale_q_columns(c_w_in[j][:, :4 * e], e, e // C_HEADS))
            w_f = jnp.pad(bf(c_w_in[j][:, 4 * e:]), ((0, 0), (0, LANES - n_f)))
            f_bias = jnp.pad(c_f_bias[j].astype(jnp.float32), (0, LANES - n_f)).reshape(1, LANES)
            proj, f_logit = norm_proj(xs, norm_pre[i], w_main, w_f)
            qx, kx = forget_gate_slabs(f_logit, f_bias, batch=batch, seq=seq, heads=n_f)
            o = forgetting_attention(proj, qx, kx, batch=batch, seq=seq)
            xs = out_proj(o, proj, 3, bf(c_w_out[j]), xs, norm_post[i])
    return xs.reshape(batch, seq, d)
```

```python
import functools

import jax
import jax.numpy as jnp
import numpy as np
from jax import lax
from jax.experimental import pallas as pl
from jax.experimental.pallas import tpu as pltpu

RMS_EPS = 1e-6
NEG_INF = -1e30
F32_TINY = float(np.finfo(np.float32).tiny)
LANES = 128

A_HEADS = 16
A_CHUNK = 64
A_LEFT_CHUNKS = 8
A_REL_CLIP = 256
A_HEADS_PER_STEP = 2
A_QTILE = 2 * A_CHUNK
A_LEFT = A_LEFT_CHUNKS * A_CHUNK
A_KTILE = A_LEFT + A_QTILE

RG_BLOCKS = 16
RG_GROUP_BLOCKS = 4
CONV_WIDTH = 4
RG_C = 8.0
LRU_TS = 32
LRU_GATE_ROWS = 64
LRU_SUB_STEPS = 16

C_HEADS = 16
FOX_T = 512
N_SPLIT = 3

HEADS_PER_STEP = 2
OUT_PROJ_ROWS = 256
VMEM_LIMIT_BYTES = 56 * 1024 * 1024


def _params(*sem):
    return pltpu.CompilerParams(dimension_semantics=sem, vmem_limit_bytes=VMEM_LIMIT_BYTES)


def _norm_proj_body(x_ref, gam_ref, w_ref, *rest, has_extra):
    if has_extra:
        w2_ref, o_ref, o2_ref, xn_ref = rest
    else:
        o_ref, xn_ref = rest

    @pl.when(pl.program_id(1) == 0)
    def _():
        x = x_ref[...]
        ms = jnp.mean(x * x, axis=-1, keepdims=True)
        xn_ref[...] = (x * lax.rsqrt(ms + RMS_EPS) * gam_ref[...]).astype(jnp.bfloat16)
        if has_extra:
            o2_ref[...] = jnp.dot(xn_ref[...], w2_ref[...], preferred_element_type=jnp.float32)

    res = jnp.dot(xn_ref[...], w_ref[...], preferred_element_type=jnp.float32)
    for c in range(o_ref.shape[0]):
        o_ref[c] = res[:, c * LANES:(c + 1) * LANES].astype(o_ref.dtype)


def norm_proj(x, gamma, w, w2=None, *, tm=1024, tn=1024):
    m, d = x.shape
    n = w.shape[1]
    tm, tn = min(tm, m), min(tn, n)
    assert m % tm == 0 and n % tn == 0 and tn % LANES == 0
    in_specs = [pl.BlockSpec((tm, d), lambda i, j: (i, 0)),
                pl.BlockSpec((1, d), lambda i, j: (0, 0)),
                pl.BlockSpec((d, tn), lambda i, j: (0, j))]
    out_shape = [jax.ShapeDtypeStruct((n // LANES, m, LANES), jnp.bfloat16)]
    out_specs = [pl.BlockSpec((tn // LANES, tm, LANES), lambda i, j: (j, i, 0))]
    args = [x, gamma.reshape(1, d), w]
    if w2 is not None:
        in_specs.append(pl.BlockSpec((d, w2.shape[1]), lambda i, j: (0, 0)))
        out_shape.append(jax.ShapeDtypeStruct((m, w2.shape[1]), jnp.float32))
        out_specs.append(pl.BlockSpec((tm, w2.shape[1]), lambda i, j: (i, 0)))
        args.append(w2)
    out = pl.pallas_call(
        functools.partial(_norm_proj_body, has_extra=w2 is not None),
        grid=(m // tm, n // tn),
        in_specs=in_specs, out_specs=out_specs, out_shape=out_shape,
        scratch_shapes=[pltpu.VMEM((tm, d), jnp.bfloat16)],
        compiler_params=_params("parallel", "arbitrary"),
        name="norm_proj",
    )(*args)
    return out if w2 is not None else out[0]


def _proj_body(h_ref, w_ref, *rest, has_extra):
    if has_extra:
        w2_ref, o_ref, o2_ref = rest

        @pl.when(pl.program_id(1) == 0)
        def _():
            o2_ref[...] = jnp.dot(h_ref[...], w2_ref[...], preferred_element_type=jnp.float32)
    else:
        (o_ref,) = rest

    res = jnp.dot(h_ref[...], w_ref[...], preferred_element_type=jnp.float32)
    for c in range(o_ref.shape[0]):
        o_ref[c] = res[:, c * LANES:(c + 1) * LANES].astype(o_ref.dtype)


def proj_slabs(h, w, w2=None, *, tm=1024, tn=2048):
    m, d = h.shape
    n = w.shape[1]
    tm, tn = min(tm, m), min(tn, n)
    assert m % tm == 0 and n % tn == 0 and tn % LANES == 0
    in_specs = [pl.BlockSpec((tm, d), lambda i, j: (i, 0)),
                pl.BlockSpec((d, tn), lambda i, j: (0, j))]
    out_shape = [jax.ShapeDtypeStruct((n // LANES, m, LANES), jnp.bfloat16)]
    out_specs = [pl.BlockSpec((tn // LANES, tm, LANES), lambda i, j: (j, i, 0))]
    args = [h, w]
    if w2 is not None:
        in_specs.append(pl.BlockSpec((d, w2.shape[1]), lambda i, j: (0, 0)))
        out_shape.append(jax.ShapeDtypeStruct((m, w2.shape[1]), jnp.float32))
        out_specs.append(pl.BlockSpec((tm, w2.shape[1]), lambda i, j: (i, 0)))
        args.append(w2)
    out = pl.pallas_call(
        functools.partial(_proj_body, has_extra=w2 is not None),
        grid=(m // tm, n // tn),
        in_specs=in_specs, out_specs=out_specs, out_shape=out_shape,
        compiler_params=_params("parallel", "arbitrary"),
        name="proj",
    )(*args)
    return out if w2 is not None else out[0]


def _out_proj_body(a_ref, g_ref, w_ref, x_ref, gam_ref, *rest, has_next):
    if has_next:
        gam_next_ref, o_ref, h_next_ref = rest
    else:
        (o_ref,) = rest
    ns, tm, _ = a_ref.shape
    n_chunks = tm // OUT_PROJ_ROWS

    def gated(c):
        rs = slice(c * OUT_PROJ_ROWS, (c + 1) * OUT_PROJ_ROWS)
        g = jnp.concatenate([g_ref[s, rs, :] for s in range(ns)], axis=1).astype(jnp.float32)
        a = jnp.concatenate([a_ref[s, rs, :] for s in range(ns)], axis=1).astype(jnp.float32)
        half_g = 0.5 * g
        return (a * (half_g * (1.0 + jnp.tanh(half_g)))).astype(jnp.bfloat16)

    def finish(c, y):
        rs = slice(c * OUT_PROJ_ROWS, (c + 1) * OUT_PROJ_ROWS)
        ms = jnp.mean(y * y, axis=-1, keepdims=True)
        x_new = x_ref[rs, :] + y * lax.rsqrt(ms + RMS_EPS) * gam_ref[...]
        o_ref[rs, :] = x_new
        if has_next:
            ms_new = jnp.mean(x_new * x_new, axis=-1, keepdims=True)
            h_next_ref[rs, :] = (x_new * lax.rsqrt(ms_new + RMS_EPS)
                                 * gam_next_ref[...]).astype(h_next_ref.dtype)

    a_next = gated(0)
    y_prev = None
    for c in range(n_chunks):
        a_cur = a_next
        if c + 1 < n_chunks:
            a_next = gated(c + 1)
        y = jnp.dot(a_cur, w_ref[...], preferred_element_type=jnp.float32)
        if y_prev is not None:
            finish(c - 1, y_prev)
        y_prev = y
    finish(n_chunks - 1, y_prev)


def out_proj(a, proj, g_block, w, x, gamma, gamma_next=None, *, tm=512):
    ns, m, _ = a.shape
    e, d = w.shape
    assert ns * LANES == e
    tm = min(tm, m)
    assert m % tm == 0 and tm % OUT_PROJ_ROWS == 0
    row_spec = pl.BlockSpec((tm, d), lambda i: (i, 0))
    vec_spec = pl.BlockSpec((1, d), lambda i: (0, 0))
    in_specs = [pl.BlockSpec((ns, tm, LANES), lambda i: (0, i, 0)),
                pl.BlockSpec((ns, tm, LANES), lambda i: (g_block, i, 0)),
                pl.BlockSpec((e, d), lambda i: (0, 0), pipeline_mode=pl.Buffered(1)),
                row_spec, vec_spec]
    out_shape = [jax.ShapeDtypeStruct((m, d), jnp.float32)]
    out_specs = [row_spec]
    args = [a, proj, w, x, gamma.reshape(1, d)]
    if gamma_next is not None:
        in_specs.append(vec_spec)
        out_shape.append(jax.ShapeDtypeStruct((m, d), jnp.bfloat16))
        out_specs.append(row_spec)
        args.append(gamma_next.reshape(1, d))
    out = pl.pallas_call(
        functools.partial(_out_proj_body, has_next=gamma_next is not None),
        grid=(m // tm,),
        in_specs=in_specs, out_specs=out_specs, out_shape=out_shape,
        compiler_params=_params("parallel"),
        name="out_proj",
    )(*args)
    return out if gamma_next is not None else out[0]


def _chunk_attn_body(q_ref, k_ref, v_ref, bias_ref, o_ref, s_ref):
    hp, seq, _ = q_ref.shape
    n_tiles = seq // A_QTILE
    n_cut = A_LEFT // A_QTILE

    def window(t):
        q0 = t * A_QTILE
        k0 = max(q0 - A_LEFT, 0)
        n_keys = q0 + A_QTILE - k0
        return slice(q0, q0 + A_QTILE), slice(k0, k0 + n_keys), n_keys

    def scores(t, slot):
        q_rows, k_rows, n_keys = window(t)
        for h in range(hp):
            s = lax.dot_general(q_ref[h, q_rows, :], k_ref[h, k_rows, :], (((1,), (1,)), ((), ())),
                                preferred_element_type=jnp.float32)
            s_ref[slot, h, :, :n_keys] = s + bias_ref[h, min(t, n_cut), :, :n_keys]

    def finish(t, slot):
        q_rows, k_rows, n_keys = window(t)
        ps, ls = [], []
        for h in range(hp):
            s = s_ref[slot, h, :, :n_keys]
            p = jnp.exp(s - jnp.max(s, axis=-1, keepdims=True))
            ls.append(jnp.sum(p, axis=-1, keepdims=True))
            ps.append(p.astype(jnp.bfloat16))
        outs = [jnp.dot(ps[h], v_ref[h, k_rows, :], preferred_element_type=jnp.float32)
                for h in range(hp)]
        for h in range(hp):
            o_ref[h, q_rows, :] = (outs[h] / ls[h]).astype(o_ref.dtype)

    scores(0, 0)
    for t in range(n_tiles):
        if t + 1 < n_tiles:
            scores(t + 1, (t + 1) % 2)
        finish(t, t % 2)


def _bias_tables_body(u_ref, o_ref):
    n_tab, qt, kt = o_ref.shape[1:]
    qi = lax.broadcasted_iota(jnp.int32, (qt, kt), 0)
    j = lax.broadcasted_iota(jnp.int32, (qt, kt), 1)
    chunk_start = (qi // A_CHUNK) * A_CHUNK
    for i in range(n_tab):
        diag = jnp.broadcast_to(u_ref[0, i:i + 1, :], (qt, u_ref.shape[2]))
        table = pltpu.roll(diag, 0, 1, stride=1, stride_axis=0)[:, :kt]
        band_start = chunk_start + (i * qt - A_LEFT)
        visible = (j >= band_start) & (j < band_start + (A_LEFT_CHUNKS + 1) * A_CHUNK)
        o_ref[0, i] = jnp.where(visible, table, NEG_INF)


def _band_bias_tables(rel_bias):
    heads = rel_bias.shape[0]
    n_tab = A_LEFT // A_QTILE + 1
    n_diag = pl.cdiv(A_QTILE + A_KTILE - 1, LANES) * LANES
    m = np.arange(n_diag)
    key_minus_query = np.where(m < A_KTILE, m, m - n_diag)
    offsets = A_QTILE * np.arange(n_tab)[:, None]
    idx = np.clip(offsets - key_minus_query[None, :], -A_REL_CLIP, A_REL_CLIP) + A_REL_CLIP
    diags = rel_bias.astype(jnp.float32)[:, idx]
    return pl.pallas_call(
        _bias_tables_body,
        grid=(heads,),
        in_specs=[pl.BlockSpec((1, n_tab, n_diag), lambda h: (h, 0, 0))],
        out_specs=pl.BlockSpec((1, n_tab, A_QTILE, A_KTILE), lambda h: (h, 0, 0, 0)),
        out_shape=jax.ShapeDtypeStruct((heads, n_tab, A_QTILE, A_KTILE), jnp.float32),
        compiler_params=_params("parallel"),
        name="bias_tables",
    )(diags)


def chunk_attention(proj, rel_bias, *, batch, seq):
    heads = A_HEADS
    hp = A_HEADS_PER_STEP
    assert proj.shape[0] == 4 * heads and seq % (2 * A_QTILE) == 0 and seq >= A_KTILE
    tables = _band_bias_tables(rel_bias)
    blk = (hp, seq, LANES)
    nhb = heads // hp
    return pl.pallas_call(
        _chunk_attn_body,
        grid=(batch, nhb),
        in_specs=[pl.BlockSpec(blk, lambda b, h: (h, b, 0)),
                  pl.BlockSpec(blk, lambda b, h: (nhb + h, b, 0)),
                  pl.BlockSpec(blk, lambda b, h: (2 * nhb + h, b, 0)),
                  pl.BlockSpec((hp,) + tables.shape[1:], lambda b, h: (h, 0, 0, 0))],
        out_specs=pl.BlockSpec(blk, lambda b, h: (h, b, 0)),
        out_shape=jax.ShapeDtypeStruct((heads, batch * seq, LANES), jnp.bfloat16),
        scratch_shapes=[pltpu.VMEM((2, hp, A_QTILE, A_KTILE), jnp.float32)],
        compiler_params=_params("parallel", "parallel"),
        name="chunk_attn",
    )(proj, proj, proj, tables)


def _lru_body(xr_ref, cw_ref, cb_ref, wg_ref, ba_ref, bx_ref, lam_ref, o_ref,
              xbuf_ref, y_ref, pre_ref, a_ref, u_ref, hs_ref, h_ref, *, groups, gw):
    ns, nb, ts, _ = o_ref.shape
    rows = ts * nb
    hist = CONV_WIDTH * nb
    sub = LRU_SUB_STEPS
    pr = sub * nb

    @pl.when(pl.program_id(0) == 0)
    def _():
        xbuf_ref[0:hist, :] = jnp.zeros((hist, xbuf_ref.shape[1]), jnp.float32)
        h_ref[...] = jnp.zeros(h_ref.shape, jnp.float32)

    r_idx = lax.broadcasted_iota(jnp.int32, (pr, pr), 0)
    k_idx = lax.broadcasted_iota(jnp.int32, (pr, pr), 1)
    perm = jnp.where(k_idx == (r_idx % nb) * sub + r_idx // nb, 1.0, 0.0).astype(jnp.bfloat16)

    for part in range(ts // sub):
        steps = slice(part * sub, (part + 1) * sub)
        x_bm = jnp.concatenate(
            [jnp.concatenate([xr_ref[c, b, steps, :] for b in range(nb)], axis=0) for c in range(ns)],
            axis=1)
        xbuf_ref[hist + part * pr:hist + (part + 1) * pr, :] = jnp.dot(
            perm, x_bm, preferred_element_type=jnp.float32)

    def conv_and_gate_matmul(gi, slot):
        cs = slice(gi * gw, (gi + 1) * gw)
        y = cb_ref[:, cs] + cw_ref[CONV_WIDTH - 1:CONV_WIDTH, cs] * xbuf_ref[hist:hist + rows, cs]
        for tap in range(CONV_WIDTH - 1):
            off = hist - (CONV_WIDTH - 1 - tap) * nb
            y = y + cw_ref[tap:tap + 1, cs] * xbuf_ref[off:off + rows, cs]
        y_ref[slot] = y
        pre_ref[slot] = jnp.dot(y.astype(jnp.bfloat16), wg_ref[gi], preferred_element_type=jnp.float32)

    def gates(gi, slot):
        cs = slice(gi * gw, (gi + 1) * gw)
        half_ba = 0.5 * ba_ref[:, cs]
        half_bx = 0.5 * bx_ref[:, cs]
        lam = lam_ref[:, cs]
        softplus_neg_lam = jnp.maximum(-lam, 0.0) + jnp.log1p(jnp.exp(-jnp.abs(lam)))
        half_coef = (-0.5 * RG_C) * softplus_neg_lam
        for k in range(rows // LRU_GATE_ROWS):
            rk = slice(k * LRU_GATE_ROWS, (k + 1) * LRU_GATE_ROWS)
            tanh_r = jnp.tanh(pre_ref[slot, rk, :gw] + half_ba)
            tanh_i = jnp.tanh(pre_ref[slot, rk, gw:] + half_bx)
            log_a = half_coef * tanh_r + half_coef
            a = jnp.exp(log_a)
            quarter = (-0.25 * jnp.tanh(log_a)) * (1.0 + a * a)
            half_mult = quarter * lax.rsqrt(jnp.maximum(quarter, F32_TINY))
            a_ref[rk, cs] = a
            u_ref[rk, cs] = half_mult * ((tanh_i + 1.0) * y_ref[slot, rk, :])

    conv_and_gate_matmul(0, 0)
    for gi in range(groups):
        if gi + 1 < groups:
            conv_and_gate_matmul(gi + 1, (gi + 1) % 2)
        gates(gi, gi % 2)

    xbuf_ref[0:hist, :] = xbuf_ref[rows:rows + hist, :]

    def time_step(t, h):
        rs = pl.ds(pl.multiple_of(t * nb, nb), nb)
        h = a_ref[rs, :] * h + u_ref[rs, :]
        hs_ref[rs, :] = h.astype(hs_ref.dtype)
        return h

    h_ref[...] = lax.fori_loop(0, ts, time_step, h_ref[...])

    for part in range(ts // sub):
        steps = slice(part * sub, (part + 1) * sub)
        hs_bm = jnp.dot(perm, hs_ref[part * pr:(part + 1) * pr, :],
                        preferred_element_type=jnp.float32).astype(o_ref.dtype)
        for c in range(ns):
            for b in range(nb):
                o_ref[c, b, steps, :] = hs_bm[b * sub:(b + 1) * sub, c * LANES:(c + 1) * LANES]


def _gate_group_weights(w_a, w_x):
    nb, bs, _ = w_a.shape
    groups = nb // RG_GROUP_BLOCKS
    gw = RG_GROUP_BLOCKS * bs
    eye = jnp.eye(RG_GROUP_BLOCKS, dtype=w_a.dtype)

    def block_diag(w):
        w = w.reshape(groups, RG_GROUP_BLOCKS, bs, bs)
        return (w[:, :, :, None, :] * (0.5 * eye)[None, :, None, :, None]).reshape(groups, gw, gw)

    return jnp.concatenate([block_diag(w_a), block_diag(w_x)], axis=2).astype(jnp.bfloat16)


def rg_lru(proj, conv_w, conv_b, w_a, b_a, w_x, b_x, lam, *, batch, seq):
    ns = proj.shape[0] // 2
    width = ns * LANES
    gw = RG_GROUP_BLOCKS * (width // RG_BLOCKS)
    groups = width // gw
    ts = min(LRU_TS, seq)
    rows = ts * batch
    assert seq % ts == 0 and gw % LANES == 0 and ts >= CONV_WIDTH and ts % LRU_SUB_STEPS == 0
    assert batch == LRU_SUB_STEPS and rows % LRU_GATE_ROWS == 0
    wg = _gate_group_weights(w_a, w_x)
    vec = lambda v: v.reshape(1, width).astype(jnp.float32)
    const2 = lambda s: (0, 0)
    blk = pl.BlockSpec((ns, batch, ts, LANES), lambda s: (0, 0, s, 0))
    out = pl.pallas_call(
        functools.partial(_lru_body, groups=groups, gw=gw),
        grid=(seq // ts,),
        in_specs=[blk,
                  pl.BlockSpec((CONV_WIDTH, width), const2),
                  pl.BlockSpec((1, width), const2),
                  pl.BlockSpec(wg.shape, lambda s: (0, 0, 0), pipeline_mode=pl.Buffered(1)),
                  pl.BlockSpec((1, width), const2),
                  pl.BlockSpec((1, width), const2),
                  pl.BlockSpec((1, width), const2)],
        out_specs=blk,
        out_shape=jax.ShapeDtypeStruct((ns, batch, seq, LANES), jnp.bfloat16),
        scratch_shapes=[pltpu.VMEM((rows + CONV_WIDTH * batch, width), jnp.float32),
                        pltpu.VMEM((2, rows, gw), jnp.float32),
                        pltpu.VMEM((2, rows, 2 * gw), jnp.float32),
                        pltpu.VMEM((rows, width), jnp.float32),
                        pltpu.VMEM((rows, width), jnp.float32),
                        pltpu.VMEM((rows, width), jnp.bfloat16),
                        pltpu.VMEM((batch, width), jnp.float32)],
        compiler_params=_params("arbitrary"),
        name="rg_lru",
    )(proj.reshape(2 * ns, batch, seq, LANES), conv_w.astype(jnp.float32), vec(conv_b), wg,
      vec(b_a), vec(b_x), vec(lam))
    return out.reshape(ns, batch * seq, LANES)


FOX_PIECE_LANES = 16 * N_SPLIT


def _split3(x):
    hi = x.astype(jnp.bfloat16).astype(jnp.float32)
    r1 = x - hi
    mid = r1.astype(jnp.bfloat16).astype(jnp.float32)
    lo = (r1 - mid).astype(jnp.bfloat16).astype(jnp.float32)
    return hi, mid, lo


def _forget_gate_body(f_ref, fb_ref, qx_ref, kx_ref, *, heads):
    z = f_ref[0] + fb_ref[...]
    c = jnp.minimum(z, 0.0) - jnp.log1p(jnp.exp(-jnp.abs(z)))
    n = c.shape[0]
    row = lax.broadcasted_iota(jnp.int32, c.shape, 0)
    d = 1
    while d < n:
        c = c + jnp.where(row >= d, pltpu.roll(c, d, 0), 0.0)
        d *= 2
    lane = lax.broadcasted_iota(jnp.int32, c.shape, 1)
    hi, mid, lo = _split3(c)
    pieces = jnp.where(lane < heads, hi,
                       jnp.where(lane < 2 * heads, pltpu.roll(mid, heads, 1),
                                 jnp.where(lane < 3 * heads, pltpu.roll(lo, 2 * heads, 1), 0.0)))
    qx_ref[...] = pieces.astype(qx_ref.dtype)
    kx_ref[...] = pltpu.roll(-pieces, FOX_PIECE_LANES, 1).astype(kx_ref.dtype)


def forget_gate_slabs(f_logit, f_bias_padded, *, batch, seq, heads):
    assert heads * N_SPLIT == FOX_PIECE_LANES and 2 * FOX_PIECE_LANES <= LANES
    f3 = f_logit.reshape(batch, seq, LANES)
    shape = jax.ShapeDtypeStruct((batch * seq, LANES), jnp.bfloat16)
    spec = pl.BlockSpec((seq, LANES), lambda b: (b, 0))
    return pl.pallas_call(
        functools.partial(_forget_gate_body, heads=heads),
        grid=(batch,),
        in_specs=[pl.BlockSpec((1, seq, LANES), lambda b: (b, 0, 0)),
                  pl.BlockSpec((1, LANES), lambda b: (0, 0))],
        out_specs=[spec, spec],
        out_shape=[shape, shape],
        compiler_params=_params("parallel"),
        name="forget_gate",
    )(f3, f_bias_padded)


def _fox_body(q_ref, qx_ref, k_ref, kx_ref, v_ref, o_ref, s_ref, m_ref, l_ref, acc_ref):
    hp, seq, dh = q_ref.shape
    t = FOX_T
    heads = FOX_PIECE_LANES // N_SPLIT
    steps = [(i, j) for i in range(seq // t) for j in range(i + 1)]
    lane = lax.broadcasted_iota(jnp.int32, (1, LANES), 1)

    def augment(x_ref, px_ref, h, r, first_lane):
        rel = lane - (first_lane + pl.program_id(1) * hp + h)
        select = (rel == 0) | (rel == heads) | (rel == 2 * heads)
        extra = jnp.where(select, jnp.ones((), px_ref.dtype), px_ref[r, :])
        return jnp.concatenate([x_ref[h, r, :], extra], axis=1)

    def rows(b):
        return slice(b * t, (b + 1) * t)

    def scores(n, slot):
        i, j = steps[n]
        for h in range(hp):
            q_aug = augment(q_ref, qx_ref, h, rows(i), FOX_PIECE_LANES)
            k_aug = augment(k_ref, kx_ref, h, rows(j), 0)
            s = lax.dot_general(q_aug, k_aug, (((1,), (1,)), ((), ())),
                                preferred_element_type=jnp.float32)
            if i == j:
                r = lax.broadcasted_iota(jnp.int32, s.shape, 0)
                c = lax.broadcasted_iota(jnp.int32, s.shape, 1)
                s = jnp.where(c <= r, s, NEG_INF)
            s_ref[slot, h] = s

    def finish(n, slot):
        i, j = steps[n]
        ps, alphas = [], []
        for h in range(hp):
            s = s_ref[slot, h]
            m_cur = jnp.max(s, axis=1, keepdims=True)
            if j == 0:
                m_next = jnp.broadcast_to(m_cur, (t, LANES))
            else:
                m_prev = m_ref[h]
                m_next = jnp.maximum(m_prev, m_cur)
                alphas.append(jnp.exp(m_prev - m_next))
            m_ref[h] = m_next
            p = jnp.exp(s - jnp.tile(m_next, (1, t // LANES)))
            l_cur = jnp.sum(p, axis=1, keepdims=True)
            if j == 0:
                l_ref[h] = jnp.broadcast_to(l_cur, (t, LANES))
            else:
                l_ref[h] = l_cur + alphas[h] * l_ref[h]
            ps.append(p.astype(jnp.bfloat16))
        pvs = [jnp.dot(ps[h], v_ref[h, rows(j), :], preferred_element_type=jnp.float32)
               for h in range(hp)]
        for h in range(hp):
            acc = pvs[h] if j == 0 else alphas[h] * acc_ref[h] + pvs[h]
            if j == i:
                o_ref[h, rows(i), :] = (acc / l_ref[h]).astype(o_ref.dtype)
            else:
                acc_ref[h] = acc

    scores(0, 0)
    for n in range(len(steps)):
        if n + 1 < len(steps):
            scores(n + 1, (n + 1) % 2)
        finish(n, n % 2)


def forgetting_attention(proj, qx, kx, *, batch, seq):
    heads = C_HEADS
    hp = HEADS_PER_STEP
    assert proj.shape[0] == 4 * heads and seq % FOX_T == 0
    blk = (hp, seq, LANES)
    nhb = heads // hp
    return pl.pallas_call(
        _fox_body,
        grid=(batch, nhb),
        in_specs=[pl.BlockSpec(blk, lambda b, h: (h, b, 0)),
                  pl.BlockSpec((seq, LANES), lambda b, h: (b, 0)),
                  pl.BlockSpec(blk, lambda b, h: (nhb + h, b, 0)),
                  pl.BlockSpec((seq, LANES), lambda b, h: (b, 0)),
                  pl.BlockSpec(blk, lambda b, h: (2 * nhb + h, b, 0))],
        out_specs=pl.BlockSpec(blk, lambda b, h: (h, b, 0)),
        out_shape=jax.ShapeDtypeStruct((heads, batch * seq, LANES), jnp.bfloat16),
        scratch_shapes=[pltpu.VMEM((2, hp, FOX_T, FOX_T), jnp.float32),
                        pltpu.VMEM((hp, FOX_T, LANES), jnp.float32),
                        pltpu.VMEM((hp, FOX_T, LANES), jnp.float32),
                        pltpu.VMEM((hp, FOX_T, LANES), jnp.float32)],
        compiler_params=_params("parallel", "parallel"),
        name="fox_attn",
    )(proj, qx, proj, kx, proj)


def _scale_q_columns(w, width, dh):
    scale = jnp.where(jnp.arange(w.shape[1]) < width, dh ** -0.5, 1.0).astype(w.dtype)
    return w * scale


def _input_projection(xs, h, gamma, w, w2=None, *, tn=2048):
    if h is None:
        return norm_proj(xs, gamma, w, w2)
    return proj_slabs(h, w, w2, tn=tn)


def kernel(x, norm_pre, norm_post, a_w_in, a_rel_bias, a_w_out, b_w_in, b_conv_w, b_conv_b,
           b_gate_a_w, b_gate_a_b, b_gate_x_w, b_gate_x_b, b_lambda, b_w_out,
           c_w_in, c_f_bias, c_w_out):
    batch, seq, d = x.shape
    depth = norm_pre.shape[0]
    bf = lambda w: w.astype(jnp.bfloat16)
    xs = x.reshape(batch * seq, d)
    h = None
    for i in range(depth):
        m, j = i % 3, i // 3
        gamma_next = norm_pre[i + 1] if i + 1 < depth else None
        if m == 0:
            e = a_w_out.shape[1]
            w_in = bf(_scale_q_columns(a_w_in[j], e, e // A_HEADS))
            proj = _input_projection(xs, h, norm_pre[i], w_in)
            o = chunk_attention(proj, a_rel_bias[j], batch=batch, seq=seq)
            res = out_proj(o, proj, 3, bf(a_w_out[j]), xs, norm_post[i], gamma_next)
        elif m == 1:
            proj = _input_projection(xs, h, norm_pre[i], bf(b_w_in[j]), tn=b_w_in.shape[2] // 4)
            hs = rg_lru(proj, b_conv_w[j], b_conv_b[j], b_gate_a_w[j], b_gate_a_b[j],
                        b_gate_x_w[j], b_gate_x_b[j], b_lambda[j], batch=batch, seq=seq)
            res = out_proj(hs, proj, 1, bf(b_w_out[j]), xs, norm_post[i], gamma_next)
        else:
            e = c_w_out.shape[1]
            n_f = c_w_in.shape[2] - 4 * e
            w_main = bf(_scale_q_columns(c_w_in[j][:, :4 * e], e, e // C_HEADS))
            w_f = jnp.pad(bf(c_w_in[j][:, 4 * e:]), ((0, 0), (0, LANES - n_f)))
            f_bias = jnp.pad(c_f_bias[j].astype(jnp.float32), (0, LANES - n_f)).reshape(1, LANES)
            proj, f_logit = _input_projection(xs, h, norm_pre[i], w_main, w_f)
            qx, kx = forget_gate_slabs(f_logit, f_bias, batch=batch, seq=seq, heads=n_f)
            o = forgetting_attention(proj, qx, kx, batch=batch, seq=seq)
            res = out_proj(o, proj, 3, bf(c_w_out[j]), xs, norm_post[i], gamma_next)
        xs, h = res if gamma_next is not None else (res, None)
    return xs.reshape(batch, seq, d)
```

```python
import functools

import jax
import jax.numpy as jnp
import numpy as np
from jax import lax
from jax.experimental import pallas as pl
from jax.experimental.pallas import tpu as pltpu

RMS_EPS = 1e-6
NEG_INF = -1e30
F32_TINY = float(np.finfo(np.float32).tiny)
LOG2E = float(np.log2(np.e))
LANES = 128

A_HEADS = 16
A_CHUNK = 64
A_LEFT_CHUNKS = 8
A_REL_CLIP = 256
A_HEADS_PER_STEP = 2
A_QTILE = 2 * A_CHUNK
A_LEFT = A_LEFT_CHUNKS * A_CHUNK
A_KTILE = A_LEFT + A_QTILE

RG_BLOCKS = 16
RG_GROUP_BLOCKS = 4
CONV_WIDTH = 4
RG_C = 8.0
LRU_TS = 32
LRU_GATE_ROWS = 64
LRU_SUB_STEPS = 16

C_HEADS = 16
FOX_T = 512
FOX_ROWS = 128
N_SPLIT = 3

HEADS_PER_STEP = 2
OUT_PROJ_ROWS = 256
VMEM_LIMIT_BYTES = 56 * 1024 * 1024


def _params(*sem):
    return pltpu.CompilerParams(dimension_semantics=sem, vmem_limit_bytes=VMEM_LIMIT_BYTES)


def _norm_proj_body(x_ref, gam_ref, w_ref, *rest, has_extra):
    if has_extra:
        w2_ref, o_ref, o2_ref, xn_ref = rest
    else:
        o_ref, xn_ref = rest

    @pl.when(pl.program_id(1) == 0)
    def _():
        x = x_ref[...]
        ms = jnp.mean(x * x, axis=-1, keepdims=True)
        xn_ref[...] = (x * lax.rsqrt(ms + RMS_EPS) * gam_ref[...]).astype(jnp.bfloat16)
        if has_extra:
            o2_ref[...] = jnp.dot(xn_ref[...], w2_ref[...], preferred_element_type=jnp.float32)

    res = jnp.dot(xn_ref[...], w_ref[...], preferred_element_type=jnp.float32)
    for c in range(o_ref.shape[0]):
        o_ref[c] = res[:, c * LANES:(c + 1) * LANES].astype(o_ref.dtype)


def norm_proj(x, gamma, w, w2=None, *, tm=1024, tn=1024):
    m, d = x.shape
    n = w.shape[1]
    tm, tn = min(tm, m), min(tn, n)
    assert m % tm == 0 and n % tn == 0 and tn % LANES == 0
    in_specs = [pl.BlockSpec((tm, d), lambda i, j: (i, 0)),
                pl.BlockSpec((1, d), lambda i, j: (0, 0)),
                pl.BlockSpec((d, tn), lambda i, j: (0, j))]
    out_shape = [jax.ShapeDtypeStruct((n // LANES, m, LANES), jnp.bfloat16)]
    out_specs = [pl.BlockSpec((tn // LANES, tm, LANES), lambda i, j: (j, i, 0))]
    args = [x, gamma.reshape(1, d), w]
    if w2 is not None:
        in_specs.append(pl.BlockSpec((d, w2.shape[1]), lambda i, j: (0, 0)))
        out_shape.append(jax.ShapeDtypeStruct((m, w2.shape[1]), jnp.float32))
        out_specs.append(pl.BlockSpec((tm, w2.shape[1]), lambda i, j: (i, 0)))
        args.append(w2)
    out = pl.pallas_call(
        functools.partial(_norm_proj_body, has_extra=w2 is not None),
        grid=(m // tm, n // tn),
        in_specs=in_specs, out_specs=out_specs, out_shape=out_shape,
        scratch_shapes=[pltpu.VMEM((tm, d), jnp.bfloat16)],
        compiler_params=_params("parallel", "arbitrary"),
        name="norm_proj",
    )(*args)
    return out if w2 is not None else out[0]


def _proj_body(h_ref, w_ref, *rest, has_extra):
    if has_extra:
        w2_ref, o_ref, o2_ref = rest

        @pl.when(pl.program_id(1) == 0)
        def _():
            o2_ref[...] = jnp.dot(h_ref[...], w2_ref[...], preferred_element_type=jnp.float32)
    else:
        (o_ref,) = rest

    res = jnp.dot(h_ref[...], w_ref[...], preferred_element_type=jnp.float32)
    for c in range(o_ref.shape[0]):
        o_ref[c] = res[:, c * LANES:(c + 1) * LANES].astype(o_ref.dtype)


def proj_slabs(h, w, w2=None, *, tm=1024, tn=2048):
    m, d = h.shape
    n = w.shape[1]
    tm, tn = min(tm, m), min(tn, n)
    assert m % tm == 0 and n % tn == 0 and tn % LANES == 0
    in_specs = [pl.BlockSpec((tm, d), lambda i, j: (i, 0)),
                pl.BlockSpec((d, tn), lambda i, j: (0, j))]
    out_shape = [jax.ShapeDtypeStruct((n // LANES, m, LANES), jnp.bfloat16)]
    out_specs = [pl.BlockSpec((tn // LANES, tm, LANES), lambda i, j: (j, i, 0))]
    args = [h, w]
    if w2 is not None:
        in_specs.append(pl.BlockSpec((d, w2.shape[1]), lambda i, j: (0, 0)))
        out_shape.append(jax.ShapeDtypeStruct((m, w2.shape[1]), jnp.float32))
        out_specs.append(pl.BlockSpec((tm, w2.shape[1]), lambda i, j: (i, 0)))
        args.append(w2)
    out = pl.pallas_call(
        functools.partial(_proj_body, has_extra=w2 is not None),
        grid=(m // tm, n // tn),
        in_specs=in_specs, out_specs=out_specs, out_shape=out_shape,
        compiler_params=_params("parallel", "arbitrary"),
        name="proj",
    )(*args)
    return out if w2 is not None else out[0]


def _out_proj_body(a_ref, g_ref, w_ref, x_ref, gam_ref, *rest, has_next):
    if has_next:
        gam_next_ref, o_ref, h_next_ref = rest
    else:
        (o_ref,) = rest
    ns, tm, _ = a_ref.shape
    n_chunks = tm // OUT_PROJ_ROWS

    def gated(c):
        rs = slice(c * OUT_PROJ_ROWS, (c + 1) * OUT_PROJ_ROWS)
        g = jnp.concatenate([g_ref[s, rs, :] for s in range(ns)], axis=1).astype(jnp.float32)
        a = jnp.concatenate([a_ref[s, rs, :] for s in range(ns)], axis=1).astype(jnp.float32)
        half_g = 0.5 * g
        return (a * (half_g * (1.0 + jnp.tanh(half_g)))).astype(jnp.bfloat16)

    def finish(c, y):
        rs = slice(c * OUT_PROJ_ROWS, (c + 1) * OUT_PROJ_ROWS)
        ms = jnp.mean(y * y, axis=-1, keepdims=True)
        x_new = x_ref[rs, :] + y * lax.rsqrt(ms + RMS_EPS) * gam_ref[...]
        o_ref[rs, :] = x_new
        if has_next:
            ms_new = jnp.mean(x_new * x_new, axis=-1, keepdims=True)
            h_next_ref[rs, :] = (x_new * lax.rsqrt(ms_new + RMS_EPS)
                                 * gam_next_ref[...]).astype(h_next_ref.dtype)

    a_next = gated(0)
    y_prev = None
    for c in range(n_chunks):
        a_cur = a_next
        if c + 1 < n_chunks:
            a_next = gated(c + 1)
        y = jnp.dot(a_cur, w_ref[...], preferred_element_type=jnp.float32)
        if y_prev is not None:
            finish(c - 1, y_prev)
        y_prev = y
    finish(n_chunks - 1, y_prev)


def out_proj(a, proj, g_block, w, x, gamma, gamma_next=None, *, tm=512):
    ns, m, _ = a.shape
    e, d = w.shape
    assert ns * LANES == e
    tm = min(tm, m)
    assert m % tm == 0 and tm % OUT_PROJ_ROWS == 0
    row_spec = pl.BlockSpec((tm, d), lambda i: (i, 0))
    vec_spec = pl.BlockSpec((1, d), lambda i: (0, 0))
    in_specs = [pl.BlockSpec((ns, tm, LANES), lambda i: (0, i, 0)),
                pl.BlockSpec((ns, tm, LANES), lambda i: (g_block, i, 0)),
                pl.BlockSpec((e, d), lambda i: (0, 0), pipeline_mode=pl.Buffered(1)),
                row_spec, vec_spec]
    out_shape = [jax.ShapeDtypeStruct((m, d), jnp.float32)]
    out_specs = [row_spec]
    args = [a, proj, w, x, gamma.reshape(1, d)]
    if gamma_next is not None:
        in_specs.append(vec_spec)
        out_shape.append(jax.ShapeDtypeStruct((m, d), jnp.bfloat16))
        out_specs.append(row_spec)
        args.append(gamma_next.reshape(1, d))
    out = pl.pallas_call(
        functools.partial(_out_proj_body, has_next=gamma_next is not None),
        grid=(m // tm,),
        in_specs=in_specs, out_specs=out_specs, out_shape=out_shape,
        compiler_params=_params("parallel"),
        name="out_proj",
    )(*args)
    return out if gamma_next is not None else out[0]


def _chunk_attn_body(q_ref, k_ref, v_ref, bias_ref, o_ref, s_ref):
    hp, seq, _ = q_ref.shape
    n_tiles = seq // A_QTILE
    n_cut = A_LEFT // A_QTILE

    def window(t):
        q0 = t * A_QTILE
        k0 = max(q0 - A_LEFT, 0)
        n_keys = q0 + A_QTILE - k0
        return slice(q0, q0 + A_QTILE), slice(k0, k0 + n_keys), n_keys

    def scores(t, slot):
        q_rows, k_rows, n_keys = window(t)
        for h in range(hp):
            s = lax.dot_general(q_ref[h, q_rows, :], k_ref[h, k_rows, :], (((1,), (1,)), ((), ())),
                                preferred_element_type=jnp.float32)
            s_ref[slot, h, :, :n_keys] = s + bias_ref[h, min(t, n_cut), :, :n_keys]

    def finish(t, slot):
        q_rows, k_rows, n_keys = window(t)
        ps, ls = [], []
        for h in range(hp):
            s = s_ref[slot, h, :, :n_keys]
            p = jnp.exp2(s - jnp.max(s, axis=-1, keepdims=True))
            ls.append(jnp.sum(p, axis=-1, keepdims=True))
            ps.append(p.astype(jnp.bfloat16))
        outs = [jnp.dot(ps[h], v_ref[h, k_rows, :], preferred_element_type=jnp.float32)
                for h in range(hp)]
        for h in range(hp):
            o_ref[h, q_rows, :] = (outs[h] / ls[h]).astype(o_ref.dtype)

    scores(0, 0)
    for t in range(n_tiles):
        if t + 1 < n_tiles:
            scores(t + 1, (t + 1) % 2)
        finish(t, t % 2)


def _bias_tables_body(u_ref, o_ref):
    n_tab, qt, kt = o_ref.shape[1:]
    qi = lax.broadcasted_iota(jnp.int32, (qt, kt), 0)
    j = lax.broadcasted_iota(jnp.int32, (qt, kt), 1)
    chunk_start = (qi // A_CHUNK) * A_CHUNK
    for i in range(n_tab):
        diag = jnp.broadcast_to(u_ref[0, i:i + 1, :], (qt, u_ref.shape[2]))
        table = pltpu.roll(diag, 0, 1, stride=1, stride_axis=0)[:, :kt]
        band_start = chunk_start + (i * qt - A_LEFT)
        visible = (j >= band_start) & (j < band_start + (A_LEFT_CHUNKS + 1) * A_CHUNK)
        o_ref[0, i] = jnp.where(visible, LOG2E * table, NEG_INF)


def _band_bias_tables(rel_bias):
    heads = rel_bias.shape[0]
    n_tab = A_LEFT // A_QTILE + 1
    n_diag = pl.cdiv(A_QTILE + A_KTILE - 1, LANES) * LANES
    m = np.arange(n_diag)
    key_minus_query = np.where(m < A_KTILE, m, m - n_diag)
    offsets = A_QTILE * np.arange(n_tab)[:, None]
    idx = np.clip(offsets - key_minus_query[None, :], -A_REL_CLIP, A_REL_CLIP) + A_REL_CLIP
    diags = rel_bias.astype(jnp.float32)[:, idx]
    return pl.pallas_call(
        _bias_tables_body,
        grid=(heads,),
        in_specs=[pl.BlockSpec((1, n_tab, n_diag), lambda h: (h, 0, 0))],
        out_specs=pl.BlockSpec((1, n_tab, A_QTILE, A_KTILE), lambda h: (h, 0, 0, 0)),
        out_shape=jax.ShapeDtypeStruct((heads, n_tab, A_QTILE, A_KTILE), jnp.float32),
        compiler_params=_params("parallel"),
        name="bias_tables",
    )(diags)


def chunk_attention(proj, rel_bias, *, batch, seq):
    heads = A_HEADS
    hp = A_HEADS_PER_STEP
    assert proj.shape[0] == 4 * heads and seq % (2 * A_QTILE) == 0 and seq >= A_KTILE
    tables = _band_bias_tables(rel_bias)
    blk = (hp, seq, LANES)
    nhb = heads // hp
    return pl.pallas_call(
        _chunk_attn_body,
        grid=(nhb, batch),
        in_specs=[pl.BlockSpec(blk, lambda h, b: (h, b, 0)),
                  pl.BlockSpec(blk, lambda h, b: (nhb + h, b, 0)),
                  pl.BlockSpec(blk, lambda h, b: (2 * nhb + h, b, 0)),
                  pl.BlockSpec((hp,) + tables.shape[1:], lambda h, b: (h, 0, 0, 0))],
        out_specs=pl.BlockSpec(blk, lambda h, b: (h, b, 0)),
        out_shape=jax.ShapeDtypeStruct((heads, batch * seq, LANES), jnp.bfloat16),
        scratch_shapes=[pltpu.VMEM((2, hp, A_QTILE, A_KTILE), jnp.float32)],
        compiler_params=_params("parallel", "parallel"),
        name="chunk_attn",
    )(proj, proj, proj, tables)


def _lru_body(xr_ref, cw_ref, cb_ref, wg_ref, ba_ref, bx_ref, lam_ref, o_ref,
              xbuf_ref, y_ref, pre_ref, a_ref, u_ref, hs_ref, h_ref, *, groups, gw):
    ns, nb, ts, _ = o_ref.shape
    rows = ts * nb
    hist = CONV_WIDTH * nb
    sub = LRU_SUB_STEPS
    pr = sub * nb

    @pl.when(pl.program_id(0) == 0)
    def _():
        xbuf_ref[0:hist, :] = jnp.zeros((hist, xbuf_ref.shape[1]), jnp.float32)
        h_ref[...] = jnp.zeros(h_ref.shape, jnp.float32)

    r_idx = lax.broadcasted_iota(jnp.int32, (pr, pr), 0)
    k_idx = lax.broadcasted_iota(jnp.int32, (pr, pr), 1)
    perm = jnp.where(k_idx == (r_idx % nb) * sub + r_idx // nb, 1.0, 0.0).astype(jnp.bfloat16)

    for part in range(ts // sub):
        steps = slice(part * sub, (part + 1) * sub)
        x_bm = jnp.concatenate(
            [jnp.concatenate([xr_ref[c, b, steps, :] for b in range(nb)], axis=0) for c in range(ns)],
            axis=1)
        xbuf_ref[hist + part * pr:hist + (part + 1) * pr, :] = jnp.dot(
            perm, x_bm, preferred_element_type=jnp.float32)

    def conv_and_gate_matmul(gi, slot):
        cs = slice(gi * gw, (gi + 1) * gw)
        y = cb_ref[:, cs] + cw_ref[CONV_WIDTH - 1:CONV_WIDTH, cs] * xbuf_ref[hist:hist + rows, cs]
        for tap in range(CONV_WIDTH - 1):
            off = hist - (CONV_WIDTH - 1 - tap) * nb
            y = y + cw_ref[tap:tap + 1, cs] * xbuf_ref[off:off + rows, cs]
        y_ref[slot] = y
        pre_ref[slot] = jnp.dot(y.astype(jnp.bfloat16), wg_ref[gi], preferred_element_type=jnp.float32)

    def gates(gi, slot):
        cs = slice(gi * gw, (gi + 1) * gw)
        half_ba = 0.5 * ba_ref[:, cs]
        half_bx = 0.5 * bx_ref[:, cs]
        lam = lam_ref[:, cs]
        softplus_neg_lam = jnp.maximum(-lam, 0.0) + jnp.log1p(jnp.exp(-jnp.abs(lam)))
        half_coef = (-0.5 * RG_C) * softplus_neg_lam
        for k in range(rows // LRU_GATE_ROWS):
            rk = slice(k * LRU_GATE_ROWS, (k + 1) * LRU_GATE_ROWS)
            tanh_r = jnp.tanh(pre_ref[slot, rk, :gw] + half_ba)
            tanh_i = jnp.tanh(pre_ref[slot, rk, gw:] + half_bx)
            log_a = half_coef * tanh_r + half_coef
            a = jnp.exp(log_a)
            quarter = (-0.25 * jnp.tanh(log_a)) * (1.0 + a * a)
            half_mult = quarter * lax.rsqrt(jnp.maximum(quarter, F32_TINY))
            a_ref[rk, cs] = a
            u_ref[rk, cs] = half_mult * ((tanh_i + 1.0) * y_ref[slot, rk, :])

    conv_and_gate_matmul(0, 0)
    for gi in range(groups):
        if gi + 1 < groups:
            conv_and_gate_matmul(gi + 1, (gi + 1) % 2)
        gates(gi, gi % 2)

    xbuf_ref[0:hist, :] = xbuf_ref[rows:rows + hist, :]

    def time_step(t, h):
        rs = pl.ds(pl.multiple_of(t * nb, nb), nb)
        h = a_ref[rs, :] * h + u_ref[rs, :]
        hs_ref[rs, :] = h.astype(hs_ref.dtype)
        return h

    h_ref[...] = lax.fori_loop(0, ts, time_step, h_ref[...])

    for part in range(ts // sub):
        steps = slice(part * sub, (part + 1) * sub)
        hs_bm = jnp.dot(perm, hs_ref[part * pr:(part + 1) * pr, :],
                        preferred_element_type=jnp.float32).astype(o_ref.dtype)
        for c in range(ns):
            for b in range(nb):
                o_ref[c, b, steps, :] = hs_bm[b * sub:(b + 1) * sub, c * LANES:(c + 1) * LANES]


def _gate_group_weights(w_a, w_x):
    nb, bs, _ = w_a.shape
    groups = nb // RG_GROUP_BLOCKS
    gw = RG_GROUP_BLOCKS * bs

    def block_diag(w):
        w = (0.5 * w).astype(jnp.bfloat16).reshape(groups, RG_GROUP_BLOCKS, bs, bs)
        rows = [jnp.pad(w[:, n], ((0, 0), (0, 0), (n * bs, gw - (n + 1) * bs)))
                for n in range(RG_GROUP_BLOCKS)]
        return jnp.concatenate(rows, axis=1)

    return jnp.concatenate([block_diag(w_a), block_diag(w_x)], axis=2)


def rg_lru(proj, conv_w, conv_b, w_a, b_a, w_x, b_x, lam, *, batch, seq):
    ns = proj.shape[0] // 2
    width = ns * LANES
    gw = RG_GROUP_BLOCKS * (width // RG_BLOCKS)
    groups = width // gw
    ts = min(LRU_TS, seq)
    rows = ts * batch
    assert seq % ts == 0 and gw % LANES == 0 and ts >= CONV_WIDTH and ts % LRU_SUB_STEPS == 0
    assert batch == LRU_SUB_STEPS and rows % LRU_GATE_ROWS == 0
    wg = _gate_group_weights(w_a, w_x)
    vec = lambda v: v.reshape(1, width).astype(jnp.float32)
    const2 = lambda s: (0, 0)
    blk = pl.BlockSpec((ns, batch, ts, LANES), lambda s: (0, 0, s, 0))
    out = pl.pallas_call(
        functools.partial(_lru_body, groups=groups, gw=gw),
        grid=(seq // ts,),
        in_specs=[blk,
                  pl.BlockSpec((CONV_WIDTH, width), const2),
                  pl.BlockSpec((1, width), const2),
                  pl.BlockSpec(wg.shape, lambda s: (0, 0, 0), pipeline_mode=pl.Buffered(1)),
                  pl.BlockSpec((1, width), const2),
                  pl.BlockSpec((1, width), const2),
                  pl.BlockSpec((1, width), const2)],
        out_specs=blk,
        out_shape=jax.ShapeDtypeStruct((ns, batch, seq, LANES), jnp.bfloat16),
        scratch_shapes=[pltpu.VMEM((rows + CONV_WIDTH * batch, width), jnp.float32),
                        pltpu.VMEM((2, rows, gw), jnp.float32),
                        pltpu.VMEM((2, rows, 2 * gw), jnp.float32),
                        pltpu.VMEM((rows, width), jnp.float32),
                        pltpu.VMEM((rows, width), jnp.float32),
                        pltpu.VMEM((rows, width), jnp.bfloat16),
                        pltpu.VMEM((batch, width), jnp.float32)],
        compiler_params=_params("arbitrary"),
        name="rg_lru",
    )(proj.reshape(2 * ns, batch, seq, LANES), conv_w.astype(jnp.float32), vec(conv_b), wg,
      vec(b_a), vec(b_x), vec(lam))
    return out.reshape(ns, batch * seq, LANES)


FOX_PIECE_LANES = 16 * N_SPLIT


def _split3(x):
    hi = x.astype(jnp.bfloat16).astype(jnp.float32)
    r1 = x - hi
    mid = r1.astype(jnp.bfloat16).astype(jnp.float32)
    lo = (r1 - mid).astype(jnp.bfloat16).astype(jnp.float32)
    return hi, mid, lo


def _forget_gate_body(f_ref, fb_ref, qx_ref, kx_ref, *, heads):
    z = f_ref[0] + fb_ref[...]
    c = jnp.minimum(z, 0.0) - jnp.log1p(jnp.exp(-jnp.abs(z)))
    n = c.shape[0]
    row = lax.broadcasted_iota(jnp.int32, c.shape, 0)
    d = 1
    while d < n:
        c = c + jnp.where(row >= d, pltpu.roll(c, d, 0), 0.0)
        d *= 2
    lane = lax.broadcasted_iota(jnp.int32, c.shape, 1)
    hi, mid, lo = _split3(LOG2E * c)
    pieces = jnp.where(lane < heads, hi,
                       jnp.where(lane < 2 * heads, pltpu.roll(mid, heads, 1),
                                 jnp.where(lane < 3 * heads, pltpu.roll(lo, 2 * heads, 1), 0.0)))
    qx_ref[...] = pieces.astype(qx_ref.dtype)
    kx_ref[...] = pltpu.roll(-pieces, FOX_PIECE_LANES, 1).astype(kx_ref.dtype)


def forget_gate_slabs(f_logit, f_bias_padded, *, batch, seq, heads):
    assert heads * N_SPLIT == FOX_PIECE_LANES and 2 * FOX_PIECE_LANES <= LANES
    f3 = f_logit.reshape(batch, seq, LANES)
    shape = jax.ShapeDtypeStruct((batch * seq, LANES), jnp.bfloat16)
    spec = pl.BlockSpec((seq, LANES), lambda b: (b, 0))
    return pl.pallas_call(
        functools.partial(_forget_gate_body, heads=heads),
        grid=(batch,),
        in_specs=[pl.BlockSpec((1, seq, LANES), lambda b: (b, 0, 0)),
                  pl.BlockSpec((1, LANES), lambda b: (0, 0))],
        out_specs=[spec, spec],
        out_shape=[shape, shape],
        compiler_params=_params("parallel"),
        name="forget_gate",
    )(f3, f_bias_padded)


def _fox_body(q_ref, qx_ref, k_ref, kx_ref, v_ref, o_ref, s_ref, m_ref, l_ref, acc_ref):
    hp, seq, dh = q_ref.shape
    t = FOX_T
    heads = FOX_PIECE_LANES // N_SPLIT
    steps = [(i, j) for i in range(seq // t) for j in range(i + 1)]
    lane = lax.broadcasted_iota(jnp.int32, (1, LANES), 1)

    def augment(x_ref, px_ref, h, r, first_lane):
        rel = lane - (first_lane + pl.program_id(1) * hp + h)
        select = (rel == 0) | (rel == heads) | (rel == 2 * heads)
        extra = jnp.where(select, jnp.ones((), px_ref.dtype), px_ref[r, :])
        return jnp.concatenate([x_ref[h, r, :], extra], axis=1)

    def parts(i, j):
        if i != j:
            return [(0, t, t)]
        return [(0, t // 2, t // 2), (t // 2, t, t)]

    def scores(n, slot):
        i, j = steps[n]
        for h in range(hp):
            for r0, r1, n_keys in parts(i, j):
                q_aug = augment(q_ref, qx_ref, h, slice(i * t + r0, i * t + r1), FOX_PIECE_LANES)
                k_aug = augment(k_ref, kx_ref, h, slice(j * t, j * t + n_keys), 0)
                s = lax.dot_general(q_aug, k_aug, (((1,), (1,)), ((), ())),
                                    preferred_element_type=jnp.float32)
                if i == j:
                    r = lax.broadcasted_iota(jnp.int32, s.shape, 0) + r0
                    c = lax.broadcasted_iota(jnp.int32, s.shape, 1)
                    s = jnp.where(c <= r, s, NEG_INF)
                s_ref[slot, h, r0:r1, :n_keys] = s

    def finish(n, slot):
        i, j = steps[n]
        work = [(h, rc, rc + FOX_ROWS, n_keys) for h in range(hp) for r0, r1, n_keys in parts(i, j)
                for rc in range(r0, r1, FOX_ROWS)]
        for h, r0, r1, n_keys in work:
            s = s_ref[slot, h, r0:r1, :n_keys]
            m_cur = jnp.max(s, axis=1, keepdims=True)
            if j == 0:
                m_next = jnp.broadcast_to(m_cur, (r1 - r0, LANES))
            else:
                m_prev = m_ref[h, r0:r1, :]
                m_next = jnp.maximum(m_prev, m_cur)
                alpha = jnp.exp2(m_prev - m_next)
            m_ref[h, r0:r1, :] = m_next
            p = jnp.exp2(s - jnp.tile(m_next, (1, n_keys // LANES)))
            l_cur = jnp.sum(p, axis=1, keepdims=True)
            pv = jnp.dot(p.astype(jnp.bfloat16), v_ref[h, j * t:j * t + n_keys, :],
                         preferred_element_type=jnp.float32)
            if j == 0:
                l_new = jnp.broadcast_to(l_cur, (r1 - r0, LANES))
                acc = pv
            else:
                l_new = l_cur + alpha * l_ref[h, r0:r1, :]
                acc = alpha * acc_ref[h, r0:r1, :] + pv
            if j == i:
                o_ref[h, i * t + r0:i * t + r1, :] = (acc / l_new).astype(o_ref.dtype)
            else:
                l_ref[h, r0:r1, :] = l_new
                acc_ref[h, r0:r1, :] = acc

    scores(0, 0)
    for n in range(len(steps)):
        if n + 1 < len(steps):
            scores(n + 1, (n + 1) % 2)
        finish(n, n % 2)


def forgetting_attention(proj, qx, kx, *, batch, seq):
    heads = C_HEADS
    hp = HEADS_PER_STEP
    assert proj.shape[0] == 4 * heads and seq % FOX_T == 0
    blk = (hp, seq, LANES)
    nhb = heads // hp
    return pl.pallas_call(
        _fox_body,
        grid=(batch, nhb),
        in_specs=[pl.BlockSpec(blk, lambda b, h: (h, b, 0)),
                  pl.BlockSpec((seq, LANES), lambda b, h: (b, 0)),
                  pl.BlockSpec(blk, lambda b, h: (nhb + h, b, 0)),
                  pl.BlockSpec((seq, LANES), lambda b, h: (b, 0)),
                  pl.BlockSpec(blk, lambda b, h: (2 * nhb + h, b, 0))],
        out_specs=pl.BlockSpec(blk, lambda b, h: (h, b, 0)),
        out_shape=jax.ShapeDtypeStruct((heads, batch * seq, LANES), jnp.bfloat16),
        scratch_shapes=[pltpu.VMEM((2, hp, FOX_T, FOX_T), jnp.float32),
                        pltpu.VMEM((hp, FOX_T, LANES), jnp.float32),
                        pltpu.VMEM((hp, FOX_T, LANES), jnp.float32),
                        pltpu.VMEM((hp, FOX_T, LANES), jnp.float32)],
        compiler_params=_params("parallel", "parallel"),
        name="fox_attn",
    )(proj, qx, proj, kx, proj)


def _scale_q_columns(w, width, dh):
    scale = jnp.where(jnp.arange(w.shape[1]) < width, LOG2E * dh ** -0.5, 1.0).astype(w.dtype)
    return w * scale


def _input_projection(xs, h, gamma, w, w2=None, *, tn=2048):
    if h is None:
        return norm_proj(xs, gamma, w, w2)
    return proj_slabs(h, w, w2, tn=tn)


def kernel(x, norm_pre, norm_post, a_w_in, a_rel_bias, a_w_out, b_w_in, b_conv_w, b_conv_b,
           b_gate_a_w, b_gate_a_b, b_gate_x_w, b_gate_x_b, b_lambda, b_w_out,
           c_w_in, c_f_bias, c_w_out):
    batch, seq, d = x.shape
    depth = norm_pre.shape[0]
    bf = lambda w: w.astype(jnp.bfloat16)
    xs = x.reshape(batch * seq, d)
    h = None
    for i in range(depth):
        m, j = i % 3, i // 3
        gamma_next = norm_pre[i + 1] if i + 1 < depth else None
        if m == 0:
            e = a_w_out.shape[1]
            w_in = bf(_scale_q_columns(a_w_in[j], e, e // A_HEADS))
            proj = _input_projection(xs, h, norm_pre[i], w_in)
            o = chunk_attention(proj, a_rel_bias[j], batch=batch, seq=seq)
            res = out_proj(o, proj, 3, bf(a_w_out[j]), xs, norm_post[i], gamma_next)
        elif m == 1:
            proj = _input_projection(xs, h, norm_pre[i], bf(b_w_in[j]), tn=b_w_in.shape[2] // 4)
            hs = rg_lru(proj, b_conv_w[j], b_conv_b[j], b_gate_a_w[j], b_gate_a_b[j],
                        b_gate_x_w[j], b_gate_x_b[j], b_lambda[j], batch=batch, seq=seq)
            res = out_proj(hs, proj, 1, bf(b_w_out[j]), xs, norm_post[i], gamma_next)
        else:
            e = c_w_out.shape[1]
            n_f = c_w_in.shape[2] - 4 * e
            w_main = bf(_scale_q_columns(c_w_in[j][:, :4 * e], e, e // C_HEADS))
            w_f = jnp.pad(bf(c_w_in[j][:, 4 * e:]), ((0, 0), (0, LANES - n_f)))
            f_bias = jnp.pad(c_f_bias[j].astype(jnp.float32), (0, LANES - n_f)).reshape(1, LANES)
            proj, f_logit = _input_projection(xs, h, norm_pre[i], w_main, w_f)
            qx, kx = forget_gate_slabs(f_logit, f_bias, batch=batch, seq=seq, heads=n_f)
            o = forgetting_attention(proj, qx, kx, batch=batch, seq=seq)
            res = out_proj(o, proj, 3, bf(c_w_out[j]), xs, norm_post[i], gamma_next)
        xs, h = res if gamma_next is not None else (res, None)
    return xs.reshape(batch, seq, d)
```

```python
import functools

import jax
import jax.numpy as jnp
import numpy as np
from jax import lax
from jax.experimental import pallas as pl
from jax.experimental.pallas import tpu as pltpu

RMS_EPS = 1e-6
NEG_INF = -1e30
F32_TINY = float(np.finfo(np.float32).tiny)
LOG2E = float(np.log2(np.e))
LANES = 128
MXU_WIDTH = 256

A_HEADS = 16
A_CHUNK = 64
A_LEFT_CHUNKS = 8
A_REL_CLIP = 256
A_HEADS_PER_STEP = 2
A_QTILE = 2 * A_CHUNK
A_LEFT = A_LEFT_CHUNKS * A_CHUNK
A_KTILE = A_LEFT + A_QTILE

RG_BLOCKS = 16
RG_GROUP_BLOCKS = 4
CONV_WIDTH = 4
RG_C = 8.0
LRU_TS = 32
LRU_GATE_ROWS = 64
LRU_SUB_STEPS = 16

C_HEADS = 16
FOX_T = 512
FOX_ROWS = 128
N_SPLIT = 3

HEADS_PER_STEP = 2
OUT_PROJ_ROWS = 256
VMEM_LIMIT_BYTES = 56 * 1024 * 1024


def _params(*sem):
    return pltpu.CompilerParams(dimension_semantics=sem, vmem_limit_bytes=VMEM_LIMIT_BYTES)


def _norm_proj_body(x_ref, gam_ref, w_ref, *rest, has_extra):
    if has_extra:
        w2_ref, o_ref, o2_ref, xn_ref = rest
    else:
        o_ref, xn_ref = rest

    @pl.when(pl.program_id(1) == 0)
    def _():
        x = x_ref[...]
        ms = jnp.mean(x * x, axis=-1, keepdims=True)
        xn_ref[...] = (x * lax.rsqrt(ms + RMS_EPS) * gam_ref[...]).astype(jnp.bfloat16)
        if has_extra:
            o2_ref[...] = jnp.dot(xn_ref[...], w2_ref[...], preferred_element_type=jnp.float32)

    res = jnp.dot(xn_ref[...], w_ref[...], preferred_element_type=jnp.float32)
    for c in range(o_ref.shape[0]):
        o_ref[c] = res[:, c * LANES:(c + 1) * LANES].astype(o_ref.dtype)


def norm_proj(x, gamma, w, w2=None, *, tm=1024, tn=1024):
    m, d = x.shape
    n = w.shape[1]
    tm, tn = min(tm, m), min(tn, n)
    assert m % tm == 0 and n % tn == 0 and tn % LANES == 0
    in_specs = [pl.BlockSpec((tm, d), lambda i, j: (i, 0)),
                pl.BlockSpec((1, d), lambda i, j: (0, 0)),
                pl.BlockSpec((d, tn), lambda i, j: (0, j))]
    out_shape = [jax.ShapeDtypeStruct((n // LANES, m, LANES), jnp.bfloat16)]
    out_specs = [pl.BlockSpec((tn // LANES, tm, LANES), lambda i, j: (j, i, 0))]
    args = [x, gamma.reshape(1, d), w]
    if w2 is not None:
        in_specs.append(pl.BlockSpec((d, w2.shape[1]), lambda i, j: (0, 0)))
        out_shape.append(jax.ShapeDtypeStruct((m, w2.shape[1]), jnp.float32))
        out_specs.append(pl.BlockSpec((tm, w2.shape[1]), lambda i, j: (i, 0)))
        args.append(w2)
    out = pl.pallas_call(
        functools.partial(_norm_proj_body, has_extra=w2 is not None),
        grid=(m // tm, n // tn),
        in_specs=in_specs, out_specs=out_specs, out_shape=out_shape,
        scratch_shapes=[pltpu.VMEM((tm, d), jnp.bfloat16)],
        compiler_params=_params("parallel", "arbitrary"),
        name="norm_proj",
    )(*args)
    return out if w2 is not None else out[0]


def _proj_body(h_ref, w_ref, *rest, has_extra):
    if has_extra:
        w2_ref, o_ref, o2_ref = rest

        @pl.when(pl.program_id(1) == 0)
        def _():
            o2_ref[...] = jnp.dot(h_ref[...], w2_ref[...], preferred_element_type=jnp.float32)
    else:
        (o_ref,) = rest

    res = jnp.dot(h_ref[...], w_ref[...], preferred_element_type=jnp.float32)
    for c in range(o_ref.shape[0]):
        o_ref[c] = res[:, c * LANES:(c + 1) * LANES].astype(o_ref.dtype)


def proj_slabs(h, w, w2=None, *, tm=1024, tn=2048):
    m, d = h.shape
    n = w.shape[1]
    tm, tn = min(tm, m), min(tn, n)
    assert m % tm == 0 and n % tn == 0 and tn % LANES == 0
    in_specs = [pl.BlockSpec((tm, d), lambda i, j: (i, 0)),
                pl.BlockSpec((d, tn), lambda i, j: (0, j))]
    out_shape = [jax.ShapeDtypeStruct((n // LANES, m, LANES), jnp.bfloat16)]
    out_specs = [pl.BlockSpec((tn // LANES, tm, LANES), lambda i, j: (j, i, 0))]
    args = [h, w]
    if w2 is not None:
        in_specs.append(pl.BlockSpec((d, w2.shape[1]), lambda i, j: (0, 0)))
        out_shape.append(jax.ShapeDtypeStruct((m, w2.shape[1]), jnp.float32))
        out_specs.append(pl.BlockSpec((tm, w2.shape[1]), lambda i, j: (i, 0)))
        args.append(w2)
    out = pl.pallas_call(
        functools.partial(_proj_body, has_extra=w2 is not None),
        grid=(m // tm, n // tn),
        in_specs=in_specs, out_specs=out_specs, out_shape=out_shape,
        compiler_params=_params("parallel", "arbitrary"),
        name="proj",
    )(*args)
    return out if w2 is not None else out[0]


def _out_proj_body(a_ref, g_ref, w_ref, x_ref, gam_ref, *rest, has_next):
    if has_next:
        gam_next_ref, o_ref, h_next_ref = rest
    else:
        (o_ref,) = rest
    ns, tm, _ = a_ref.shape
    n_chunks = tm // OUT_PROJ_ROWS

    def gated(c):
        rs = slice(c * OUT_PROJ_ROWS, (c + 1) * OUT_PROJ_ROWS)
        g = jnp.concatenate([g_ref[s, rs, :] for s in range(ns)], axis=1).astype(jnp.float32)
        a = jnp.concatenate([a_ref[s, rs, :] for s in range(ns)], axis=1).astype(jnp.float32)
        half_g = 0.5 * g
        return (a * (half_g * (1.0 + jnp.tanh(half_g)))).astype(jnp.bfloat16)

    def finish(c, y):
        rs = slice(c * OUT_PROJ_ROWS, (c + 1) * OUT_PROJ_ROWS)
        ms = jnp.mean(y * y, axis=-1, keepdims=True)
        x_new = x_ref[rs, :] + y * lax.rsqrt(ms + RMS_EPS) * gam_ref[...]
        o_ref[rs, :] = x_new
        if has_next:
            ms_new = jnp.mean(x_new * x_new, axis=-1, keepdims=True)
            h_next_ref[rs, :] = (x_new * lax.rsqrt(ms_new + RMS_EPS)
                                 * gam_next_ref[...]).astype(h_next_ref.dtype)

    a_next = gated(0)
    y_prev = None
    for c in range(n_chunks):
        a_cur = a_next
        if c + 1 < n_chunks:
            a_next = gated(c + 1)
        y = jnp.dot(a_cur, w_ref[...], preferred_element_type=jnp.float32)
        if y_prev is not None:
            finish(c - 1, y_prev)
        y_prev = y
    finish(n_chunks - 1, y_prev)


def out_proj(a, proj, g_block, w, x, gamma, gamma_next=None, *, tm=512):
    ns, m, _ = a.shape
    e, d = w.shape
    assert ns * LANES == e
    tm = min(tm, m)
    assert m % tm == 0 and tm % OUT_PROJ_ROWS == 0
    row_spec = pl.BlockSpec((tm, d), lambda i: (i, 0))
    vec_spec = pl.BlockSpec((1, d), lambda i: (0, 0))
    in_specs = [pl.BlockSpec((ns, tm, LANES), lambda i: (0, i, 0)),
                pl.BlockSpec((ns, tm, LANES), lambda i: (g_block, i, 0)),
                pl.BlockSpec((e, d), lambda i: (0, 0), pipeline_mode=pl.Buffered(1)),
                row_spec, vec_spec]
    out_shape = [jax.ShapeDtypeStruct((m, d), jnp.float32)]
    out_specs = [row_spec]
    args = [a, proj, w, x, gamma.reshape(1, d)]
    if gamma_next is not None:
        in_specs.append(vec_spec)
        out_shape.append(jax.ShapeDtypeStruct((m, d), jnp.bfloat16))
        out_specs.append(row_spec)
        args.append(gamma_next.reshape(1, d))
    out = pl.pallas_call(
        functools.partial(_out_proj_body, has_next=gamma_next is not None),
        grid=(m // tm,),
        in_specs=in_specs, out_specs=out_specs, out_shape=out_shape,
        compiler_params=_params("parallel"),
        name="out_proj",
    )(*args)
    return out if gamma_next is not None else out[0]


def _chunk_attn_body(q_ref, k_ref, v_ref, bias_ref, o_ref, s_ref):
    hp, seq, _ = q_ref.shape
    n_tiles = seq // A_QTILE
    n_cut = A_LEFT // A_QTILE

    def window(t):
        q0 = t * A_QTILE
        k0 = max(q0 - A_LEFT, 0)
        n_keys = q0 + A_QTILE - k0
        return slice(q0, q0 + A_QTILE), slice(k0, k0 + n_keys), n_keys

    def scores(t, slot):
        q_rows, k_rows, n_keys = window(t)
        for h in range(hp):
            s = lax.dot_general(q_ref[h, q_rows, :], k_ref[h, k_rows, :], (((1,), (1,)), ((), ())),
                                preferred_element_type=jnp.float32)
            s_ref[slot, h, :, :n_keys] = s + bias_ref[h, min(t, n_cut), :, :n_keys]

    def finish(t, slot):
        q_rows, k_rows, n_keys = window(t)
        ps, ls = [], []
        for h in range(hp):
            s = s_ref[slot, h, :, :n_keys]
            p = jnp.exp2(s - jnp.max(s, axis=-1, keepdims=True))
            ls.append(jnp.sum(p, axis=-1, keepdims=True))
            ps.append(p.astype(jnp.bfloat16))
        outs = [jnp.dot(ps[h], v_ref[h, k_rows, :], preferred_element_type=jnp.float32)
                for h in range(hp)]
        for h in range(hp):
            o_ref[h, q_rows, :] = (outs[h] / ls[h]).astype(o_ref.dtype)

    scores(0, 0)
    for t in range(n_tiles):
        if t + 1 < n_tiles:
            scores(t + 1, (t + 1) % 2)
        finish(t, t % 2)


def _bias_tables_body(u_ref, o_ref):
    n_tab, qt, kt = o_ref.shape[1:]
    qi = lax.broadcasted_iota(jnp.int32, (qt, kt), 0)
    j = lax.broadcasted_iota(jnp.int32, (qt, kt), 1)
    chunk_start = (qi // A_CHUNK) * A_CHUNK
    for i in range(n_tab):
        diag = jnp.broadcast_to(u_ref[0, i:i + 1, :], (qt, u_ref.shape[2]))
        table = pltpu.roll(diag, 0, 1, stride=1, stride_axis=0)[:, :kt]
        band_start = chunk_start + (i * qt - A_LEFT)
        visible = (j >= band_start) & (j < band_start + (A_LEFT_CHUNKS + 1) * A_CHUNK)
        o_ref[0, i] = jnp.where(visible, LOG2E * table, NEG_INF)


def _band_bias_tables(rel_bias):
    heads = rel_bias.shape[0]
    n_tab = A_LEFT // A_QTILE + 1
    n_diag = pl.cdiv(A_QTILE + A_KTILE - 1, LANES) * LANES
    m = np.arange(n_diag)
    key_minus_query = np.where(m < A_KTILE, m, m - n_diag)
    offsets = A_QTILE * np.arange(n_tab)[:, None]
    idx = np.clip(offsets - key_minus_query[None, :], -A_REL_CLIP, A_REL_CLIP) + A_REL_CLIP
    diags = rel_bias.astype(jnp.float32)[:, idx]
    return pl.pallas_call(
        _bias_tables_body,
        grid=(heads,),
        in_specs=[pl.BlockSpec((1, n_tab, n_diag), lambda h: (h, 0, 0))],
        out_specs=pl.BlockSpec((1, n_tab, A_QTILE, A_KTILE), lambda h: (h, 0, 0, 0)),
        out_shape=jax.ShapeDtypeStruct((heads, n_tab, A_QTILE, A_KTILE), jnp.float32),
        compiler_params=_params("parallel"),
        name="bias_tables",
    )(diags)


def chunk_attention(proj, rel_bias, *, batch, seq):
    heads = A_HEADS
    hp = A_HEADS_PER_STEP
    assert proj.shape[0] == 4 * heads and seq % (2 * A_QTILE) == 0 and seq >= A_KTILE
    tables = _band_bias_tables(rel_bias)
    blk = (hp, seq, LANES)
    nhb = heads // hp
    return pl.pallas_call(
        _chunk_attn_body,
        grid=(nhb, batch),
        in_specs=[pl.BlockSpec(blk, lambda h, b: (h, b, 0)),
                  pl.BlockSpec(blk, lambda h, b: (nhb + h, b, 0)),
                  pl.BlockSpec(blk, lambda h, b: (2 * nhb + h, b, 0)),
                  pl.BlockSpec((hp,) + tables.shape[1:], lambda h, b: (h, 0, 0, 0))],
        out_specs=pl.BlockSpec(blk, lambda h, b: (h, b, 0)),
        out_shape=jax.ShapeDtypeStruct((heads, batch * seq, LANES), jnp.bfloat16),
        scratch_shapes=[pltpu.VMEM((2, hp, A_QTILE, A_KTILE), jnp.float32)],
        compiler_params=_params("parallel", "parallel"),
        name="chunk_attn",
    )(proj, proj, proj, tables)


def _lru_body(xr_ref, cw_ref, cb_ref, wg_ref, ba_ref, bx_ref, lam_ref, o_ref,
              xbuf_ref, y_ref, pre_ref, a_ref, u_ref, hs_ref, h_ref, *, groups, gw, pieces):
    ns, nb, ts, _ = o_ref.shape
    rows = ts * nb
    hist = CONV_WIDTH * nb
    sub = LRU_SUB_STEPS
    pr = sub * nb

    @pl.when(pl.program_id(0) == 0)
    def _():
        xbuf_ref[0:hist, :] = jnp.zeros((hist, xbuf_ref.shape[1]), jnp.float32)
        h_ref[...] = jnp.zeros(h_ref.shape, jnp.float32)

    r_idx = lax.broadcasted_iota(jnp.int32, (pr, pr), 0)
    k_idx = lax.broadcasted_iota(jnp.int32, (pr, pr), 1)
    perm = jnp.where(k_idx == (r_idx % nb) * sub + r_idx // nb, 1.0, 0.0).astype(jnp.bfloat16)

    for part in range(ts // sub):
        steps = slice(part * sub, (part + 1) * sub)
        x_bm = jnp.concatenate(
            [jnp.concatenate([xr_ref[c, b, steps, :] for b in range(nb)], axis=0) for c in range(ns)],
            axis=1)
        xbuf_ref[hist + part * pr:hist + (part + 1) * pr, :] = jnp.dot(
            perm, x_bm, preferred_element_type=jnp.float32)

    def conv_and_gate_matmul(gi, slot):
        cs = slice(gi * gw, (gi + 1) * gw)
        y = cb_ref[:, cs] + cw_ref[CONV_WIDTH - 1:CONV_WIDTH, cs] * xbuf_ref[hist:hist + rows, cs]
        for tap in range(CONV_WIDTH - 1):
            off = hist - (CONV_WIDTH - 1 - tap) * nb
            y = y + cw_ref[tap:tap + 1, cs] * xbuf_ref[off:off + rows, cs]
        y_ref[slot] = y
        y_bf = y.astype(jnp.bfloat16)
        for p, (c0, c1, k0, k1) in enumerate(pieces):
            w = wg_ref[gi, p, :k1 - k0, :2 * (c1 - c0)]
            pre = jnp.dot(y_bf[:, k0:k1], w, preferred_element_type=jnp.float32)
            pre_ref[slot, :, c0:c1] = pre[:, :c1 - c0]
            pre_ref[slot, :, gw + c0:gw + c1] = pre[:, c1 - c0:]

    def gates(gi, slot):
        cs = slice(gi * gw, (gi + 1) * gw)
        half_ba = 0.5 * ba_ref[:, cs]
        half_bx = 0.5 * bx_ref[:, cs]
        lam = lam_ref[:, cs]
        softplus_neg_lam = jnp.maximum(-lam, 0.0) + jnp.log1p(jnp.exp(-jnp.abs(lam)))
        half_coef = (-0.5 * RG_C) * softplus_neg_lam
        for k in range(rows // LRU_GATE_ROWS):
            rk = slice(k * LRU_GATE_ROWS, (k + 1) * LRU_GATE_ROWS)
            tanh_r = jnp.tanh(pre_ref[slot, rk, :gw] + half_ba)
            tanh_i = jnp.tanh(pre_ref[slot, rk, gw:] + half_bx)
            log_a = half_coef * tanh_r + half_coef
            a = jnp.exp(log_a)
            quarter = (-0.25 * jnp.tanh(log_a)) * (1.0 + a * a)
            half_mult = quarter * lax.rsqrt(jnp.maximum(quarter, F32_TINY))
            a_ref[rk, cs] = a
            u_ref[rk, cs] = half_mult * ((tanh_i + 1.0) * y_ref[slot, rk, :])

    conv_and_gate_matmul(0, 0)
    for gi in range(groups):
        if gi + 1 < groups:
            conv_and_gate_matmul(gi + 1, (gi + 1) % 2)
        gates(gi, gi % 2)

    xbuf_ref[0:hist, :] = xbuf_ref[rows:rows + hist, :]

    def time_step(t, h):
        rs = pl.ds(pl.multiple_of(t * nb, nb), nb)
        h = a_ref[rs, :] * h + u_ref[rs, :]
        hs_ref[rs, :] = h.astype(hs_ref.dtype)
        return h

    h_ref[...] = lax.fori_loop(0, ts, time_step, h_ref[...])

    for part in range(ts // sub):
        steps = slice(part * sub, (part + 1) * sub)
        hs_bm = jnp.dot(perm, hs_ref[part * pr:(part + 1) * pr, :],
                        preferred_element_type=jnp.float32).astype(o_ref.dtype)
        for c in range(ns):
            for b in range(nb):
                o_ref[c, b, steps, :] = hs_bm[b * sub:(b + 1) * sub, c * LANES:(c + 1) * LANES]


def _gate_pieces(gw, bs):
    pieces = []
    for c0 in range(0, gw, MXU_WIDTH):
        c1 = min(c0 + MXU_WIDTH, gw)
        k0 = (c0 // bs) * bs // LANES * LANES
        k1 = min(pl.cdiv(((c1 - 1) // bs + 1) * bs, LANES) * LANES, gw)
        pieces.append((c0, c1, k0, k1))
    return pieces


def _gate_group_weights(w_a, w_x):
    nb, bs, _ = w_a.shape
    groups = nb // RG_GROUP_BLOCKS
    gw = RG_GROUP_BLOCKS * bs
    pieces = _gate_pieces(gw, bs)
    k_max = max(k1 - k0 for _, _, k0, k1 in pieces)
    n_max = max(2 * (c1 - c0) for c0, c1, _, _ in pieces)

    def block_diag(w):
        w = (0.5 * w).astype(jnp.bfloat16).reshape(groups, RG_GROUP_BLOCKS, bs, bs)
        rows = [jnp.pad(w[:, n], ((0, 0), (0, 0), (n * bs, gw - (n + 1) * bs)))
                for n in range(RG_GROUP_BLOCKS)]
        return jnp.concatenate(rows, axis=1)

    full_a, full_x = block_diag(w_a), block_diag(w_x)
    out = []
    for c0, c1, k0, k1 in pieces:
        piece = jnp.concatenate([full_a[:, k0:k1, c0:c1], full_x[:, k0:k1, c0:c1]], axis=2)
        out.append(jnp.pad(piece, ((0, 0), (0, k_max - (k1 - k0)), (0, n_max - 2 * (c1 - c0)))))
    return jnp.stack(out, axis=1)


def rg_lru(proj, conv_w, conv_b, w_a, b_a, w_x, b_x, lam, *, batch, seq):
    ns = proj.shape[0] // 2
    width = ns * LANES
    gw = RG_GROUP_BLOCKS * (width // RG_BLOCKS)
    groups = width // gw
    ts = min(LRU_TS, seq)
    rows = ts * batch
    assert seq % ts == 0 and gw % LANES == 0 and ts >= CONV_WIDTH and ts % LRU_SUB_STEPS == 0
    assert batch == LRU_SUB_STEPS and rows % LRU_GATE_ROWS == 0
    wg = _gate_group_weights(w_a, w_x)
    vec = lambda v: v.reshape(1, width).astype(jnp.float32)
    const2 = lambda s: (0, 0)
    blk = pl.BlockSpec((ns, batch, ts, LANES), lambda s: (0, 0, s, 0))
    out = pl.pallas_call(
        functools.partial(_lru_body, groups=groups, gw=gw, pieces=_gate_pieces(gw, width // RG_BLOCKS)),
        grid=(seq // ts,),
        in_specs=[blk,
                  pl.BlockSpec((CONV_WIDTH, width), const2),
                  pl.BlockSpec((1, width), const2),
                  pl.BlockSpec(wg.shape, lambda s: (0, 0, 0, 0), pipeline_mode=pl.Buffered(1)),
                  pl.BlockSpec((1, width), const2),
                  pl.BlockSpec((1, width), const2),
                  pl.BlockSpec((1, width), const2)],
        out_specs=blk,
        out_shape=jax.ShapeDtypeStruct((ns, batch, seq, LANES), jnp.bfloat16),
        scratch_shapes=[pltpu.VMEM((rows + CONV_WIDTH * batch, width), jnp.float32),
                        pltpu.VMEM((2, rows, gw), jnp.float32),
                        pltpu.VMEM((2, rows, 2 * gw), jnp.float32),
                        pltpu.VMEM((rows, width), jnp.float32),
                        pltpu.VMEM((rows, width), jnp.float32),
                        pltpu.VMEM((rows, width), jnp.bfloat16),
                        pltpu.VMEM((batch, width), jnp.float32)],
        compiler_params=_params("arbitrary"),
        name="rg_lru",
    )(proj.reshape(2 * ns, batch, seq, LANES), conv_w.astype(jnp.float32), vec(conv_b), wg,
      vec(b_a), vec(b_x), vec(lam))
    return out.reshape(ns, batch * seq, LANES)


FOX_PIECE_LANES = 16 * N_SPLIT


def _split3(x):
    hi = x.astype(jnp.bfloat16).astype(jnp.float32)
    r1 = x - hi
    mid = r1.astype(jnp.bfloat16).astype(jnp.float32)
    lo = (r1 - mid).astype(jnp.bfloat16).astype(jnp.float32)
    return hi, mid, lo


def _forget_gate_body(f_ref, fb_ref, qx_ref, kx_ref, *, heads):
    z = f_ref[0] + fb_ref[...]
    c = jnp.minimum(z, 0.0) - jnp.log1p(jnp.exp(-jnp.abs(z)))
    n = c.shape[0]
    row = lax.broadcasted_iota(jnp.int32, c.shape, 0)
    d = 1
    while d < n:
        c = c + jnp.where(row >= d, pltpu.roll(c, d, 0), 0.0)
        d *= 2
    lane = lax.broadcasted_iota(jnp.int32, c.shape, 1)
    hi, mid, lo = _split3(LOG2E * c)
    pieces = jnp.where(lane < heads, hi,
                       jnp.where(lane < 2 * heads, pltpu.roll(mid, heads, 1),
                                 jnp.where(lane < 3 * heads, pltpu.roll(lo, 2 * heads, 1), 0.0)))
    qx_ref[...] = pieces.astype(qx_ref.dtype)
    kx_ref[...] = pltpu.roll(-pieces, FOX_PIECE_LANES, 1).astype(kx_ref.dtype)


def forget_gate_slabs(f_logit, f_bias_padded, *, batch, seq, heads):
    assert heads * N_SPLIT == FOX_PIECE_LANES and 2 * FOX_PIECE_LANES <= LANES
    f3 = f_logit.reshape(batch, seq, LANES)
    shape = jax.ShapeDtypeStruct((batch * seq, LANES), jnp.bfloat16)
    spec = pl.BlockSpec((seq, LANES), lambda b: (b, 0))
    return pl.pallas_call(
        functools.partial(_forget_gate_body, heads=heads),
        grid=(batch,),
        in_specs=[pl.BlockSpec((1, seq, LANES), lambda b: (b, 0, 0)),
                  pl.BlockSpec((1, LANES), lambda b: (0, 0))],
        out_specs=[spec, spec],
        out_shape=[shape, shape],
        compiler_params=_params("parallel"),
        name="forget_gate",
    )(f3, f_bias_padded)


def _fox_body(q_ref, qx_ref, k_ref, kx_ref, v_ref, o_ref, s_ref, m_ref, l_ref, acc_ref):
    hp, seq, dh = q_ref.shape
    t = FOX_T
    heads = FOX_PIECE_LANES // N_SPLIT
    steps = [(i, j) for i in range(seq // t) for j in range(i + 1)]
    lane = lax.broadcasted_iota(jnp.int32, (1, LANES), 1)

    def augment(x_ref, px_ref, h, r, first_lane):
        rel = lane - (first_lane + pl.program_id(1) * hp + h)
        select = (rel == 0) | (rel == heads) | (rel == 2 * heads)
        extra = jnp.where(select, jnp.ones((), px_ref.dtype), px_ref[r, :])
        return jnp.concatenate([x_ref[h, r, :], extra], axis=1)

    def parts(i, j):
        if i != j:
            return [(0, t, t)]
        return [(0, t // 2, t // 2), (t // 2, t, t)]

    def scores(n, slot):
        i, j = steps[n]
        for h in range(hp):
            for r0, r1, n_keys in parts(i, j):
                q_aug = augment(q_ref, qx_ref, h, slice(i * t + r0, i * t + r1), FOX_PIECE_LANES)
                k_aug = augment(k_ref, kx_ref, h, slice(j * t, j * t + n_keys), 0)
                s = lax.dot_general(q_aug, k_aug, (((1,), (1,)), ((), ())),
                                    preferred_element_type=jnp.float32)
                if i == j:
                    r = lax.broadcasted_iota(jnp.int32, s.shape, 0) + r0
                    c = lax.broadcasted_iota(jnp.int32, s.shape, 1)
                    s = jnp.where(c <= r, s, NEG_INF)
                s_ref[slot, h, r0:r1, :n_keys] = s

    def finish(n, slot):
        i, j = steps[n]
        work = [(h, rc, rc + FOX_ROWS, n_keys) for h in range(hp) for r0, r1, n_keys in parts(i, j)
                for rc in range(r0, r1, FOX_ROWS)]
        for h, r0, r1, n_keys in work:
            s = s_ref[slot, h, r0:r1, :n_keys]
            m_cur = jnp.max(s, axis=1, keepdims=True)
            if j == 0:
                m_next = jnp.broadcast_to(m_cur, (r1 - r0, LANES))
            else:
                m_prev = m_ref[h, r0:r1, :]
                m_next = jnp.maximum(m_prev, m_cur)
                alpha = jnp.exp2(m_prev - m_next)
            m_ref[h, r0:r1, :] = m_next
            p = jnp.exp2(s - jnp.tile(m_next, (1, n_keys // LANES)))
            l_cur = jnp.sum(p, axis=1, keepdims=True)
            pv = jnp.dot(p.astype(jnp.bfloat16), v_ref[h, j * t:j * t + n_keys, :],
                         preferred_element_type=jnp.float32)
            if j == 0:
                l_new = jnp.broadcast_to(l_cur, (r1 - r0, LANES))
                acc = pv
            else:
                l_new = l_cur + alpha * l_ref[h, r0:r1, :]
                acc = alpha * acc_ref[h, r0:r1, :] + pv
            if j == i:
                o_ref[h, i * t + r0:i * t + r1, :] = (acc / l_new).astype(o_ref.dtype)
            else:
                l_ref[h, r0:r1, :] = l_new
                acc_ref[h, r0:r1, :] = acc

    scores(0, 0)
    for n in range(len(steps)):
        if n + 1 < len(steps):
            scores(n + 1, (n + 1) % 2)
        finish(n, n % 2)


def forgetting_attention(proj, qx, kx, *, batch, seq):
    heads = C_HEADS
    hp = HEADS_PER_STEP
    assert proj.shape[0] == 4 * heads and seq % FOX_T == 0
    blk = (hp, seq, LANES)
    nhb = heads // hp
    return pl.pallas_call(
        _fox_body,
        grid=(batch, nhb),
        in_specs=[pl.BlockSpec(blk, lambda b, h: (h, b, 0)),
                  pl.BlockSpec((seq, LANES), lambda b, h: (b, 0)),
                  pl.BlockSpec(blk, lambda b, h: (nhb + h, b, 0)),
                  pl.BlockSpec((seq, LANES), lambda b, h: (b, 0)),
                  pl.BlockSpec(blk, lambda b, h: (2 * nhb + h, b, 0))],
        out_specs=pl.BlockSpec(blk, lambda b, h: (h, b, 0)),
        out_shape=jax.ShapeDtypeStruct((heads, batch * seq, LANES), jnp.bfloat16),
        scratch_shapes=[pltpu.VMEM((2, hp, FOX_T, FOX_T), jnp.float32),
                        pltpu.VMEM((hp, FOX_T, LANES), jnp.float32),
                        pltpu.VMEM((hp, FOX_T, LANES), jnp.float32),
                        pltpu.VMEM((hp, FOX_T, LANES), jnp.float32)],
        compiler_params=_params("parallel", "parallel"),
        name="fox_attn",
    )(proj, qx, proj, kx, proj)


def _q_column_scale(n, width, dh):
    return jnp.where(jnp.arange(n) < width, LOG2E * dh ** -0.5, 1.0).astype(jnp.float32)


def _split_cast_body(w_ref, scale_ref, main_ref, tail_ref):
    n_main = main_ref.shape[1]
    n_tail = w_ref.shape[2] - n_main
    main_ref[...] = (w_ref[0, :, :n_main] * scale_ref[...]).astype(main_ref.dtype)
    tail_ref[...] = jnp.zeros(tail_ref.shape, tail_ref.dtype)
    tail_ref[:, :n_tail] = w_ref[0, :, n_main:].astype(tail_ref.dtype)


def split_cast_weight(w_stack, j, n_main, col_scale, *, tk=256):
    _, d, n = w_stack.shape
    assert d % tk == 0 and n_main % LANES == 0 and 0 < n - n_main <= LANES
    return pl.pallas_call(
        _split_cast_body,
        grid=(d // tk,),
        in_specs=[pl.BlockSpec((1, tk, n), lambda i: (j, i, 0)),
                  pl.BlockSpec((1, n_main), lambda i: (0, 0))],
        out_specs=[pl.BlockSpec((tk, n_main), lambda i: (i, 0)),
                   pl.BlockSpec((tk, LANES), lambda i: (i, 0))],
        out_shape=[jax.ShapeDtypeStruct((d, n_main), jnp.bfloat16),
                   jax.ShapeDtypeStruct((d, LANES), jnp.bfloat16)],
        compiler_params=_params("parallel"),
        name="split_cast",
    )(w_stack, col_scale.reshape(1, n_main))


def _input_projection(xs, h, gamma, w, w2=None, *, tn=2048):
    if h is None:
        return norm_proj(xs, gamma, w, w2)
    return proj_slabs(h, w, w2, tn=tn)


def kernel(x, norm_pre, norm_post, a_w_in, a_rel_bias, a_w_out, b_w_in, b_conv_w, b_conv_b,
           b_gate_a_w, b_gate_a_b, b_gate_x_w, b_gate_x_b, b_lambda, b_w_out,
           c_w_in, c_f_bias, c_w_out):
    batch, seq, d = x.shape
    depth = norm_pre.shape[0]
    bf = lambda w: w.astype(jnp.bfloat16)
    xs = x.reshape(batch * seq, d)
    h = None
    for i in range(depth):
        m, j = i % 3, i // 3
        gamma_next = norm_pre[i + 1] if i + 1 < depth else None
        if m == 0:
            e = a_w_out.shape[1]
            w_in = bf(a_w_in[j] * _q_column_scale(4 * e, e, e // A_HEADS))
            proj = _input_projection(xs, h, norm_pre[i], w_in)
            o = chunk_attention(proj, a_rel_bias[j], batch=batch, seq=seq)
            res = out_proj(o, proj, 3, bf(a_w_out[j]), xs, norm_post[i], gamma_next)
        elif m == 1:
            proj = _input_projection(xs, h, norm_pre[i], bf(b_w_in[j]), tn=b_w_in.shape[2] // 4)
            hs = rg_lru(proj, b_conv_w[j], b_conv_b[j], b_gate_a_w[j], b_gate_a_b[j],
                        b_gate_x_w[j], b_gate_x_b[j], b_lambda[j], batch=batch, seq=seq)
            res = out_proj(hs, proj, 1, bf(b_w_out[j]), xs, norm_post[i], gamma_next)
        else:
            e = c_w_out.shape[1]
            n_f = c_w_in.shape[2] - 4 * e
            w_main, w_f = split_cast_weight(c_w_in, j, 4 * e, _q_column_scale(4 * e, e, e // C_HEADS))
            f_bias = jnp.pad(c_f_bias[j].astype(jnp.float32), (0, LANES - n_f)).reshape(1, LANES)
            proj, f_logit = _input_projection(xs, h, norm_pre[i], w_main, w_f)
            qx, kx = forget_gate_slabs(f_logit, f_bias, batch=batch, seq=seq, heads=n_f)
            o = forgetting_attention(proj, qx, kx, batch=batch, seq=seq)
            res = out_proj(o, proj, 3, bf(c_w_out[j]), xs, norm_post[i], gamma_next)
        xs, h = res if gamma_next is not None else (res, None)
    return xs.reshape(batch, seq, d)
```

```python
import functools

import jax
import jax.numpy as jnp
import numpy as np
from jax import lax
from jax.experimental import pallas as pl
from jax.experimental.pallas import tpu as pltpu

RMS_EPS = 1e-6
NEG_INF = -1e30
F32_TINY = float(np.finfo(np.float32).tiny)
LOG2E = float(np.log2(np.e))
LANES = 128
MXU_WIDTH = 256

A_HEADS = 16
A_CHUNK = 64
A_LEFT_CHUNKS = 8
A_REL_CLIP = 256
A_HEADS_PER_STEP = 4
A_QTILE = 2 * A_CHUNK
A_LEFT = A_LEFT_CHUNKS * A_CHUNK
A_KTILE = A_LEFT + A_QTILE

RG_BLOCKS = 16
RG_GROUP_BLOCKS = 4
CONV_WIDTH = 4
RG_C = 8.0
LRU_TS = 32
LRU_GATE_ROWS = 64
LRU_SUB_STEPS = 16

C_HEADS = 16
FOX_T = 512
FOX_ROWS = 128
N_SPLIT = 3

HEADS_PER_STEP = 4
OUT_PROJ_ROWS = 256
VMEM_LIMIT_BYTES = 56 * 1024 * 1024


def _params(*sem):
    return pltpu.CompilerParams(dimension_semantics=sem, vmem_limit_bytes=VMEM_LIMIT_BYTES)


def _norm_proj_body(x_ref, gam_ref, w_ref, *rest, has_extra):
    if has_extra:
        w2_ref, o_ref, o2_ref, xn_ref = rest
    else:
        o_ref, xn_ref = rest

    @pl.when(pl.program_id(1) == 0)
    def _():
        x = x_ref[...]
        ms = jnp.mean(x * x, axis=-1, keepdims=True)
        xn_ref[...] = (x * lax.rsqrt(ms + RMS_EPS) * gam_ref[...]).astype(jnp.bfloat16)
        if has_extra:
            o2_ref[...] = jnp.dot(xn_ref[...], w2_ref[...], preferred_element_type=jnp.float32)

    res = jnp.dot(xn_ref[...], w_ref[...], preferred_element_type=jnp.float32)
    for c in range(o_ref.shape[0]):
        o_ref[c] = res[:, c * LANES:(c + 1) * LANES].astype(o_ref.dtype)


def norm_proj(x, gamma, w, w2=None, *, tm=1024, tn=1024):
    m, d = x.shape
    n = w.shape[1]
    tm, tn = min(tm, m), min(tn, n)
    assert m % tm == 0 and n % tn == 0 and tn % LANES == 0
    in_specs = [pl.BlockSpec((tm, d), lambda i, j: (i, 0)),
                pl.BlockSpec((1, d), lambda i, j: (0, 0)),
                pl.BlockSpec((d, tn), lambda i, j: (0, j))]
    out_shape = [jax.ShapeDtypeStruct((n // LANES, m, LANES), jnp.bfloat16)]
    out_specs = [pl.BlockSpec((tn // LANES, tm, LANES), lambda i, j: (j, i, 0))]
    args = [x, gamma.reshape(1, d), w]
    if w2 is not None:
        in_specs.append(pl.BlockSpec((d, w2.shape[1]), lambda i, j: (0, 0)))
        out_shape.append(jax.ShapeDtypeStruct((m, w2.shape[1]), jnp.float32))
        out_specs.append(pl.BlockSpec((tm, w2.shape[1]), lambda i, j: (i, 0)))
        args.append(w2)
    out = pl.pallas_call(
        functools.partial(_norm_proj_body, has_extra=w2 is not None),
        grid=(m // tm, n // tn),
        in_specs=in_specs, out_specs=out_specs, out_shape=out_shape,
        scratch_shapes=[pltpu.VMEM((tm, d), jnp.bfloat16)],
        compiler_params=_params("parallel", "arbitrary"),
        name="norm_proj",
    )(*args)
    return out if w2 is not None else out[0]


def _proj_body(h_ref, w_ref, *rest, has_extra):
    if has_extra:
        w2_ref, o_ref, o2_ref = rest

        @pl.when(pl.program_id(1) == 0)
        def _():
            o2_ref[...] = jnp.dot(h_ref[...], w2_ref[...], preferred_element_type=jnp.float32)
    else:
        (o_ref,) = rest

    res = jnp.dot(h_ref[...], w_ref[...], preferred_element_type=jnp.float32)
    for c in range(o_ref.shape[0]):
        o_ref[c] = res[:, c * LANES:(c + 1) * LANES].astype(o_ref.dtype)


def proj_slabs(h, w, w2=None, *, tm=1024, tn=2048):
    m, d = h.shape
    n = w.shape[1]
    tm, tn = min(tm, m), min(tn, n)
    assert m % tm == 0 and n % tn == 0 and tn % LANES == 0
    in_specs = [pl.BlockSpec((tm, d), lambda i, j: (i, 0)),
                pl.BlockSpec((d, tn), lambda i, j: (0, j))]
    out_shape = [jax.ShapeDtypeStruct((n // LANES, m, LANES), jnp.bfloat16)]
    out_specs = [pl.BlockSpec((tn // LANES, tm, LANES), lambda i, j: (j, i, 0))]
    args = [h, w]
    if w2 is not None:
        in_specs.append(pl.BlockSpec((d, w2.shape[1]), lambda i, j: (0, 0)))
        out_shape.append(jax.ShapeDtypeStruct((m, w2.shape[1]), jnp.float32))
        out_specs.append(pl.BlockSpec((tm, w2.shape[1]), lambda i, j: (i, 0)))
        args.append(w2)
    out = pl.pallas_call(
        functools.partial(_proj_body, has_extra=w2 is not None),
        grid=(m // tm, n // tn),
        in_specs=in_specs, out_specs=out_specs, out_shape=out_shape,
        compiler_params=_params("parallel", "arbitrary"),
        name="proj",
    )(*args)
    return out if w2 is not None else out[0]


def _out_proj_body(a_ref, g_ref, w_ref, x_ref, gam_ref, *rest, has_next):
    if has_next:
        gam_next_ref, o_ref, h_next_ref = rest
    else:
        (o_ref,) = rest
    ns, tm, _ = a_ref.shape
    n_chunks = tm // OUT_PROJ_ROWS

    def gated(c):
        rs = slice(c * OUT_PROJ_ROWS, (c + 1) * OUT_PROJ_ROWS)
        g = jnp.concatenate([g_ref[s, rs, :] for s in range(ns)], axis=1).astype(jnp.float32)
        a = jnp.concatenate([a_ref[s, rs, :] for s in range(ns)], axis=1).astype(jnp.float32)
        half_g = 0.5 * g
        return (a * (half_g * (1.0 + jnp.tanh(half_g)))).astype(jnp.bfloat16)

    def finish(c, y):
        rs = slice(c * OUT_PROJ_ROWS, (c + 1) * OUT_PROJ_ROWS)
        ms = jnp.mean(y * y, axis=-1, keepdims=True)
        x_new = x_ref[rs, :] + y * lax.rsqrt(ms + RMS_EPS) * gam_ref[...]
        o_ref[rs, :] = x_new
        if has_next:
            ms_new = jnp.mean(x_new * x_new, axis=-1, keepdims=True)
            h_next_ref[rs, :] = (x_new * lax.rsqrt(ms_new + RMS_EPS)
                                 * gam_next_ref[...]).astype(h_next_ref.dtype)

    a_next = gated(0)
    y_prev = None
    for c in range(n_chunks):
        a_cur = a_next
        if c + 1 < n_chunks:
            a_next = gated(c + 1)
        y = jnp.dot(a_cur, w_ref[...], preferred_element_type=jnp.float32)
        if y_prev is not None:
            finish(c - 1, y_prev)
        y_prev = y
    finish(n_chunks - 1, y_prev)


def out_proj(a, proj, g_block, w, x, gamma, gamma_next=None, *, tm=512):
    ns, m, _ = a.shape
    e, d = w.shape
    assert ns * LANES == e
    tm = min(tm, m)
    assert m % tm == 0 and tm % OUT_PROJ_ROWS == 0
    row_spec = pl.BlockSpec((tm, d), lambda i: (i, 0))
    vec_spec = pl.BlockSpec((1, d), lambda i: (0, 0))
    in_specs = [pl.BlockSpec((ns, tm, LANES), lambda i: (0, i, 0)),
                pl.BlockSpec((ns, tm, LANES), lambda i: (g_block, i, 0)),
                pl.BlockSpec((e, d), lambda i: (0, 0), pipeline_mode=pl.Buffered(1)),
                row_spec, vec_spec]
    out_shape = [jax.ShapeDtypeStruct((m, d), jnp.float32)]
    out_specs = [row_spec]
    args = [a, proj, w, x, gamma.reshape(1, d)]
    if gamma_next is not None:
        in_specs.append(vec_spec)
        out_shape.append(jax.ShapeDtypeStruct((m, d), jnp.bfloat16))
        out_specs.append(row_spec)
        args.append(gamma_next.reshape(1, d))
    out = pl.pallas_call(
        functools.partial(_out_proj_body, has_next=gamma_next is not None),
        grid=(m // tm,),
        in_specs=in_specs, out_specs=out_specs, out_shape=out_shape,
        compiler_params=_params("parallel"),
        name="out_proj",
    )(*args)
    return out if gamma_next is not None else out[0]


def _chunk_attn_body(q_ref, k_ref, v_ref, bias_ref, o_ref, s_ref, p_ref, l_ref):
    hp, seq, _ = q_ref.shape
    n_tiles = seq // A_QTILE
    n_cut = A_LEFT // A_QTILE

    def window(t):
        q0 = t * A_QTILE
        k0 = max(q0 - A_LEFT, 0)
        n_keys = q0 + A_QTILE - k0
        return slice(q0, q0 + A_QTILE), slice(k0, k0 + n_keys), n_keys

    def scores(t, slot):
        q_rows, k_rows, n_keys = window(t)
        for h in range(hp):
            s = lax.dot_general(q_ref[h, q_rows, :], k_ref[h, k_rows, :], (((1,), (1,)), ((), ())),
                                preferred_element_type=jnp.float32)
            s_ref[slot, h, :, :n_keys] = s + bias_ref[h, min(t, n_cut), :, :n_keys]

    def softmax(t, slot):
        _, _, n_keys = window(t)
        for h in range(hp):
            s = s_ref[slot, h, :, :n_keys]
            p = jnp.exp2(s - jnp.max(s, axis=-1, keepdims=True))
            l_ref[slot, h] = jnp.broadcast_to(jnp.sum(p, axis=-1, keepdims=True), (A_QTILE, LANES))
            p_ref[slot, h, :, :n_keys] = p.astype(p_ref.dtype)

    def weighted_values(t, slot):
        q_rows, k_rows, n_keys = window(t)
        for h in range(hp):
            o = jnp.dot(p_ref[slot, h, :, :n_keys], v_ref[h, k_rows, :],
                        preferred_element_type=jnp.float32)
            o_ref[h, q_rows, :] = (o / l_ref[slot, h]).astype(o_ref.dtype)

    scores(0, 0)
    scores(1, 1)
    softmax(0, 0)
    for t in range(n_tiles):
        if t + 2 < n_tiles:
            scores(t + 2, t % 2)
        if t + 1 < n_tiles:
            softmax(t + 1, (t + 1) % 2)
        weighted_values(t, t % 2)


def _bias_tables_body(u_ref, o_ref):
    n_tab, qt, kt = o_ref.shape[1:]
    qi = lax.broadcasted_iota(jnp.int32, (qt, kt), 0)
    j = lax.broadcasted_iota(jnp.int32, (qt, kt), 1)
    chunk_start = (qi // A_CHUNK) * A_CHUNK
    for i in range(n_tab):
        diag = jnp.broadcast_to(u_ref[0, i:i + 1, :], (qt, u_ref.shape[2]))
        table = pltpu.roll(diag, 0, 1, stride=1, stride_axis=0)[:, :kt]
        band_start = chunk_start + (i * qt - A_LEFT)
        visible = (j >= band_start) & (j < band_start + (A_LEFT_CHUNKS + 1) * A_CHUNK)
        o_ref[0, i] = jnp.where(visible, LOG2E * table, NEG_INF)


def _band_bias_tables(rel_bias):
    heads = rel_bias.shape[0]
    n_tab = A_LEFT // A_QTILE + 1
    n_diag = pl.cdiv(A_QTILE + A_KTILE - 1, LANES) * LANES
    m = np.arange(n_diag)
    key_minus_query = np.where(m < A_KTILE, m, m - n_diag)
    offsets = A_QTILE * np.arange(n_tab)[:, None]
    idx = np.clip(offsets - key_minus_query[None, :], -A_REL_CLIP, A_REL_CLIP) + A_REL_CLIP
    diags = rel_bias.astype(jnp.float32)[:, idx]
    return pl.pallas_call(
        _bias_tables_body,
        grid=(heads,),
        in_specs=[pl.BlockSpec((1, n_tab, n_diag), lambda h: (h, 0, 0))],
        out_specs=pl.BlockSpec((1, n_tab, A_QTILE, A_KTILE), lambda h: (h, 0, 0, 0)),
        out_shape=jax.ShapeDtypeStruct((heads, n_tab, A_QTILE, A_KTILE), jnp.float32),
        compiler_params=_params("parallel"),
        name="bias_tables",
    )(diags)


def chunk_attention(proj, rel_bias, *, batch, seq):
    heads = A_HEADS
    hp = A_HEADS_PER_STEP
    assert proj.shape[0] == 4 * heads and seq % (2 * A_QTILE) == 0 and seq >= A_KTILE
    tables = _band_bias_tables(rel_bias)
    blk = (hp, seq, LANES)
    nhb = heads // hp
    return pl.pallas_call(
        _chunk_attn_body,
        grid=(nhb, batch),
        in_specs=[pl.BlockSpec(blk, lambda h, b: (h, b, 0)),
                  pl.BlockSpec(blk, lambda h, b: (nhb + h, b, 0)),
                  pl.BlockSpec(blk, lambda h, b: (2 * nhb + h, b, 0)),
                  pl.BlockSpec((hp,) + tables.shape[1:], lambda h, b: (h, 0, 0, 0))],
        out_specs=pl.BlockSpec(blk, lambda h, b: (h, b, 0)),
        out_shape=jax.ShapeDtypeStruct((heads, batch * seq, LANES), jnp.bfloat16),
        scratch_shapes=[pltpu.VMEM((2, hp, A_QTILE, A_KTILE), jnp.float32),
                        pltpu.VMEM((2, hp, A_QTILE, A_KTILE), jnp.bfloat16),
                        pltpu.VMEM((2, hp, A_QTILE, LANES), jnp.float32)],
        compiler_params=_params("parallel", "parallel"),
        name="chunk_attn",
    )(proj, proj, proj, tables)


def _lru_body(xr_ref, cw_ref, cb_ref, wg_ref, ba_ref, bx_ref, lam_ref, o_ref,
              xbuf_ref, y_ref, pre_ref, a_ref, u_ref, hs_ref, h_ref, *, groups, gw, pieces):
    ns, nb, ts, _ = o_ref.shape
    rows = ts * nb
    hist = CONV_WIDTH * nb
    sub = LRU_SUB_STEPS
    pr = sub * nb

    @pl.when(pl.program_id(0) == 0)
    def _():
        xbuf_ref[0:hist, :] = jnp.zeros((hist, xbuf_ref.shape[1]), jnp.float32)
        h_ref[...] = jnp.zeros(h_ref.shape, jnp.float32)

    r_idx = lax.broadcasted_iota(jnp.int32, (pr, pr), 0)
    k_idx = lax.broadcasted_iota(jnp.int32, (pr, pr), 1)
    perm = jnp.where(k_idx == (r_idx % nb) * sub + r_idx // nb, 1.0, 0.0).astype(jnp.bfloat16)

    for part in range(ts // sub):
        steps = slice(part * sub, (part + 1) * sub)
        x_bm = jnp.concatenate(
            [jnp.concatenate([xr_ref[c, b, steps, :] for b in range(nb)], axis=0) for c in range(ns)],
            axis=1)
        xbuf_ref[hist + part * pr:hist + (part + 1) * pr, :] = jnp.dot(
            perm, x_bm, preferred_element_type=jnp.float32)

    def conv_and_gate_matmul(gi, slot):
        cs = slice(gi * gw, (gi + 1) * gw)
        y = cb_ref[:, cs] + cw_ref[CONV_WIDTH - 1:CONV_WIDTH, cs] * xbuf_ref[hist:hist + rows, cs]
        for tap in range(CONV_WIDTH - 1):
            off = hist - (CONV_WIDTH - 1 - tap) * nb
            y = y + cw_ref[tap:tap + 1, cs] * xbuf_ref[off:off + rows, cs]
        y_ref[slot] = y
        y_bf = y.astype(jnp.bfloat16)
        for p, (c0, c1, k0, k1) in enumerate(pieces):
            w = wg_ref[gi, p, :k1 - k0, :2 * (c1 - c0)]
            pre = jnp.dot(y_bf[:, k0:k1], w, preferred_element_type=jnp.float32)
            pre_ref[slot, :, c0:c1] = pre[:, :c1 - c0]
            pre_ref[slot, :, gw + c0:gw + c1] = pre[:, c1 - c0:]

    def gates(gi, slot):
        cs = slice(gi * gw, (gi + 1) * gw)
        half_ba = 0.5 * ba_ref[:, cs]
        half_bx = 0.5 * bx_ref[:, cs]
        lam = lam_ref[:, cs]
        softplus_neg_lam = jnp.maximum(-lam, 0.0) + jnp.log1p(jnp.exp(-jnp.abs(lam)))
        half_coef = (-0.5 * RG_C) * softplus_neg_lam
        for k in range(rows // LRU_GATE_ROWS):
            rk = slice(k * LRU_GATE_ROWS, (k + 1) * LRU_GATE_ROWS)
            tanh_r = jnp.tanh(pre_ref[slot, rk, :gw] + half_ba)
            tanh_i = jnp.tanh(pre_ref[slot, rk, gw:] + half_bx)
            log_a = half_coef * tanh_r + half_coef
            a = jnp.exp(log_a)
            quarter = (-0.25 * jnp.tanh(log_a)) * (1.0 + a * a)
            half_mult = quarter * lax.rsqrt(jnp.maximum(quarter, F32_TINY))
            a_ref[rk, cs] = a
            u_ref[rk, cs] = half_mult * ((tanh_i + 1.0) * y_ref[slot, rk, :])

    conv_and_gate_matmul(0, 0)
    for gi in range(groups):
        if gi + 1 < groups:
            conv_and_gate_matmul(gi + 1, (gi + 1) % 2)
        gates(gi, gi % 2)

    xbuf_ref[0:hist, :] = xbuf_ref[rows:rows + hist, :]

    def time_step(t, h):
        rs = pl.ds(pl.multiple_of(t * nb, nb), nb)
        h = a_ref[rs, :] * h + u_ref[rs, :]
        hs_ref[rs, :] = h.astype(hs_ref.dtype)
        return h

    h_ref[...] = lax.fori_loop(0, ts, time_step, h_ref[...])

    for part in range(ts // sub):
        steps = slice(part * sub, (part + 1) * sub)
        hs_bm = jnp.dot(perm, hs_ref[part * pr:(part + 1) * pr, :],
                        preferred_element_type=jnp.float32).astype(o_ref.dtype)
        for c in range(ns):
            for b in range(nb):
                o_ref[c, b, steps, :] = hs_bm[b * sub:(b + 1) * sub, c * LANES:(c + 1) * LANES]


def _gate_pieces(gw, bs):
    pieces = []
    for c0 in range(0, gw, MXU_WIDTH):
        c1 = min(c0 + MXU_WIDTH, gw)
        k0 = (c0 // bs) * bs // LANES * LANES
        k1 = min(pl.cdiv(((c1 - 1) // bs + 1) * bs, LANES) * LANES, gw)
        pieces.append((c0, c1, k0, k1))
    return pieces


def _gate_group_weights(w_a, w_x):
    nb, bs, _ = w_a.shape
    groups = nb // RG_GROUP_BLOCKS
    gw = RG_GROUP_BLOCKS * bs
    pieces = _gate_pieces(gw, bs)
    k_max = max(k1 - k0 for _, _, k0, k1 in pieces)
    n_max = max(2 * (c1 - c0) for c0, c1, _, _ in pieces)

    def block_diag(w):
        w = (0.5 * w).astype(jnp.bfloat16).reshape(groups, RG_GROUP_BLOCKS, bs, bs)
        rows = [jnp.pad(w[:, n], ((0, 0), (0, 0), (n * bs, gw - (n + 1) * bs)))
                for n in range(RG_GROUP_BLOCKS)]
        return jnp.concatenate(rows, axis=1)

    full_a, full_x = block_diag(w_a), block_diag(w_x)
    out = []
    for c0, c1, k0, k1 in pieces:
        piece = jnp.concatenate([full_a[:, k0:k1, c0:c1], full_x[:, k0:k1, c0:c1]], axis=2)
        out.append(jnp.pad(piece, ((0, 0), (0, k_max - (k1 - k0)), (0, n_max - 2 * (c1 - c0)))))
    return jnp.stack(out, axis=1)


def rg_lru(proj, conv_w, conv_b, w_a, b_a, w_x, b_x, lam, *, batch, seq):
    ns = proj.shape[0] // 2
    width = ns * LANES
    gw = RG_GROUP_BLOCKS * (width // RG_BLOCKS)
    groups = width // gw
    ts = min(LRU_TS, seq)
    rows = ts * batch
    assert seq % ts == 0 and gw % LANES == 0 and ts >= CONV_WIDTH and ts % LRU_SUB_STEPS == 0
    assert batch == LRU_SUB_STEPS and rows % LRU_GATE_ROWS == 0
    wg = _gate_group_weights(w_a, w_x)
    vec = lambda v: v.reshape(1, width).astype(jnp.float32)
    const2 = lambda s: (0, 0)
    blk = pl.BlockSpec((ns, batch, ts, LANES), lambda s: (0, 0, s, 0))
    out = pl.pallas_call(
        functools.partial(_lru_body, groups=groups, gw=gw, pieces=_gate_pieces(gw, width // RG_BLOCKS)),
        grid=(seq // ts,),
        in_specs=[blk,
                  pl.BlockSpec((CONV_WIDTH, width), const2),
                  pl.BlockSpec((1, width), const2),
                  pl.BlockSpec(wg.shape, lambda s: (0, 0, 0, 0), pipeline_mode=pl.Buffered(1)),
                  pl.BlockSpec((1, width), const2),
                  pl.BlockSpec((1, width), const2),
                  pl.BlockSpec((1, width), const2)],
        out_specs=blk,
        out_shape=jax.ShapeDtypeStruct((ns, batch, seq, LANES), jnp.bfloat16),
        scratch_shapes=[pltpu.VMEM((rows + CONV_WIDTH * batch, width), jnp.float32),
                        pltpu.VMEM((2, rows, gw), jnp.float32),
                        pltpu.VMEM((2, rows, 2 * gw), jnp.float32),
                        pltpu.VMEM((rows, width), jnp.float32),
                        pltpu.VMEM((rows, width), jnp.float32),
                        pltpu.VMEM((rows, width), jnp.bfloat16),
                        pltpu.VMEM((batch, width), jnp.float32)],
        compiler_params=_params("arbitrary"),
        name="rg_lru",
    )(proj.reshape(2 * ns, batch, seq, LANES), conv_w.astype(jnp.float32), vec(conv_b), wg,
      vec(b_a), vec(b_x), vec(lam))
    return out.reshape(ns, batch * seq, LANES)


FOX_PIECE_LANES = 16 * N_SPLIT


def _split3(x):
    hi = x.astype(jnp.bfloat16).astype(jnp.float32)
    r1 = x - hi
    mid = r1.astype(jnp.bfloat16).astype(jnp.float32)
    lo = (r1 - mid).astype(jnp.bfloat16).astype(jnp.float32)
    return hi, mid, lo


def _forget_gate_body(f_ref, fb_ref, qx_ref, kx_ref, *, heads):
    z = f_ref[0] + fb_ref[...]
    c = jnp.minimum(z, 0.0) - jnp.log1p(jnp.exp(-jnp.abs(z)))
    n = c.shape[0]
    row = lax.broadcasted_iota(jnp.int32, c.shape, 0)
    d = 1
    while d < n:
        c = c + jnp.where(row >= d, pltpu.roll(c, d, 0), 0.0)
        d *= 2
    lane = lax.broadcasted_iota(jnp.int32, c.shape, 1)
    hi, mid, lo = _split3(LOG2E * c)
    pieces = jnp.where(lane < heads, hi,
                       jnp.where(lane < 2 * heads, pltpu.roll(mid, heads, 1),
                                 jnp.where(lane < 3 * heads, pltpu.roll(lo, 2 * heads, 1), 0.0)))
    qx_ref[...] = pieces.astype(qx_ref.dtype)
    kx_ref[...] = pltpu.roll(-pieces, FOX_PIECE_LANES, 1).astype(kx_ref.dtype)


def forget_gate_slabs(f_logit, f_bias_padded, *, batch, seq, heads):
    assert heads * N_SPLIT == FOX_PIECE_LANES and 2 * FOX_PIECE_LANES <= LANES
    f3 = f_logit.reshape(batch, seq, LANES)
    shape = jax.ShapeDtypeStruct((batch * seq, LANES), jnp.bfloat16)
    spec = pl.BlockSpec((seq, LANES), lambda b: (b, 0))
    return pl.pallas_call(
        functools.partial(_forget_gate_body, heads=heads),
        grid=(batch,),
        in_specs=[pl.BlockSpec((1, seq, LANES), lambda b: (b, 0, 0)),
                  pl.BlockSpec((1, LANES), lambda b: (0, 0))],
        out_specs=[spec, spec],
        out_shape=[shape, shape],
        compiler_params=_params("parallel"),
        name="forget_gate",
    )(f3, f_bias_padded)


def _fox_body(q_ref, qx_ref, k_ref, kx_ref, v_ref, o_ref, s_ref, m_ref, l_ref, acc_ref):
    hp, seq, dh = q_ref.shape
    t = FOX_T
    heads = FOX_PIECE_LANES // N_SPLIT
    steps = [(i, j) for i in range(seq // t) for j in range(i + 1)]
    lane = lax.broadcasted_iota(jnp.int32, (1, LANES), 1)

    def augment(x_ref, px_ref, h, r, first_lane):
        rel = lane - (first_lane + pl.program_id(1) * hp + h)
        select = (rel == 0) | (rel == heads) | (rel == 2 * heads)
        extra = jnp.where(select, jnp.ones((), px_ref.dtype), px_ref[r, :])
        return jnp.concatenate([x_ref[h, r, :], extra], axis=1)

    def parts(i, j):
        if i != j:
            return [(0, t, t)]
        return [(0, t // 2, t // 2), (t // 2, t, t)]

    def scores(n, slot):
        i, j = steps[n]
        for h in range(hp):
            for r0, r1, n_keys in parts(i, j):
                q_aug = augment(q_ref, qx_ref, h, slice(i * t + r0, i * t + r1), FOX_PIECE_LANES)
                k_aug = augment(k_ref, kx_ref, h, slice(j * t, j * t + n_keys), 0)
                s = lax.dot_general(q_aug, k_aug, (((1,), (1,)), ((), ())),
                                    preferred_element_type=jnp.float32)
                if i == j:
                    r = lax.broadcasted_iota(jnp.int32, s.shape, 0) + r0
                    c = lax.broadcasted_iota(jnp.int32, s.shape, 1)
                    s = jnp.where(c <= r, s, NEG_INF)
                s_ref[slot, h, r0:r1, :n_keys] = s

    def finish(n, slot):
        i, j = steps[n]
        work = [(h, rc, rc + FOX_ROWS, n_keys) for h in range(hp) for r0, r1, n_keys in parts(i, j)
                for rc in range(r0, r1, FOX_ROWS)]
        for h, r0, r1, n_keys in work:
            s = s_ref[slot, h, r0:r1, :n_keys]
            m_cur = jnp.max(s, axis=1, keepdims=True)
            if j == 0:
                m_next = jnp.broadcast_to(m_cur, (r1 - r0, LANES))
            else:
                m_prev = m_ref[h, r0:r1, :]
                m_next = jnp.maximum(m_prev, m_cur)
                alpha = jnp.exp2(m_prev - m_next)
            m_ref[h, r0:r1, :] = m_next
            p = jnp.exp2(s - jnp.tile(m_next, (1, n_keys // LANES)))
            l_cur = jnp.sum(p, axis=1, keepdims=True)
            pv = jnp.dot(p.astype(jnp.bfloat16), v_ref[h, j * t:j * t + n_keys, :],
                         preferred_element_type=jnp.float32)
            if j == 0:
                l_new = jnp.broadcast_to(l_cur, (r1 - r0, LANES))
                acc = pv
            else:
                l_new = l_cur + alpha * l_ref[h, r0:r1, :]
                acc = alpha * acc_ref[h, r0:r1, :] + pv
            if j == i:
                o_ref[h, i * t + r0:i * t + r1, :] = (acc / l_new).astype(o_ref.dtype)
            else:
                l_ref[h, r0:r1, :] = l_new
                acc_ref[h, r0:r1, :] = acc

    scores(0, 0)
    for n in range(len(steps)):
        if n + 1 < len(steps):
            scores(n + 1, (n + 1) % 2)
        finish(n, n % 2)


def forgetting_attention(proj, qx, kx, *, batch, seq):
    heads = C_HEADS
    hp = HEADS_PER_STEP
    assert proj.shape[0] == 4 * heads and seq % FOX_T == 0
    blk = (hp, seq, LANES)
    nhb = heads // hp
    return pl.pallas_call(
        _fox_body,
        grid=(batch, nhb),
        in_specs=[pl.BlockSpec(blk, lambda b, h: (h, b, 0)),
                  pl.BlockSpec((seq, LANES), lambda b, h: (b, 0)),
                  pl.BlockSpec(blk, lambda b, h: (nhb + h, b, 0)),
                  pl.BlockSpec((seq, LANES), lambda b, h: (b, 0)),
                  pl.BlockSpec(blk, lambda b, h: (2 * nhb + h, b, 0))],
        out_specs=pl.BlockSpec(blk, lambda b, h: (h, b, 0)),
        out_shape=jax.ShapeDtypeStruct((heads, batch * seq, LANES), jnp.bfloat16),
        scratch_shapes=[pltpu.VMEM((2, hp, FOX_T, FOX_T), jnp.float32),
                        pltpu.VMEM((hp, FOX_T, LANES), jnp.float32),
                        pltpu.VMEM((hp, FOX_T, LANES), jnp.float32),
                        pltpu.VMEM((hp, FOX_T, LANES), jnp.float32)],
        compiler_params=_params("parallel", "parallel"),
        name="fox_attn",
    )(proj, qx, proj, kx, proj)


def _q_column_scale(n, width, dh):
    return jnp.where(jnp.arange(n) < width, LOG2E * dh ** -0.5, 1.0).astype(jnp.float32)


def _split_cast_body(w_ref, scale_ref, main_ref, tail_ref):
    n_main = main_ref.shape[1]
    n_tail = w_ref.shape[2] - n_main
    main_ref[...] = (w_ref[0, :, :n_main] * scale_ref[...]).astype(main_ref.dtype)
    tail_ref[...] = jnp.zeros(tail_ref.shape, tail_ref.dtype)
    tail_ref[:, :n_tail] = w_ref[0, :, n_main:].astype(tail_ref.dtype)


def split_cast_weight(w_stack, j, n_main, col_scale, *, tk=256):
    _, d, n = w_stack.shape
    assert d % tk == 0 and n_main % LANES == 0 and 0 < n - n_main <= LANES
    return pl.pallas_call(
        _split_cast_body,
        grid=(d // tk,),
        in_specs=[pl.BlockSpec((1, tk, n), lambda i: (j, i, 0)),
                  pl.BlockSpec((1, n_main), lambda i: (0, 0))],
        out_specs=[pl.BlockSpec((tk, n_main), lambda i: (i, 0)),
                   pl.BlockSpec((tk, LANES), lambda i: (i, 0))],
        out_shape=[jax.ShapeDtypeStruct((d, n_main), jnp.bfloat16),
                   jax.ShapeDtypeStruct((d, LANES), jnp.bfloat16)],
        compiler_params=_params("parallel"),
        name="split_cast",
    )(w_stack, col_scale.reshape(1, n_main))


def _input_projection(xs, h, gamma, w, w2=None, *, tn=2048):
    if h is None:
        return norm_proj(xs, gamma, w, w2)
    return proj_slabs(h, w, w2, tn=tn)


def kernel(x, norm_pre, norm_post, a_w_in, a_rel_bias, a_w_out, b_w_in, b_conv_w, b_conv_b,
           b_gate_a_w, b_gate_a_b, b_gate_x_w, b_gate_x_b, b_lambda, b_w_out,
           c_w_in, c_f_bias, c_w_out):
    batch, seq, d = x.shape
    depth = norm_pre.shape[0]
    bf = lambda w: w.astype(jnp.bfloat16)
    xs = x.reshape(batch * seq, d)
    h = None
    for i in range(depth):
        m, j = i % 3, i // 3
        gamma_next = norm_pre[i + 1] if i + 1 < depth else None
        if m == 0:
            e = a_w_out.shape[1]
            w_in = bf(a_w_in[j] * _q_column_scale(4 * e, e, e // A_HEADS))
            proj = _input_projection(xs, h, norm_pre[i], w_in)
            o = chunk_attention(proj, a_rel_bias[j], batch=batch, seq=seq)
            res = out_proj(o, proj, 3, bf(a_w_out[j]), xs, norm_post[i], gamma_next)
        elif m == 1:
            proj = _input_projection(xs, h, norm_pre[i], bf(b_w_in[j]), tn=b_w_in.shape[2] // 4)
            hs = rg_lru(proj, b_conv_w[j], b_conv_b[j], b_gate_a_w[j], b_gate_a_b[j],
                        b_gate_x_w[j], b_gate_x_b[j], b_lambda[j], batch=batch, seq=seq)
            res = out_proj(hs, proj, 1, bf(b_w_out[j]), xs, norm_post[i], gamma_next)
        else:
            e = c_w_out.shape[1]
            n_f = c_w_in.shape[2] - 4 * e
            w_main, w_f = split_cast_weight(c_w_in, j, 4 * e, _q_column_scale(4 * e, e, e // C_HEADS))
            f_bias = jnp.pad(c_f_bias[j].astype(jnp.float32), (0, LANES - n_f)).reshape(1, LANES)
            proj, f_logit = _input_projection(xs, h, norm_pre[i], w_main, w_f)
            qx, kx = forget_gate_slabs(f_logit, f_bias, batch=batch, seq=seq, heads=n_f)
            o = forgetting_attention(proj, qx, kx, batch=batch, seq=seq)
            res = out_proj(o, proj, 3, bf(c_w_out[j]), xs, norm_post[i], gamma_next)
        xs, h = res if gamma_next is not None else (res, None)
    return xs.reshape(batch, seq, d)
```

```python
import functools

import jax
import jax.numpy as jnp
import numpy as np
from jax import lax
from jax.experimental import pallas as pl
from jax.experimental.pallas import tpu as pltpu

RMS_EPS = 1e-6
NEG_INF = -1e30
F32_TINY = float(np.finfo(np.float32).tiny)
LOG2E = float(np.log2(np.e))
LANES = 128
MXU_WIDTH = 256

A_HEADS = 16
A_CHUNK = 64
A_LEFT_CHUNKS = 8
A_REL_CLIP = 256
A_HEADS_PER_STEP = 4
A_QTILE = 2 * A_CHUNK
A_LEFT = A_LEFT_CHUNKS * A_CHUNK
A_KTILE = A_LEFT + A_QTILE

RG_BLOCKS = 16
RG_GROUP_BLOCKS = 4
CONV_WIDTH = 4
RG_C = 8.0
LRU_TS = 32
LRU_GATE_ROWS = 64
LRU_SUB_STEPS = 16

C_HEADS = 16
FOX_T = 512
FOX_ROWS = 128
N_SPLIT = 3

HEADS_PER_STEP = 4
OUT_PROJ_ROWS = 256
VMEM_LIMIT_BYTES = 56 * 1024 * 1024


def _params(*sem):
    return pltpu.CompilerParams(dimension_semantics=sem, vmem_limit_bytes=VMEM_LIMIT_BYTES)


def _norm_proj_body(x_ref, gam_ref, w_ref, *rest, has_extra):
    if has_extra:
        w2_ref, o_ref, o2_ref, xn_ref = rest
    else:
        o_ref, xn_ref = rest

    @pl.when(pl.program_id(1) == 0)
    def _():
        x = x_ref[...]
        ms = jnp.mean(x * x, axis=-1, keepdims=True)
        xn_ref[...] = (x * lax.rsqrt(ms + RMS_EPS) * gam_ref[...]).astype(jnp.bfloat16)
        if has_extra:
            o2_ref[...] = jnp.dot(xn_ref[...], w2_ref[...], preferred_element_type=jnp.float32)

    res = jnp.dot(xn_ref[...], w_ref[...], preferred_element_type=jnp.float32)
    for c in range(o_ref.shape[0]):
        o_ref[c] = res[:, c * LANES:(c + 1) * LANES].astype(o_ref.dtype)


def norm_proj(x, gamma, w, w2=None, *, tm=1024, tn=2048):
    m, d = x.shape
    n = w.shape[1]
    tm, tn = min(tm, m), min(tn, n)
    assert m % tm == 0 and n % tn == 0 and tn % LANES == 0
    in_specs = [pl.BlockSpec((tm, d), lambda i, j: (i, 0)),
                pl.BlockSpec((1, d), lambda i, j: (0, 0)),
                pl.BlockSpec((d, tn), lambda i, j: (0, j))]
    out_shape = [jax.ShapeDtypeStruct((n // LANES, m, LANES), jnp.bfloat16)]
    out_specs = [pl.BlockSpec((tn // LANES, tm, LANES), lambda i, j: (j, i, 0))]
    args = [x, gamma.reshape(1, d), w]
    if w2 is not None:
        in_specs.append(pl.BlockSpec((d, w2.shape[1]), lambda i, j: (0, 0)))
        out_shape.append(jax.ShapeDtypeStruct((m, w2.shape[1]), jnp.float32))
        out_specs.append(pl.BlockSpec((tm, w2.shape[1]), lambda i, j: (i, 0)))
        args.append(w2)
    out = pl.pallas_call(
        functools.partial(_norm_proj_body, has_extra=w2 is not None),
        grid=(m // tm, n // tn),
        in_specs=in_specs, out_specs=out_specs, out_shape=out_shape,
        scratch_shapes=[pltpu.VMEM((tm, d), jnp.bfloat16)],
        compiler_params=_params("parallel", "arbitrary"),
        name="norm_proj",
    )(*args)
    return out if w2 is not None else out[0]


def _proj_body(h_ref, w_ref, *rest, has_extra):
    if has_extra:
        w2_ref, o_ref, o2_ref = rest

        @pl.when(pl.program_id(1) == 0)
        def _():
            o2_ref[...] = jnp.dot(h_ref[...], w2_ref[...], preferred_element_type=jnp.float32)
    else:
        (o_ref,) = rest

    res = jnp.dot(h_ref[...], w_ref[...], preferred_element_type=jnp.float32)
    for c in range(o_ref.shape[0]):
        o_ref[c] = res[:, c * LANES:(c + 1) * LANES].astype(o_ref.dtype)


def proj_slabs(h, w, w2=None, *, tm=1024, tn=2048):
    m, d = h.shape
    n = w.shape[1]
    tm, tn = min(tm, m), min(tn, n)
    assert m % tm == 0 and n % tn == 0 and tn % LANES == 0
    in_specs = [pl.BlockSpec((tm, d), lambda i, j: (i, 0)),
                pl.BlockSpec((d, tn), lambda i, j: (0, j))]
    out_shape = [jax.ShapeDtypeStruct((n // LANES, m, LANES), jnp.bfloat16)]
    out_specs = [pl.BlockSpec((tn // LANES, tm, LANES), lambda i, j: (j, i, 0))]
    args = [h, w]
    if w2 is not None:
        in_specs.append(pl.BlockSpec((d, w2.shape[1]), lambda i, j: (0, 0)))
        out_shape.append(jax.ShapeDtypeStruct((m, w2.shape[1]), jnp.float32))
        out_specs.append(pl.BlockSpec((tm, w2.shape[1]), lambda i, j: (i, 0)))
        args.append(w2)
    out = pl.pallas_call(
        functools.partial(_proj_body, has_extra=w2 is not None),
        grid=(m // tm, n // tn),
        in_specs=in_specs, out_specs=out_specs, out_shape=out_shape,
        compiler_params=_params("parallel", "arbitrary"),
        name="proj",
    )(*args)
    return out if w2 is not None else out[0]


def _out_proj_body(a_ref, g_ref, w_ref, x_ref, gam_ref, *rest, has_next):
    if has_next:
        gam_next_ref, o_ref, h_next_ref = rest
    else:
        (o_ref,) = rest
    ns, tm, _ = a_ref.shape
    n_chunks = tm // OUT_PROJ_ROWS

    def gated(c):
        rs = slice(c * OUT_PROJ_ROWS, (c + 1) * OUT_PROJ_ROWS)
        g = jnp.concatenate([g_ref[s, rs, :] for s in range(ns)], axis=1).astype(jnp.float32)
        a = jnp.concatenate([a_ref[s, rs, :] for s in range(ns)], axis=1).astype(jnp.float32)
        half_g = 0.5 * g
        return (a * (half_g * (1.0 + jnp.tanh(half_g)))).astype(jnp.bfloat16)

    def finish(c, y):
        rs = slice(c * OUT_PROJ_ROWS, (c + 1) * OUT_PROJ_ROWS)
        ms = jnp.mean(y * y, axis=-1, keepdims=True)
        x_new = x_ref[rs, :] + y * lax.rsqrt(ms + RMS_EPS) * gam_ref[...]
        o_ref[rs, :] = x_new
        if has_next:
            ms_new = jnp.mean(x_new * x_new, axis=-1, keepdims=True)
            h_next_ref[rs, :] = (x_new * lax.rsqrt(ms_new + RMS_EPS)
                                 * gam_next_ref[...]).astype(h_next_ref.dtype)

    a_next = gated(0)
    y_prev = None
    for c in range(n_chunks):
        a_cur = a_next
        if c + 1 < n_chunks:
            a_next = gated(c + 1)
        y = jnp.dot(a_cur, w_ref[...], preferred_element_type=jnp.float32)
        if y_prev is not None:
            finish(c - 1, y_prev)
        y_prev = y
    finish(n_chunks - 1, y_prev)


def out_proj(a, proj, g_block, w, x, gamma, gamma_next=None, *, tm=512):
    ns, m, _ = a.shape
    e, d = w.shape
    assert ns * LANES == e
    tm = min(tm, m)
    assert m % tm == 0 and tm % OUT_PROJ_ROWS == 0
    row_spec = pl.BlockSpec((tm, d), lambda i: (i, 0))
    vec_spec = pl.BlockSpec((1, d), lambda i: (0, 0))
    in_specs = [pl.BlockSpec((ns, tm, LANES), lambda i: (0, i, 0)),
                pl.BlockSpec((ns, tm, LANES), lambda i: (g_block, i, 0)),
                pl.BlockSpec((e, d), lambda i: (0, 0), pipeline_mode=pl.Buffered(1)),
                row_spec, vec_spec]
    out_shape = [jax.ShapeDtypeStruct((m, d), jnp.float32)]
    out_specs = [row_spec]
    args = [a, proj, w, x, gamma.reshape(1, d)]
    if gamma_next is not None:
        in_specs.append(vec_spec)
        out_shape.append(jax.ShapeDtypeStruct((m, d), jnp.bfloat16))
        out_specs.append(row_spec)
        args.append(gamma_next.reshape(1, d))
    out = pl.pallas_call(
        functools.partial(_out_proj_body, has_next=gamma_next is not None),
        grid=(m // tm,),
        in_specs=in_specs, out_specs=out_specs, out_shape=out_shape,
        compiler_params=_params("parallel"),
        name="out_proj",
    )(*args)
    return out if gamma_next is not None else out[0]


def _chunk_attn_body(q_ref, k_ref, v_ref, bias_ref, o_ref, s_ref, p_ref, l_ref):
    hp, seq, _ = q_ref.shape
    n_tiles = seq // A_QTILE
    n_cut = A_LEFT // A_QTILE

    def window(t):
        q0 = t * A_QTILE
        k0 = max(q0 - A_LEFT, 0)
        n_keys = q0 + A_QTILE - k0
        return slice(q0, q0 + A_QTILE), slice(k0, k0 + n_keys), n_keys

    def scores(t, slot):
        q_rows, k_rows, n_keys = window(t)
        for h in range(hp):
            s = lax.dot_general(q_ref[h, q_rows, :], k_ref[h, k_rows, :], (((1,), (1,)), ((), ())),
                                preferred_element_type=jnp.float32)
            s_ref[slot, h, :, :n_keys] = s + bias_ref[h, min(t, n_cut), :, :n_keys]

    def softmax(t, slot):
        _, _, n_keys = window(t)
        for h in range(hp):
            s = s_ref[slot, h, :, :n_keys]
            p = jnp.exp2(s - jnp.max(s, axis=-1, keepdims=True))
            l_ref[slot, h] = jnp.broadcast_to(jnp.sum(p, axis=-1, keepdims=True), (A_QTILE, LANES))
            p_ref[slot, h, :, :n_keys] = p.astype(p_ref.dtype)

    def weighted_values(t, slot):
        q_rows, k_rows, n_keys = window(t)
        for h in range(hp):
            o = jnp.dot(p_ref[slot, h, :, :n_keys], v_ref[h, k_rows, :],
                        preferred_element_type=jnp.float32)
            o_ref[h, q_rows, :] = (o / l_ref[slot, h]).astype(o_ref.dtype)

    scores(0, 0)
    scores(1, 1)
    softmax(0, 0)
    for t in range(n_tiles):
        if t + 2 < n_tiles:
            scores(t + 2, t % 2)
        if t + 1 < n_tiles:
            softmax(t + 1, (t + 1) % 2)
        weighted_values(t, t % 2)


def _bias_tables_body(u_ref, o_ref):
    n_tab, qt, kt = o_ref.shape[1:]
    qi = lax.broadcasted_iota(jnp.int32, (qt, kt), 0)
    j = lax.broadcasted_iota(jnp.int32, (qt, kt), 1)
    chunk_start = (qi // A_CHUNK) * A_CHUNK
    for i in range(n_tab):
        diag = jnp.broadcast_to(u_ref[0, i:i + 1, :], (qt, u_ref.shape[2]))
        table = pltpu.roll(diag, 0, 1, stride=1, stride_axis=0)[:, :kt]
        band_start = chunk_start + (i * qt - A_LEFT)
        visible = (j >= band_start) & (j < band_start + (A_LEFT_CHUNKS + 1) * A_CHUNK)
        o_ref[0, i] = jnp.where(visible, LOG2E * table, NEG_INF)


def _band_bias_tables(rel_bias):
    heads = rel_bias.shape[0]
    n_tab = A_LEFT // A_QTILE + 1
    n_diag = pl.cdiv(A_QTILE + A_KTILE - 1, LANES) * LANES
    m = np.arange(n_diag)
    key_minus_query = np.where(m < A_KTILE, m, m - n_diag)
    offsets = A_QTILE * np.arange(n_tab)[:, None]
    idx = np.clip(offsets - key_minus_query[None, :], -A_REL_CLIP, A_REL_CLIP) + A_REL_CLIP
    diags = rel_bias.astype(jnp.float32)[:, idx]
    return pl.pallas_call(
        _bias_tables_body,
        grid=(heads,),
        in_specs=[pl.BlockSpec((1, n_tab, n_diag), lambda h: (h, 0, 0))],
        out_specs=pl.BlockSpec((1, n_tab, A_QTILE, A_KTILE), lambda h: (h, 0, 0, 0)),
        out_shape=jax.ShapeDtypeStruct((heads, n_tab, A_QTILE, A_KTILE), jnp.float32),
        compiler_params=_params("parallel"),
        name="bias_tables",
    )(diags)


def chunk_attention(proj, rel_bias, *, batch, seq):
    heads = A_HEADS
    hp = A_HEADS_PER_STEP
    assert proj.shape[0] == 4 * heads and seq % (2 * A_QTILE) == 0 and seq >= A_KTILE
    tables = _band_bias_tables(rel_bias)
    blk = (hp, seq, LANES)
    nhb = heads // hp
    return pl.pallas_call(
        _chunk_attn_body,
        grid=(nhb, batch),
        in_specs=[pl.BlockSpec(blk, lambda h, b: (h, b, 0)),
                  pl.BlockSpec(blk, lambda h, b: (nhb + h, b, 0)),
                  pl.BlockSpec(blk, lambda h, b: (2 * nhb + h, b, 0)),
                  pl.BlockSpec((hp,) + tables.shape[1:], lambda h, b: (h, 0, 0, 0))],
        out_specs=pl.BlockSpec(blk, lambda h, b: (h, b, 0)),
        out_shape=jax.ShapeDtypeStruct((heads, batch * seq, LANES), jnp.bfloat16),
        scratch_shapes=[pltpu.VMEM((2, hp, A_QTILE, A_KTILE), jnp.float32),
                        pltpu.VMEM((2, hp, A_QTILE, A_KTILE), jnp.bfloat16),
                        pltpu.VMEM((2, hp, A_QTILE, LANES), jnp.float32)],
        compiler_params=_params("parallel", "parallel"),
        name="chunk_attn",
    )(proj, proj, proj, tables)


def _lru_body(xr_ref, cw_ref, cb_ref, wg_ref, ba_ref, bx_ref, lam_ref, o_ref,
              xbuf_ref, y_ref, pre_ref, a_ref, u_ref, hs_ref, h_ref, *, groups, gw, pieces):
    ns, nb, ts, _ = o_ref.shape
    rows = ts * nb
    hist = CONV_WIDTH * nb
    sub = LRU_SUB_STEPS
    pr = sub * nb

    @pl.when(pl.program_id(0) == 0)
    def _():
        xbuf_ref[0:hist, :] = jnp.zeros((hist, xbuf_ref.shape[1]), jnp.float32)
        h_ref[...] = jnp.zeros(h_ref.shape, jnp.float32)

    r_idx = lax.broadcasted_iota(jnp.int32, (pr, pr), 0)
    k_idx = lax.broadcasted_iota(jnp.int32, (pr, pr), 1)
    perm = jnp.where(k_idx == (r_idx % nb) * sub + r_idx // nb, 1.0, 0.0).astype(jnp.bfloat16)

    for part in range(ts // sub):
        steps = slice(part * sub, (part + 1) * sub)
        x_bm = jnp.concatenate(
            [jnp.concatenate([xr_ref[c, b, steps, :] for b in range(nb)], axis=0) for c in range(ns)],
            axis=1)
        xbuf_ref[hist + part * pr:hist + (part + 1) * pr, :] = jnp.dot(
            perm, x_bm, preferred_element_type=jnp.float32)

    def conv_and_gate_matmul(gi, slot):
        cs = slice(gi * gw, (gi + 1) * gw)
        y = cb_ref[:, cs] + cw_ref[CONV_WIDTH - 1:CONV_WIDTH, cs] * xbuf_ref[hist:hist + rows, cs]
        for tap in range(CONV_WIDTH - 1):
            off = hist - (CONV_WIDTH - 1 - tap) * nb
            y = y + cw_ref[tap:tap + 1, cs] * xbuf_ref[off:off + rows, cs]
        y_ref[slot] = y
        y_bf = y.astype(jnp.bfloat16)
        for p, (c0, c1, k0, k1) in enumerate(pieces):
            w = wg_ref[gi, p, :k1 - k0, :2 * (c1 - c0)]
            pre = jnp.dot(y_bf[:, k0:k1], w, preferred_element_type=jnp.float32)
            pre_ref[slot, :, c0:c1] = pre[:, :c1 - c0]
            pre_ref[slot, :, gw + c0:gw + c1] = pre[:, c1 - c0:]

    def gates(gi, slot):
        cs = slice(gi * gw, (gi + 1) * gw)
        half_ba = 0.5 * ba_ref[:, cs]
        half_bx = 0.5 * bx_ref[:, cs]
        lam = lam_ref[:, cs]
        softplus_neg_lam = jnp.maximum(-lam, 0.0) + jnp.log1p(jnp.exp(-jnp.abs(lam)))
        half_coef = (-0.5 * RG_C) * softplus_neg_lam
        for k in range(rows // LRU_GATE_ROWS):
            rk = slice(k * LRU_GATE_ROWS, (k + 1) * LRU_GATE_ROWS)
            tanh_r = jnp.tanh(pre_ref[slot, rk, :gw] + half_ba)
            tanh_i = jnp.tanh(pre_ref[slot, rk, gw:] + half_bx)
            log_a = half_coef * tanh_r + half_coef
            a = jnp.exp(log_a)
            quarter = (-0.25 * jnp.tanh(log_a)) * (1.0 + a * a)
            half_mult = quarter * lax.rsqrt(jnp.maximum(quarter, F32_TINY))
            a_ref[rk, cs] = a
            u_ref[rk, cs] = half_mult * ((tanh_i + 1.0) * y_ref[slot, rk, :])

    conv_and_gate_matmul(0, 0)
    for gi in range(groups):
        if gi + 1 < groups:
            conv_and_gate_matmul(gi + 1, (gi + 1) % 2)
        gates(gi, gi % 2)

    xbuf_ref[0:hist, :] = xbuf_ref[rows:rows + hist, :]

    def time_step(t, h):
        rs = pl.ds(pl.multiple_of(t * nb, nb), nb)
        h = a_ref[rs, :] * h + u_ref[rs, :]
        hs_ref[rs, :] = h.astype(hs_ref.dtype)
        return h

    h_ref[...] = lax.fori_loop(0, ts, time_step, h_ref[...])

    for part in range(ts // sub):
        steps = slice(part * sub, (part + 1) * sub)
        hs_bm = jnp.dot(perm, hs_ref[part * pr:(part + 1) * pr, :],
                        preferred_element_type=jnp.float32).astype(o_ref.dtype)
        for c in range(ns):
            for b in range(nb):
                o_ref[c, b, steps, :] = hs_bm[b * sub:(b + 1) * sub, c * LANES:(c + 1) * LANES]


def _gate_pieces(gw, bs):
    pieces = []
    for c0 in range(0, gw, MXU_WIDTH):
        c1 = min(c0 + MXU_WIDTH, gw)
        k0 = (c0 // bs) * bs // LANES * LANES
        k1 = min(pl.cdiv(((c1 - 1) // bs + 1) * bs, LANES) * LANES, gw)
        pieces.append((c0, c1, k0, k1))
    return pieces


def _gate_group_weights(w_a, w_x):
    nb, bs, _ = w_a.shape
    groups = nb // RG_GROUP_BLOCKS
    gw = RG_GROUP_BLOCKS * bs
    pieces = _gate_pieces(gw, bs)
    k_max = max(k1 - k0 for _, _, k0, k1 in pieces)
    n_max = max(2 * (c1 - c0) for c0, c1, _, _ in pieces)

    def block_diag(w):
        w = (0.5 * w).astype(jnp.bfloat16).reshape(groups, RG_GROUP_BLOCKS, bs, bs)
        rows = [jnp.pad(w[:, n], ((0, 0), (0, 0), (n * bs, gw - (n + 1) * bs)))
                for n in range(RG_GROUP_BLOCKS)]
        return jnp.concatenate(rows, axis=1)

    full_a, full_x = block_diag(w_a), block_diag(w_x)
    out = []
    for c0, c1, k0, k1 in pieces:
        piece = jnp.concatenate([full_a[:, k0:k1, c0:c1], full_x[:, k0:k1, c0:c1]], axis=2)
        out.append(jnp.pad(piece, ((0, 0), (0, k_max - (k1 - k0)), (0, n_max - 2 * (c1 - c0)))))
    return jnp.stack(out, axis=1)


def rg_lru(proj, conv_w, conv_b, w_a, b_a, w_x, b_x, lam, *, batch, seq):
    ns = proj.shape[0] // 2
    width = ns * LANES
    gw = RG_GROUP_BLOCKS * (width // RG_BLOCKS)
    groups = width // gw
    ts = min(LRU_TS, seq)
    rows = ts * batch
    assert seq % ts == 0 and gw % LANES == 0 and ts >= CONV_WIDTH and ts % LRU_SUB_STEPS == 0
    assert batch == LRU_SUB_STEPS and rows % LRU_GATE_ROWS == 0
    wg = _gate_group_weights(w_a, w_x)
    vec = lambda v: v.reshape(1, width).astype(jnp.float32)
    const2 = lambda s: (0, 0)
    blk = pl.BlockSpec((ns, batch, ts, LANES), lambda s: (0, 0, s, 0))
    out = pl.pallas_call(
        functools.partial(_lru_body, groups=groups, gw=gw, pieces=_gate_pieces(gw, width // RG_BLOCKS)),
        grid=(seq // ts,),
        in_specs=[blk,
                  pl.BlockSpec((CONV_WIDTH, width), const2),
                  pl.BlockSpec((1, width), const2),
                  pl.BlockSpec(wg.shape, lambda s: (0, 0, 0, 0), pipeline_mode=pl.Buffered(1)),
                  pl.BlockSpec((1, width), const2),
                  pl.BlockSpec((1, width), const2),
                  pl.BlockSpec((1, width), const2)],
        out_specs=blk,
        out_shape=jax.ShapeDtypeStruct((ns, batch, seq, LANES), jnp.bfloat16),
        scratch_shapes=[pltpu.VMEM((rows + CONV_WIDTH * batch, width), jnp.float32),
                        pltpu.VMEM((2, rows, gw), jnp.float32),
                        pltpu.VMEM((2, rows, 2 * gw), jnp.float32),
                        pltpu.VMEM((rows, width), jnp.float32),
                        pltpu.VMEM((rows, width), jnp.float32),
                        pltpu.VMEM((rows, width), jnp.bfloat16),
                        pltpu.VMEM((batch, width), jnp.float32)],
        compiler_params=_params("arbitrary"),
        name="rg_lru",
    )(proj.reshape(2 * ns, batch, seq, LANES), conv_w.astype(jnp.float32), vec(conv_b), wg,
      vec(b_a), vec(b_x), vec(lam))
    return out.reshape(ns, batch * seq, LANES)


FOX_PIECE_LANES = 16 * N_SPLIT


def _split3(x):
    hi = x.astype(jnp.bfloat16).astype(jnp.float32)
    r1 = x - hi
    mid = r1.astype(jnp.bfloat16).astype(jnp.float32)
    lo = (r1 - mid).astype(jnp.bfloat16).astype(jnp.float32)
    return hi, mid, lo


def _forget_gate_body(f_ref, fb_ref, qx_ref, kx_ref, *, heads):
    z = f_ref[0] + fb_ref[...]
    c = jnp.minimum(z, 0.0) - jnp.log1p(jnp.exp(-jnp.abs(z)))
    n = c.shape[0]
    row = lax.broadcasted_iota(jnp.int32, c.shape, 0)
    d = 1
    while d < n:
        c = c + jnp.where(row >= d, pltpu.roll(c, d, 0), 0.0)
        d *= 2
    lane = lax.broadcasted_iota(jnp.int32, c.shape, 1)
    hi, mid, lo = _split3(LOG2E * c)
    pieces = jnp.where(lane < heads, hi,
                       jnp.where(lane < 2 * heads, pltpu.roll(mid, heads, 1),
                                 jnp.where(lane < 3 * heads, pltpu.roll(lo, 2 * heads, 1), 0.0)))
    qx_ref[...] = pieces.astype(qx_ref.dtype)
    kx_ref[...] = pltpu.roll(-pieces, FOX_PIECE_LANES, 1).astype(kx_ref.dtype)


def forget_gate_slabs(f_logit, f_bias_padded, *, batch, seq, heads):
    assert heads * N_SPLIT == FOX_PIECE_LANES and 2 * FOX_PIECE_LANES <= LANES
    f3 = f_logit.reshape(batch, seq, LANES)
    shape = jax.ShapeDtypeStruct((batch * seq, LANES), jnp.bfloat16)
    spec = pl.BlockSpec((seq, LANES), lambda b: (b, 0))
    return pl.pallas_call(
        functools.partial(_forget_gate_body, heads=heads),
        grid=(batch,),
        in_specs=[pl.BlockSpec((1, seq, LANES), lambda b: (b, 0, 0)),
                  pl.BlockSpec((1, LANES), lambda b: (0, 0))],
        out_specs=[spec, spec],
        out_shape=[shape, shape],
        compiler_params=_params("parallel"),
        name="forget_gate",
    )(f3, f_bias_padded)


def _fox_body(q_ref, qx_ref, k_ref, kx_ref, v_ref, o_ref, s_ref, m_ref, l_ref, acc_ref):
    hp, seq, dh = q_ref.shape
    t = FOX_T
    heads = FOX_PIECE_LANES // N_SPLIT
    steps = [(i, j) for i in range(seq // t) for j in range(i + 1)]
    lane = lax.broadcasted_iota(jnp.int32, (1, LANES), 1)

    def augment(x_ref, px_ref, h, r, first_lane):
        rel = lane - (first_lane + pl.program_id(1) * hp + h)
        select = (rel == 0) | (rel == heads) | (rel == 2 * heads)
        extra = jnp.where(select, jnp.ones((), px_ref.dtype), px_ref[r, :])
        return jnp.concatenate([x_ref[h, r, :], extra], axis=1)

    def parts(i, j):
        if i != j:
            return [(0, t, t)]
        return [(0, t // 2, t // 2), (t // 2, t, t)]

    def scores(n, slot):
        i, j = steps[n]
        for h in range(hp):
            for r0, r1, n_keys in parts(i, j):
                q_aug = augment(q_ref, qx_ref, h, slice(i * t + r0, i * t + r1), FOX_PIECE_LANES)
                k_aug = augment(k_ref, kx_ref, h, slice(j * t, j * t + n_keys), 0)
                s = lax.dot_general(q_aug, k_aug, (((1,), (1,)), ((), ())),
                                    preferred_element_type=jnp.float32)
                if i == j:
                    r = lax.broadcasted_iota(jnp.int32, s.shape, 0) + r0
                    c = lax.broadcasted_iota(jnp.int32, s.shape, 1)
                    s = jnp.where(c <= r, s, NEG_INF)
                s_ref[slot, h, r0:r1, :n_keys] = s

    def finish(n, slot):
        i, j = steps[n]
        work = [(h, rc, rc + FOX_ROWS, n_keys) for h in range(hp) for r0, r1, n_keys in parts(i, j)
                for rc in range(r0, r1, FOX_ROWS)]
        for h, r0, r1, n_keys in work:
            s = s_ref[slot, h, r0:r1, :n_keys]
            m_cur = jnp.max(s, axis=1, keepdims=True)
            if j == 0:
                m_next = jnp.broadcast_to(m_cur, (r1 - r0, LANES))
            else:
                m_prev = m_ref[h, r0:r1, :]
                m_next = jnp.maximum(m_prev, m_cur)
                alpha = jnp.exp2(m_prev - m_next)
            m_ref[h, r0:r1, :] = m_next
            p = jnp.exp2(s - jnp.tile(m_next, (1, n_keys // LANES)))
            l_cur = jnp.sum(p, axis=1, keepdims=True)
            pv = jnp.dot(p.astype(jnp.bfloat16), v_ref[h, j * t:j * t + n_keys, :],
                         preferred_element_type=jnp.float32)
            if j == 0:
                l_new = jnp.broadcast_to(l_cur, (r1 - r0, LANES))
                acc = pv
            else:
                l_new = l_cur + alpha * l_ref[h, r0:r1, :]
                acc = alpha * acc_ref[h, r0:r1, :] + pv
            if j == i:
                o_ref[h, i * t + r0:i * t + r1, :] = (acc / l_new).astype(o_ref.dtype)
            else:
                l_ref[h, r0:r1, :] = l_new
                acc_ref[h, r0:r1, :] = acc

    scores(0, 0)
    for n in range(len(steps)):
        if n + 1 < len(steps):
            scores(n + 1, (n + 1) % 2)
        finish(n, n % 2)


def forgetting_attention(proj, qx, kx, *, batch, seq):
    heads = C_HEADS
    hp = HEADS_PER_STEP
    assert proj.shape[0] == 4 * heads and seq % FOX_T == 0
    blk = (hp, seq, LANES)
    nhb = heads // hp
    return pl.pallas_call(
        _fox_body,
        grid=(batch, nhb),
        in_specs=[pl.BlockSpec(blk, lambda b, h: (h, b, 0)),
                  pl.BlockSpec((seq, LANES), lambda b, h: (b, 0)),
                  pl.BlockSpec(blk, lambda b, h: (nhb + h, b, 0)),
                  pl.BlockSpec((seq, LANES), lambda b, h: (b, 0)),
                  pl.BlockSpec(blk, lambda b, h: (2 * nhb + h, b, 0))],
        out_specs=pl.BlockSpec(blk, lambda b, h: (h, b, 0)),
        out_shape=jax.ShapeDtypeStruct((heads, batch * seq, LANES), jnp.bfloat16),
        scratch_shapes=[pltpu.VMEM((2, hp, FOX_T, FOX_T), jnp.float32),
                        pltpu.VMEM((hp, FOX_T, LANES), jnp.float32),
                        pltpu.VMEM((hp, FOX_T, LANES), jnp.float32),
                        pltpu.VMEM((hp, FOX_T, LANES), jnp.float32)],
        compiler_params=_params("parallel", "parallel"),
        name="fox_attn",
    )(proj, qx, proj, kx, proj)


def _q_column_scale(n, width, dh):
    return jnp.where(jnp.arange(n) < width, LOG2E * dh ** -0.5, 1.0).astype(jnp.float32)


def _split_cast_body(wt_ref, wt_tail_ref, scale_ref, main_ref, tail_ref):
    main_ref[...] = (wt_ref[0].T * scale_ref[...]).astype(main_ref.dtype)

    @pl.when(pl.program_id(0) == 0)
    def _():
        n_tail = wt_tail_ref.shape[1]
        tail_ref[...] = jnp.zeros(tail_ref.shape, tail_ref.dtype)
        tail_ref[:, :n_tail] = wt_tail_ref[0].T.astype(tail_ref.dtype)


def split_cast_weight(w_stack, j, n_main, col_scale, *, tn=256):
    _, d, n = w_stack.shape
    n_tail = n - n_main
    assert n_main % tn == 0 and n_main % n_tail == 0 and 0 < n_tail <= LANES and n_tail % 8 == 0
    wt = jnp.swapaxes(w_stack, 1, 2)
    return pl.pallas_call(
        _split_cast_body,
        grid=(n_main // tn,),
        in_specs=[pl.BlockSpec((1, tn, d), lambda i: (j, i, 0)),
                  pl.BlockSpec((1, n_tail, d), lambda i: (j, n_main // n_tail, 0)),
                  pl.BlockSpec((1, tn), lambda i: (0, i))],
        out_specs=[pl.BlockSpec((d, tn), lambda i: (0, i)),
                   pl.BlockSpec((d, LANES), lambda i: (0, 0))],
        out_shape=[jax.ShapeDtypeStruct((d, n_main), jnp.bfloat16),
                   jax.ShapeDtypeStruct((d, LANES), jnp.bfloat16)],
        compiler_params=_params("arbitrary"),
        name="split_cast",
    )(wt, wt, col_scale.reshape(1, n_main))


def _input_projection(xs, h, gamma, w, w2=None, *, tn=2048):
    if h is None:
        return norm_proj(xs, gamma, w, w2)
    return proj_slabs(h, w, w2, tn=tn)


def kernel(x, norm_pre, norm_post, a_w_in, a_rel_bias, a_w_out, b_w_in, b_conv_w, b_conv_b,
           b_gate_a_w, b_gate_a_b, b_gate_x_w, b_gate_x_b, b_lambda, b_w_out,
           c_w_in, c_f_bias, c_w_out):
    batch, seq, d = x.shape
    depth = norm_pre.shape[0]
    bf = lambda w: w.astype(jnp.bfloat16)
    xs = x.reshape(batch * seq, d)
    h = None
    for i in range(depth):
        m, j = i % 3, i // 3
        gamma_next = norm_pre[i + 1] if i + 1 < depth else None
        if m == 0:
            e = a_w_out.shape[1]
            w_in = bf(a_w_in[j] * _q_column_scale(4 * e, e, e // A_HEADS))
            proj = _input_projection(xs, h, norm_pre[i], w_in)
            o = chunk_attention(proj, a_rel_bias[j], batch=batch, seq=seq)
            res = out_proj(o, proj, 3, bf(a_w_out[j]), xs, norm_post[i], gamma_next)
        elif m == 1:
            proj = _input_projection(xs, h, norm_pre[i], bf(b_w_in[j]), tn=b_w_in.shape[2] // 4)
            hs = rg_lru(proj, b_conv_w[j], b_conv_b[j], b_gate_a_w[j], b_gate_a_b[j],
                        b_gate_x_w[j], b_gate_x_b[j], b_lambda[j], batch=batch, seq=seq)
            res = out_proj(hs, proj, 1, bf(b_w_out[j]), xs, norm_post[i], gamma_next)
        else:
            e = c_w_out.shape[1]
            n_f = c_w_in.shape[2] - 4 * e
            w_main, w_f = split_cast_weight(c_w_in, j, 4 * e, _q_column_scale(4 * e, e, e // C_HEADS))
            f_bias = jnp.pad(c_f_bias[j].astype(jnp.float32), (0, LANES - n_f)).reshape(1, LANES)
            proj, f_logit = _input_projection(xs, h, norm_pre[i], w_main, w_f)
            qx, kx = forget_gate_slabs(f_logit, f_bias, batch=batch, seq=seq, heads=n_f)
            o = forgetting_attention(proj, qx, kx, batch=batch, seq=seq)
            res = out_proj(o, proj, 3, bf(c_w_out[j]), xs, norm_post[i], gamma_next)
        xs, h = res if gamma_next is not None else (res, None)
    return xs.reshape(batch, seq, d)
```

```python
import functools

import jax
import jax.numpy as jnp
import numpy as np
from jax import lax
from jax.experimental import pallas as pl
from jax.experimental.pallas import tpu as pltpu

RMS_EPS = 1e-6
NEG_INF = -1e30
F32_TINY = float(np.finfo(np.float32).tiny)
LOG2E = float(np.log2(np.e))
LANES = 128
MXU_WIDTH = 256

A_HEADS = 16
A_CHUNK = 64
A_LEFT_CHUNKS = 8
A_REL_CLIP = 256
A_HEADS_PER_STEP = 4
A_QTILE = 2 * A_CHUNK
A_LEFT = A_LEFT_CHUNKS * A_CHUNK
A_KTILE = A_LEFT + A_QTILE

RG_BLOCKS = 16
RG_GROUP_BLOCKS = 4
CONV_WIDTH = 4
RG_C = 8.0
LRU_TS = 32
LRU_GATE_ROWS = 64
LRU_SUB_STEPS = 16

C_HEADS = 16
FOX_T = 512
FOX_ROWS = 128
N_SPLIT = 3

HEADS_PER_STEP = 4
OUT_PROJ_ROWS = 256
VMEM_LIMIT_BYTES = 56 * 1024 * 1024


def _params(*sem):
    return pltpu.CompilerParams(dimension_semantics=sem, vmem_limit_bytes=VMEM_LIMIT_BYTES)


def _norm_proj_body(x_ref, gam_ref, w_ref, *rest, has_extra):
    if has_extra:
        w2_ref, o_ref, o2_ref, xn_ref = rest
    else:
        o_ref, xn_ref = rest

    @pl.when(pl.program_id(1) == 0)
    def _():
        x = x_ref[...]
        ms = jnp.mean(x * x, axis=-1, keepdims=True)
        xn_ref[...] = (x * lax.rsqrt(ms + RMS_EPS) * gam_ref[...]).astype(jnp.bfloat16)
        if has_extra:
            o2_ref[...] = jnp.dot(xn_ref[...], w2_ref[...], preferred_element_type=jnp.float32)

    res = jnp.dot(xn_ref[...], w_ref[...], preferred_element_type=jnp.float32)
    for c in range(o_ref.shape[0]):
        o_ref[c] = res[:, c * LANES:(c + 1) * LANES].astype(o_ref.dtype)


def norm_proj(x, gamma, w, w2=None, *, tm=1024, tn=2048):
    m, d = x.shape
    n = w.shape[1]
    tm, tn = min(tm, m), min(tn, n)
    assert m % tm == 0 and n % tn == 0 and tn % LANES == 0
    in_specs = [pl.BlockSpec((tm, d), lambda i, j: (i, 0)),
                pl.BlockSpec((1, d), lambda i, j: (0, 0)),
                pl.BlockSpec((d, tn), lambda i, j: (0, j))]
    out_shape = [jax.ShapeDtypeStruct((n // LANES, m, LANES), jnp.bfloat16)]
    out_specs = [pl.BlockSpec((tn // LANES, tm, LANES), lambda i, j: (j, i, 0))]
    args = [x, gamma.reshape(1, d), w]
    if w2 is not None:
        in_specs.append(pl.BlockSpec((d, w2.shape[1]), lambda i, j: (0, 0)))
        out_shape.append(jax.ShapeDtypeStruct((m, w2.shape[1]), jnp.float32))
        out_specs.append(pl.BlockSpec((tm, w2.shape[1]), lambda i, j: (i, 0)))
        args.append(w2)
    out = pl.pallas_call(
        functools.partial(_norm_proj_body, has_extra=w2 is not None),
        grid=(m // tm, n // tn),
        in_specs=in_specs, out_specs=out_specs, out_shape=out_shape,
        scratch_shapes=[pltpu.VMEM((tm, d), jnp.bfloat16)],
        compiler_params=_params("parallel", "arbitrary"),
        name="norm_proj",
    )(*args)
    return out if w2 is not None else out[0]


def _proj_body(h_ref, w_ref, *rest, has_extra):
    if has_extra:
        w2_ref, o_ref, o2_ref = rest

        @pl.when(pl.program_id(1) == 0)
        def _():
            o2_ref[...] = jnp.dot(h_ref[...], w2_ref[...], preferred_element_type=jnp.float32)
    else:
        (o_ref,) = rest

    res = jnp.dot(h_ref[...], w_ref[...], preferred_element_type=jnp.float32)
    for c in range(o_ref.shape[0]):
        o_ref[c] = res[:, c * LANES:(c + 1) * LANES].astype(o_ref.dtype)


def proj_slabs(h, w, w2=None, *, tm=1024, tn=2048):
    m, d = h.shape
    n = w.shape[1]
    tm, tn = min(tm, m), min(tn, n)
    assert m % tm == 0 and n % tn == 0 and tn % LANES == 0
    in_specs = [pl.BlockSpec((tm, d), lambda i, j: (i, 0)),
                pl.BlockSpec((d, tn), lambda i, j: (0, j))]
    out_shape = [jax.ShapeDtypeStruct((n // LANES, m, LANES), jnp.bfloat16)]
    out_specs = [pl.BlockSpec((tn // LANES, tm, LANES), lambda i, j: (j, i, 0))]
    args = [h, w]
    if w2 is not None:
        in_specs.append(pl.BlockSpec((d, w2.shape[1]), lambda i, j: (0, 0)))
        out_shape.append(jax.ShapeDtypeStruct((m, w2.shape[1]), jnp.float32))
        out_specs.append(pl.BlockSpec((tm, w2.shape[1]), lambda i, j: (i, 0)))
        args.append(w2)
    out = pl.pallas_call(
        functools.partial(_proj_body, has_extra=w2 is not None),
        grid=(m // tm, n // tn),
        in_specs=in_specs, out_specs=out_specs, out_shape=out_shape,
        compiler_params=_params("parallel", "arbitrary"),
        name="proj",
    )(*args)
    return out if w2 is not None else out[0]


def _out_proj_body(a_ref, g_ref, w_ref, x_ref, gam_ref, *rest, has_next):
    if has_next:
        gam_next_ref, o_ref, h_next_ref = rest
    else:
        (o_ref,) = rest
    ns, tm, _ = a_ref.shape
    n_chunks = tm // OUT_PROJ_ROWS

    def gated(c):
        rs = slice(c * OUT_PROJ_ROWS, (c + 1) * OUT_PROJ_ROWS)
        g = jnp.concatenate([g_ref[s, rs, :] for s in range(ns)], axis=1).astype(jnp.float32)
        a = jnp.concatenate([a_ref[s, rs, :] for s in range(ns)], axis=1).astype(jnp.float32)
        half_g = 0.5 * g
        return (a * (half_g * (1.0 + jnp.tanh(half_g)))).astype(jnp.bfloat16)

    def finish(c, y):
        rs = slice(c * OUT_PROJ_ROWS, (c + 1) * OUT_PROJ_ROWS)
        ms = jnp.mean(y * y, axis=-1, keepdims=True)
        x_new = x_ref[rs, :] + y * lax.rsqrt(ms + RMS_EPS) * gam_ref[...]
        o_ref[rs, :] = x_new
        if has_next:
            ms_new = jnp.mean(x_new * x_new, axis=-1, keepdims=True)
            h_next_ref[rs, :] = (x_new * lax.rsqrt(ms_new + RMS_EPS)
                                 * gam_next_ref[...]).astype(h_next_ref.dtype)

    a_next = gated(0)
    y_prev = None
    for c in range(n_chunks):
        a_cur = a_next
        if c + 1 < n_chunks:
            a_next = gated(c + 1)
        y = jnp.dot(a_cur, w_ref[...], preferred_element_type=jnp.float32)
        if y_prev is not None:
            finish(c - 1, y_prev)
        y_prev = y
    finish(n_chunks - 1, y_prev)


def out_proj(a, proj, g_block, w, x, gamma, gamma_next=None, *, tm=512):
    ns, m, _ = a.shape
    e, d = w.shape
    assert ns * LANES == e
    tm = min(tm, m)
    assert m % tm == 0 and tm % OUT_PROJ_ROWS == 0
    row_spec = pl.BlockSpec((tm, d), lambda i: (i, 0))
    vec_spec = pl.BlockSpec((1, d), lambda i: (0, 0))
    in_specs = [pl.BlockSpec((ns, tm, LANES), lambda i: (0, i, 0)),
                pl.BlockSpec((ns, tm, LANES), lambda i: (g_block, i, 0)),
                pl.BlockSpec((e, d), lambda i: (0, 0), pipeline_mode=pl.Buffered(1)),
                row_spec, vec_spec]
    out_shape = [jax.ShapeDtypeStruct((m, d), jnp.float32)]
    out_specs = [row_spec]
    args = [a, proj, w, x, gamma.reshape(1, d)]
    if gamma_next is not None:
        in_specs.append(vec_spec)
        out_shape.append(jax.ShapeDtypeStruct((m, d), jnp.bfloat16))
        out_specs.append(row_spec)
        args.append(gamma_next.reshape(1, d))
    out = pl.pallas_call(
        functools.partial(_out_proj_body, has_next=gamma_next is not None),
        grid=(m // tm,),
        in_specs=in_specs, out_specs=out_specs, out_shape=out_shape,
        compiler_params=_params("parallel"),
        name="out_proj",
    )(*args)
    return out if gamma_next is not None else out[0]


def _chunk_attn_body(q_ref, k_ref, v_ref, bias_ref, o_ref, s_ref, p_ref, l_ref):
    hp, seq, _ = q_ref.shape
    n_tiles = seq // A_QTILE
    n_cut = A_LEFT // A_QTILE

    def window(t):
        q0 = t * A_QTILE
        k0 = max(q0 - A_LEFT, 0)
        n_keys = q0 + A_QTILE - k0
        return slice(q0, q0 + A_QTILE), slice(k0, k0 + n_keys), n_keys

    def scores(t, slot):
        q_rows, k_rows, n_keys = window(t)
        for h in range(hp):
            s = lax.dot_general(q_ref[h, q_rows, :], k_ref[h, k_rows, :], (((1,), (1,)), ((), ())),
                                preferred_element_type=jnp.float32)
            s_ref[slot, h, :, :n_keys] = s + bias_ref[h, min(t, n_cut), :, :n_keys]

    def softmax(t, slot):
        _, _, n_keys = window(t)
        for h in range(hp):
            s = s_ref[slot, h, :, :n_keys]
            p = jnp.exp2(s - jnp.max(s, axis=-1, keepdims=True))
            l_ref[slot, h] = jnp.broadcast_to(jnp.sum(p, axis=-1, keepdims=True), (A_QTILE, LANES))
            p_ref[slot, h, :, :n_keys] = p.astype(p_ref.dtype)

    def weighted_values(t, slot):
        q_rows, k_rows, n_keys = window(t)
        for h in range(hp):
            o = jnp.dot(p_ref[slot, h, :, :n_keys], v_ref[h, k_rows, :],
                        preferred_element_type=jnp.float32)
            o_ref[h, q_rows, :] = (o / l_ref[slot, h]).astype(o_ref.dtype)

    scores(0, 0)
    scores(1, 1)
    softmax(0, 0)
    for t in range(n_tiles):
        if t + 2 < n_tiles:
            scores(t + 2, t % 2)
        if t + 1 < n_tiles:
            softmax(t + 1, (t + 1) % 2)
        weighted_values(t, t % 2)


def _bias_tables_body(u_ref, o_ref):
    n_tab, qt, kt = o_ref.shape[1:]
    qi = lax.broadcasted_iota(jnp.int32, (qt, kt), 0)
    j = lax.broadcasted_iota(jnp.int32, (qt, kt), 1)
    chunk_start = (qi // A_CHUNK) * A_CHUNK
    for i in range(n_tab):
        diag = jnp.broadcast_to(u_ref[0, i:i + 1, :], (qt, u_ref.shape[2]))
        table = pltpu.roll(diag, 0, 1, stride=1, stride_axis=0)[:, :kt]
        band_start = chunk_start + (i * qt - A_LEFT)
        visible = (j >= band_start) & (j < band_start + (A_LEFT_CHUNKS + 1) * A_CHUNK)
        o_ref[0, i] = jnp.where(visible, LOG2E * table, NEG_INF)


def _band_bias_tables(rel_bias):
    heads = rel_bias.shape[0]
    n_tab = A_LEFT // A_QTILE + 1
    n_diag = pl.cdiv(A_QTILE + A_KTILE - 1, LANES) * LANES
    m = np.arange(n_diag)
    key_minus_query = np.where(m < A_KTILE, m, m - n_diag)
    offsets = A_QTILE * np.arange(n_tab)[:, None]
    idx = np.clip(offsets - key_minus_query[None, :], -A_REL_CLIP, A_REL_CLIP) + A_REL_CLIP
    diags = rel_bias.astype(jnp.float32)[:, idx]
    return pl.pallas_call(
        _bias_tables_body,
        grid=(heads,),
        in_specs=[pl.BlockSpec((1, n_tab, n_diag), lambda h: (h, 0, 0))],
        out_specs=pl.BlockSpec((1, n_tab, A_QTILE, A_KTILE), lambda h: (h, 0, 0, 0)),
        out_shape=jax.ShapeDtypeStruct((heads, n_tab, A_QTILE, A_KTILE), jnp.float32),
        compiler_params=_params("parallel"),
        name="bias_tables",
    )(diags)


def chunk_attention(proj, rel_bias, *, batch, seq):
    heads = A_HEADS
    hp = A_HEADS_PER_STEP
    assert proj.shape[0] == 4 * heads and seq % (2 * A_QTILE) == 0 and seq >= A_KTILE
    tables = _band_bias_tables(rel_bias)
    blk = (hp, seq, LANES)
    nhb = heads // hp
    return pl.pallas_call(
        _chunk_attn_body,
        grid=(nhb, batch),
        in_specs=[pl.BlockSpec(blk, lambda h, b: (h, b, 0)),
                  pl.BlockSpec(blk, lambda h, b: (nhb + h, b, 0)),
                  pl.BlockSpec(blk, lambda h, b: (2 * nhb + h, b, 0)),
                  pl.BlockSpec((hp,) + tables.shape[1:], lambda h, b: (h, 0, 0, 0))],
        out_specs=pl.BlockSpec(blk, lambda h, b: (h, b, 0)),
        out_shape=jax.ShapeDtypeStruct((heads, batch * seq, LANES), jnp.bfloat16),
        scratch_shapes=[pltpu.VMEM((2, hp, A_QTILE, A_KTILE), jnp.float32),
                        pltpu.VMEM((2, hp, A_QTILE, A_KTILE), jnp.bfloat16),
                        pltpu.VMEM((2, hp, A_QTILE, LANES), jnp.float32)],
        compiler_params=_params("parallel", "parallel"),
        name="chunk_attn",
    )(proj, proj, proj, tables)


def _lru_body(xr_ref, cw_ref, cb_ref, wg_ref, ba_ref, bx_ref, lam_ref, o_ref,
              xbuf_ref, y_ref, pre_ref, a_ref, u_ref, hs_ref, h_ref, *, groups, gw, pieces):
    ns, nb, ts, _ = o_ref.shape
    rows = ts * nb
    hist = CONV_WIDTH * nb
    sub = LRU_SUB_STEPS
    pr = sub * nb

    @pl.when(pl.program_id(0) == 0)
    def _():
        xbuf_ref[0:hist, :] = jnp.zeros((hist, xbuf_ref.shape[1]), jnp.float32)
        h_ref[...] = jnp.zeros(h_ref.shape, jnp.float32)

    r_idx = lax.broadcasted_iota(jnp.int32, (pr, pr), 0)
    k_idx = lax.broadcasted_iota(jnp.int32, (pr, pr), 1)
    perm = jnp.where(k_idx == (r_idx % nb) * sub + r_idx // nb, 1.0, 0.0).astype(jnp.bfloat16)

    for part in range(ts // sub):
        steps = slice(part * sub, (part + 1) * sub)
        x_bm = jnp.concatenate(
            [jnp.concatenate([xr_ref[c, b, steps, :] for b in range(nb)], axis=0) for c in range(ns)],
            axis=1)
        xbuf_ref[hist + part * pr:hist + (part + 1) * pr, :] = jnp.dot(
            perm, x_bm, preferred_element_type=jnp.float32)

    def conv_and_gate_matmul(gi, slot):
        cs = slice(gi * gw, (gi + 1) * gw)
        y = cb_ref[:, cs] + cw_ref[CONV_WIDTH - 1:CONV_WIDTH, cs] * xbuf_ref[hist:hist + rows, cs]
        for tap in range(CONV_WIDTH - 1):
            off = hist - (CONV_WIDTH - 1 - tap) * nb
            y = y + cw_ref[tap:tap + 1, cs] * xbuf_ref[off:off + rows, cs]
        y_ref[slot] = y
        y_bf = y.astype(jnp.bfloat16)
        for p, (c0, c1, k0, k1) in enumerate(pieces):
            w = wg_ref[gi, p, :k1 - k0, :2 * (c1 - c0)]
            pre = jnp.dot(y_bf[:, k0:k1], w, preferred_element_type=jnp.float32)
            pre_ref[slot, :, c0:c1] = pre[:, :c1 - c0]
            pre_ref[slot, :, gw + c0:gw + c1] = pre[:, c1 - c0:]

    def gates(gi, slot):
        cs = slice(gi * gw, (gi + 1) * gw)
        half_ba = 0.5 * ba_ref[:, cs]
        half_bx = 0.5 * bx_ref[:, cs]
        lam = lam_ref[:, cs]
        softplus_neg_lam = jnp.maximum(-lam, 0.0) + jnp.log1p(jnp.exp(-jnp.abs(lam)))
        half_coef = (-0.5 * RG_C) * softplus_neg_lam
        for k in range(rows // LRU_GATE_ROWS):
            rk = slice(k * LRU_GATE_ROWS, (k + 1) * LRU_GATE_ROWS)
            tanh_r = jnp.tanh(pre_ref[slot, rk, :gw] + half_ba)
            tanh_i = jnp.tanh(pre_ref[slot, rk, gw:] + half_bx)
            log_a = half_coef * tanh_r + half_coef
            a = jnp.exp(log_a)
            quarter = (-0.25 * jnp.tanh(log_a)) * (1.0 + a * a)
            half_mult = quarter * lax.rsqrt(jnp.maximum(quarter, F32_TINY))
            a_ref[rk, cs] = a
            u_ref[rk, cs] = half_mult * ((tanh_i + 1.0) * y_ref[slot, rk, :])

    conv_and_gate_matmul(0, 0)
    for gi in range(groups):
        if gi + 1 < groups:
            conv_and_gate_matmul(gi + 1, (gi + 1) % 2)
        gates(gi, gi % 2)

    xbuf_ref[0:hist, :] = xbuf_ref[rows:rows + hist, :]

    def time_step(t, h):
        rs = pl.ds(pl.multiple_of(t * nb, nb), nb)
        h = a_ref[rs, :] * h + u_ref[rs, :]
        hs_ref[rs, :] = h.astype(hs_ref.dtype)
        return h

    h_ref[...] = lax.fori_loop(0, ts, time_step, h_ref[...])

    for part in range(ts // sub):
        steps = slice(part * sub, (part + 1) * sub)
        hs_bm = jnp.dot(perm, hs_ref[part * pr:(part + 1) * pr, :],
                        preferred_element_type=jnp.float32).astype(o_ref.dtype)
        for c in range(ns):
            for b in range(nb):
                o_ref[c, b, steps, :] = hs_bm[b * sub:(b + 1) * sub, c * LANES:(c + 1) * LANES]


def _gate_pieces(gw, bs):
    pieces = []
    for c0 in range(0, gw, MXU_WIDTH):
        c1 = min(c0 + MXU_WIDTH, gw)
        k0 = (c0 // bs) * bs // LANES * LANES
        k1 = min(pl.cdiv(((c1 - 1) // bs + 1) * bs, LANES) * LANES, gw)
        pieces.append((c0, c1, k0, k1))
    return pieces


def _gate_group_weights(w_a, w_x):
    nb, bs, _ = w_a.shape
    groups = nb // RG_GROUP_BLOCKS
    gw = RG_GROUP_BLOCKS * bs
    pieces = _gate_pieces(gw, bs)
    k_max = max(k1 - k0 for _, _, k0, k1 in pieces)
    n_max = max(2 * (c1 - c0) for c0, c1, _, _ in pieces)

    def block_diag(w):
        w = (0.5 * w).astype(jnp.bfloat16).reshape(groups, RG_GROUP_BLOCKS, bs, bs)
        rows = [jnp.pad(w[:, n], ((0, 0), (0, 0), (n * bs, gw - (n + 1) * bs)))
                for n in range(RG_GROUP_BLOCKS)]
        return jnp.concatenate(rows, axis=1)

    full_a, full_x = block_diag(w_a), block_diag(w_x)
    out = []
    for c0, c1, k0, k1 in pieces:
        piece = jnp.concatenate([full_a[:, k0:k1, c0:c1], full_x[:, k0:k1, c0:c1]], axis=2)
        out.append(jnp.pad(piece, ((0, 0), (0, k_max - (k1 - k0)), (0, n_max - 2 * (c1 - c0)))))
    return jnp.stack(out, axis=1)


def rg_lru(proj, conv_w, conv_b, w_a, b_a, w_x, b_x, lam, *, batch, seq):
    ns = proj.shape[0] // 2
    width = ns * LANES
    gw = RG_GROUP_BLOCKS * (width // RG_BLOCKS)
    groups = width // gw
    ts = min(LRU_TS, seq)
    rows = ts * batch
    assert seq % ts == 0 and gw % LANES == 0 and ts >= CONV_WIDTH and ts % LRU_SUB_STEPS == 0
    assert batch == LRU_SUB_STEPS and rows % LRU_GATE_ROWS == 0
    wg = _gate_group_weights(w_a, w_x)
    vec = lambda v: v.reshape(1, width).astype(jnp.float32)
    const2 = lambda s: (0, 0)
    blk = pl.BlockSpec((ns, batch, ts, LANES), lambda s: (0, 0, s, 0))
    out = pl.pallas_call(
        functools.partial(_lru_body, groups=groups, gw=gw, pieces=_gate_pieces(gw, width // RG_BLOCKS)),
        grid=(seq // ts,),
        in_specs=[blk,
                  pl.BlockSpec((CONV_WIDTH, width), const2),
                  pl.BlockSpec((1, width), const2),
                  pl.BlockSpec(wg.shape, lambda s: (0, 0, 0, 0), pipeline_mode=pl.Buffered(1)),
                  pl.BlockSpec((1, width), const2),
                  pl.BlockSpec((1, width), const2),
                  pl.BlockSpec((1, width), const2)],
        out_specs=blk,
        out_shape=jax.ShapeDtypeStruct((ns, batch, seq, LANES), jnp.bfloat16),
        scratch_shapes=[pltpu.VMEM((rows + CONV_WIDTH * batch, width), jnp.float32),
                        pltpu.VMEM((2, rows, gw), jnp.float32),
                        pltpu.VMEM((2, rows, 2 * gw), jnp.float32),
                        pltpu.VMEM((rows, width), jnp.float32),
                        pltpu.VMEM((rows, width), jnp.float32),
                        pltpu.VMEM((rows, width), jnp.bfloat16),
                        pltpu.VMEM((batch, width), jnp.float32)],
        compiler_params=_params("arbitrary"),
        name="rg_lru",
    )(proj.reshape(2 * ns, batch, seq, LANES), conv_w.astype(jnp.float32), vec(conv_b), wg,
      vec(b_a), vec(b_x), vec(lam))
    return out.reshape(ns, batch * seq, LANES)


FOX_PIECE_LANES = 16 * N_SPLIT


def _split3(x):
    hi = x.astype(jnp.bfloat16).astype(jnp.float32)
    r1 = x - hi
    mid = r1.astype(jnp.bfloat16).astype(jnp.float32)
    lo = (r1 - mid).astype(jnp.bfloat16).astype(jnp.float32)
    return hi, mid, lo


def _forget_gate_body(f_ref, fb_ref, qx_ref, kx_ref, *, heads):
    z = f_ref[0] + fb_ref[...]
    c = jnp.minimum(z, 0.0) - jnp.log1p(jnp.exp(-jnp.abs(z)))
    n = c.shape[0]
    row = lax.broadcasted_iota(jnp.int32, c.shape, 0)
    d = 1
    while d < n:
        c = c + jnp.where(row >= d, pltpu.roll(c, d, 0), 0.0)
        d *= 2
    lane = lax.broadcasted_iota(jnp.int32, c.shape, 1)
    hi, mid, lo = _split3(LOG2E * c)
    pieces = jnp.where(lane < heads, hi,
                       jnp.where(lane < 2 * heads, pltpu.roll(mid, heads, 1),
                                 jnp.where(lane < 3 * heads, pltpu.roll(lo, 2 * heads, 1), 0.0)))
    qx_ref[...] = pieces.astype(qx_ref.dtype)
    kx_ref[...] = pltpu.roll(-pieces, FOX_PIECE_LANES, 1).astype(kx_ref.dtype)


def forget_gate_slabs(f_logit, f_bias_padded, *, batch, seq, heads):
    assert heads * N_SPLIT == FOX_PIECE_LANES and 2 * FOX_PIECE_LANES <= LANES
    f3 = f_logit.reshape(batch, seq, LANES)
    shape = jax.ShapeDtypeStruct((batch * seq, LANES), jnp.bfloat16)
    spec = pl.BlockSpec((seq, LANES), lambda b: (b, 0))
    return pl.pallas_call(
        functools.partial(_forget_gate_body, heads=heads),
        grid=(batch,),
        in_specs=[pl.BlockSpec((1, seq, LANES), lambda b: (b, 0, 0)),
                  pl.BlockSpec((1, LANES), lambda b: (0, 0))],
        out_specs=[spec, spec],
        out_shape=[shape, shape],
        compiler_params=_params("parallel"),
        name="forget_gate",
    )(f3, f_bias_padded)


def _fox_body(q_ref, qx_ref, k_ref, kx_ref, v_ref, o_ref, s_ref, m_ref, l_ref, acc_ref):
    hp, seq, dh = q_ref.shape
    t = FOX_T
    heads = FOX_PIECE_LANES // N_SPLIT
    steps = [(i, j) for i in range(seq // t) for j in range(i + 1)]
    lane = lax.broadcasted_iota(jnp.int32, (1, LANES), 1)

    def augment(x_ref, px_ref, h, r, first_lane):
        rel = lane - (first_lane + pl.program_id(1) * hp + h)
        select = (rel == 0) | (rel == heads) | (rel == 2 * heads)
        extra = jnp.where(select, jnp.ones((), px_ref.dtype), px_ref[r, :])
        return jnp.concatenate([x_ref[h, r, :], extra], axis=1)

    def parts(i, j):
        if i != j:
            return [(0, t, t)]
        return [(0, t // 2, t // 2), (t // 2, t, t)]

    def scores(n, slot):
        i, j = steps[n]
        for h in range(hp):
            for r0, r1, n_keys in parts(i, j):
                q_aug = augment(q_ref, qx_ref, h, slice(i * t + r0, i * t + r1), FOX_PIECE_LANES)
                k_aug = augment(k_ref, kx_ref, h, slice(j * t, j * t + n_keys), 0)
                s = lax.dot_general(q_aug, k_aug, (((1,), (1,)), ((), ())),
                                    preferred_element_type=jnp.float32)
                if i == j:
                    r = lax.broadcasted_iota(jnp.int32, s.shape, 0) + r0
                    c = lax.broadcasted_iota(jnp.int32, s.shape, 1)
                    s = jnp.where(c <= r, s, NEG_INF)
                s_ref[slot, h, r0:r1, :n_keys] = s

    def finish(n, slot):
        i, j = steps[n]
        work = [(h, rc, rc + FOX_ROWS, n_keys) for h in range(hp) for r0, r1, n_keys in parts(i, j)
                for rc in range(r0, r1, FOX_ROWS)]
        for h, r0, r1, n_keys in work:
            s = s_ref[slot, h, r0:r1, :n_keys]
            m_cur = jnp.max(s, axis=1, keepdims=True)
            if j == 0:
                m_next = jnp.broadcast_to(m_cur, (r1 - r0, LANES))
            else:
                m_prev = m_ref[h, r0:r1, :]
                m_next = jnp.maximum(m_prev, m_cur)
                alpha = jnp.exp2(m_prev - m_next)
            m_ref[h, r0:r1, :] = m_next
            p = jnp.exp2(s - jnp.tile(m_next, (1, n_keys // LANES)))
            l_cur = jnp.sum(p, axis=1, keepdims=True)
            pv = jnp.dot(p.astype(jnp.bfloat16), v_ref[h, j * t:j * t + n_keys, :],
                         preferred_element_type=jnp.float32)
            if j == 0:
                l_new = jnp.broadcast_to(l_cur, (r1 - r0, LANES))
                acc = pv
            else:
                l_new = l_cur + alpha * l_ref[h, r0:r1, :]
                acc = alpha * acc_ref[h, r0:r1, :] + pv
            if j == i:
                o_ref[h, i * t + r0:i * t + r1, :] = (acc / l_new).astype(o_ref.dtype)
            else:
                l_ref[h, r0:r1, :] = l_new
                acc_ref[h, r0:r1, :] = acc

    scores(0, 0)
    for n in range(len(steps)):
        if n + 1 < len(steps):
            scores(n + 1, (n + 1) % 2)
        finish(n, n % 2)


def forgetting_attention(proj, qx, kx, *, batch, seq):
    heads = C_HEADS
    hp = HEADS_PER_STEP
    assert proj.shape[0] == 4 * heads and seq % FOX_T == 0
    blk = (hp, seq, LANES)
    nhb = heads // hp
    return pl.pallas_call(
        _fox_body,
        grid=(batch, nhb),
        in_specs=[pl.BlockSpec(blk, lambda b, h: (h, b, 0)),
                  pl.BlockSpec((seq, LANES), lambda b, h: (b, 0)),
                  pl.BlockSpec(blk, lambda b, h: (nhb + h, b, 0)),
                  pl.BlockSpec((seq, LANES), lambda b, h: (b, 0)),
                  pl.BlockSpec(blk, lambda b, h: (2 * nhb + h, b, 0))],
        out_specs=pl.BlockSpec(blk, lambda b, h: (h, b, 0)),
        out_shape=jax.ShapeDtypeStruct((heads, batch * seq, LANES), jnp.bfloat16),
        scratch_shapes=[pltpu.VMEM((2, hp, FOX_T, FOX_T), jnp.float32),
                        pltpu.VMEM((hp, FOX_T, LANES), jnp.float32),
                        pltpu.VMEM((hp, FOX_T, LANES), jnp.float32),
                        pltpu.VMEM((hp, FOX_T, LANES), jnp.float32)],
        compiler_params=_params("parallel", "parallel"),
        name="fox_attn",
    )(proj, qx, proj, kx, proj)


def _q_column_scale(n, width, dh):
    return jnp.where(jnp.arange(n) < width, LOG2E * dh ** -0.5, 1.0).astype(jnp.float32)


def _split_cast_body(wt_ref, wt_tail_ref, scale_ref, main_ref, tail_ref):
    main_ref[...] = (wt_ref[0].T * scale_ref[...]).astype(main_ref.dtype)

    @pl.when(pl.program_id(0) == 0)
    def _():
        n_tail = wt_tail_ref.shape[1]
        tail_ref[...] = jnp.zeros(tail_ref.shape, tail_ref.dtype)
        tail_ref[:, :n_tail] = wt_tail_ref[0].T.astype(tail_ref.dtype)


def split_cast_weight(w_stack, j, n_main, col_scale, *, tn=256):
    _, d, n = w_stack.shape
    n_tail = n - n_main
    assert n_main % tn == 0 and n_main % n_tail == 0 and 0 < n_tail <= LANES and n_tail % 8 == 0
    wt = jnp.swapaxes(w_stack, 1, 2)
    return pl.pallas_call(
        _split_cast_body,
        grid=(n_main // tn,),
        in_specs=[pl.BlockSpec((1, tn, d), lambda i: (j, i, 0)),
                  pl.BlockSpec((1, n_tail, d), lambda i: (j, n_main // n_tail, 0)),
                  pl.BlockSpec((1, tn), lambda i: (0, i))],
        out_specs=[pl.BlockSpec((d, tn), lambda i: (0, i)),
                   pl.BlockSpec((d, LANES), lambda i: (0, 0))],
        out_shape=[jax.ShapeDtypeStruct((d, n_main), jnp.bfloat16),
                   jax.ShapeDtypeStruct((d, LANES), jnp.bfloat16)],
        compiler_params=_params("arbitrary"),
        name="split_cast",
    )(wt, wt, col_scale.reshape(1, n_main))


def _input_projection(xs, h, gamma, w, w2=None, *, tn=2048):
    if h is None:
        return norm_proj(xs, gamma, w, w2)
    return proj_slabs(h, w, w2, tn=tn)


def kernel(x, norm_pre, norm_post, a_w_in, a_rel_bias, a_w_out, b_w_in, b_conv_w, b_conv_b,
           b_gate_a_w, b_gate_a_b, b_gate_x_w, b_gate_x_b, b_lambda, b_w_out,
           c_w_in, c_f_bias, c_w_out):
    batch, seq, d = x.shape
    depth = norm_pre.shape[0]
    bf = lambda w: w.astype(jnp.bfloat16)
    xs = x.reshape(batch * seq, d)
    h = None
    for i in range(depth):
        m, j = i % 3, i // 3
        gamma_next = norm_pre[i + 1] if i + 1 < depth else None
        if m == 0:
            e = a_w_out.shape[1]
            w_in = bf(a_w_in[j] * _q_column_scale(4 * e, e, e // A_HEADS))
            proj = _input_projection(xs, h, norm_pre[i], w_in)
            o = chunk_attention(proj, a_rel_bias[j], batch=batch, seq=seq)
            res = out_proj(o, proj, 3, bf(a_w_out[j]), xs, norm_post[i], gamma_next)
        elif m == 1:
            proj = _input_projection(xs, h, norm_pre[i], bf(b_w_in[j]), tn=b_w_in.shape[2] // 2)
            hs = rg_lru(proj, b_conv_w[j], b_conv_b[j], b_gate_a_w[j], b_gate_a_b[j],
                        b_gate_x_w[j], b_gate_x_b[j], b_lambda[j], batch=batch, seq=seq)
            res = out_proj(hs, proj, 1, bf(b_w_out[j]), xs, norm_post[i], gamma_next)
        else:
            e = c_w_out.shape[1]
            n_f = c_w_in.shape[2] - 4 * e
            w_main, w_f = split_cast_weight(c_w_in, j, 4 * e, _q_column_scale(4 * e, e, e // C_HEADS))
            f_bias = jnp.pad(c_f_bias[j].astype(jnp.float32), (0, LANES - n_f)).reshape(1, LANES)
            proj, f_logit = _input_projection(xs, h, norm_pre[i], w_main, w_f)
            qx, kx = forget_gate_slabs(f_logit, f_bias, batch=batch, seq=seq, heads=n_f)
            o = forgetting_attention(proj, qx, kx, batch=batch, seq=seq)
            res = out_proj(o, proj, 3, bf(c_w_out[j]), xs, norm_post[i], gamma_next)
        xs, h = res if gamma_next is not None else (res, None)
    return xs.reshape(batch, seq, d)
```

```python
import functools

import jax
import jax.numpy as jnp
import numpy as np
from jax import lax
from jax.experimental import pallas as pl
from jax.experimental.pallas import tpu as pltpu

RMS_EPS = 1e-6
NEG_INF = -1e30
F32_TINY = float(np.finfo(np.float32).tiny)
LOG2E = float(np.log2(np.e))
LANES = 128
MXU_WIDTH = 256

A_HEADS = 16
A_CHUNK = 64
A_LEFT_CHUNKS = 8
A_REL_CLIP = 256
A_HEADS_PER_STEP = 4
A_QTILE = 2 * A_CHUNK
A_LEFT = A_LEFT_CHUNKS * A_CHUNK
A_KTILE = A_LEFT + A_QTILE

RG_BLOCKS = 16
RG_GROUP_BLOCKS = 4
CONV_WIDTH = 4
RG_C = 8.0
LRU_TS = 32
LRU_GATE_ROWS = 64
LRU_SUB_STEPS = 16

C_HEADS = 16
FOX_T = 512
FOX_ROWS = 128
N_SPLIT = 3

HEADS_PER_STEP = 4
OUT_PROJ_ROWS = 256
VMEM_LIMIT_BYTES = 56 * 1024 * 1024


def _params(*sem):
    return pltpu.CompilerParams(dimension_semantics=sem, vmem_limit_bytes=VMEM_LIMIT_BYTES)


def _norm_proj_body(x_ref, gam_ref, w_ref, *rest, has_extra):
    if has_extra:
        w2_ref, o_ref, o2_ref, xn_ref = rest
    else:
        o_ref, xn_ref = rest

    @pl.when(pl.program_id(1) == 0)
    def _():
        x = x_ref[...]
        ms = jnp.mean(x * x, axis=-1, keepdims=True)
        xn_ref[...] = (x * lax.rsqrt(ms + RMS_EPS) * gam_ref[...]).astype(jnp.bfloat16)
        if has_extra:
            o2_ref[...] = jnp.dot(xn_ref[...], w2_ref[...], preferred_element_type=jnp.float32)

    res = jnp.dot(xn_ref[...], w_ref[...], preferred_element_type=jnp.float32)
    for c in range(o_ref.shape[0]):
        o_ref[c] = res[:, c * LANES:(c + 1) * LANES].astype(o_ref.dtype)


def norm_proj(x, gamma, w, w2=None, *, tm=1024, tn=2048):
    m, d = x.shape
    n = w.shape[1]
    tm, tn = min(tm, m), min(tn, n)
    assert m % tm == 0 and n % tn == 0 and tn % LANES == 0
    in_specs = [pl.BlockSpec((tm, d), lambda i, j: (i, 0)),
                pl.BlockSpec((1, d), lambda i, j: (0, 0)),
                pl.BlockSpec((d, tn), lambda i, j: (0, j))]
    out_shape = [jax.ShapeDtypeStruct((n // LANES, m, LANES), jnp.bfloat16)]
    out_specs = [pl.BlockSpec((tn // LANES, tm, LANES), lambda i, j: (j, i, 0))]
    args = [x, gamma.reshape(1, d), w]
    if w2 is not None:
        in_specs.append(pl.BlockSpec((d, w2.shape[1]), lambda i, j: (0, 0)))
        out_shape.append(jax.ShapeDtypeStruct((m, w2.shape[1]), jnp.float32))
        out_specs.append(pl.BlockSpec((tm, w2.shape[1]), lambda i, j: (i, 0)))
        args.append(w2)
    out = pl.pallas_call(
        functools.partial(_norm_proj_body, has_extra=w2 is not None),
        grid=(m // tm, n // tn),
        in_specs=in_specs, out_specs=out_specs, out_shape=out_shape,
        scratch_shapes=[pltpu.VMEM((tm, d), jnp.bfloat16)],
        compiler_params=_params("parallel", "arbitrary"),
        name="norm_proj",
    )(*args)
    return out if w2 is not None else out[0]


def _proj_body(h_ref, w_ref, *rest, has_extra):
    if has_extra:
        w2_ref, o_ref, o2_ref = rest

        @pl.when(pl.program_id(1) == 0)
        def _():
            o2_ref[...] = jnp.dot(h_ref[...], w2_ref[...], preferred_element_type=jnp.float32)
    else:
        (o_ref,) = rest

    res = jnp.dot(h_ref[...], w_ref[...], preferred_element_type=jnp.float32)
    for c in range(o_ref.shape[0]):
        o_ref[c] = res[:, c * LANES:(c + 1) * LANES].astype(o_ref.dtype)


def proj_slabs(h, w, w2=None, *, tm=1024, tn=2048):
    m, d = h.shape
    n = w.shape[1]
    tm, tn = min(tm, m), min(tn, n)
    assert m % tm == 0 and n % tn == 0 and tn % LANES == 0
    out_shape = [jax.ShapeDtypeStruct((n // LANES, m, LANES), jnp.bfloat16)]
    if w2 is None:
        return pl.pallas_call(
            functools.partial(_proj_body, has_extra=False),
            grid=(n // tn, m // tm),
            in_specs=[pl.BlockSpec((tm, d), lambda j, i: (i, 0)),
                      pl.BlockSpec((d, tn), lambda j, i: (0, j))],
            out_specs=[pl.BlockSpec((tn // LANES, tm, LANES), lambda j, i: (j, i, 0))],
            out_shape=out_shape,
            compiler_params=_params("parallel", "parallel"),
            name="proj",
        )(h, w)[0]
    out_shape.append(jax.ShapeDtypeStruct((m, w2.shape[1]), jnp.float32))
    return pl.pallas_call(
        functools.partial(_proj_body, has_extra=True),
        grid=(m // tm, n // tn),
        in_specs=[pl.BlockSpec((tm, d), lambda i, j: (i, 0)),
                  pl.BlockSpec((d, tn), lambda i, j: (0, j)),
                  pl.BlockSpec((d, w2.shape[1]), lambda i, j: (0, 0))],
        out_specs=[pl.BlockSpec((tn // LANES, tm, LANES), lambda i, j: (j, i, 0)),
                   pl.BlockSpec((tm, w2.shape[1]), lambda i, j: (i, 0))],
        out_shape=out_shape,
        compiler_params=_params("parallel", "arbitrary"),
        name="proj",
    )(h, w, w2)


def _out_proj_body(a_ref, g_ref, w_ref, x_ref, gam_ref, *rest, has_next):
    if has_next:
        gam_next_ref, o_ref, h_next_ref = rest
    else:
        (o_ref,) = rest
    ns, tm, _ = a_ref.shape
    n_chunks = tm // OUT_PROJ_ROWS

    def gated(c):
        rs = slice(c * OUT_PROJ_ROWS, (c + 1) * OUT_PROJ_ROWS)
        g = jnp.concatenate([g_ref[s, rs, :] for s in range(ns)], axis=1).astype(jnp.float32)
        a = jnp.concatenate([a_ref[s, rs, :] for s in range(ns)], axis=1).astype(jnp.float32)
        half_g = 0.5 * g
        return (a * (half_g * (1.0 + jnp.tanh(half_g)))).astype(jnp.bfloat16)

    def finish(c, y):
        rs = slice(c * OUT_PROJ_ROWS, (c + 1) * OUT_PROJ_ROWS)
        ms = jnp.mean(y * y, axis=-1, keepdims=True)
        x_new = x_ref[rs, :] + y * lax.rsqrt(ms + RMS_EPS) * gam_ref[...]
        o_ref[rs, :] = x_new
        if has_next:
            ms_new = jnp.mean(x_new * x_new, axis=-1, keepdims=True)
            h_next_ref[rs, :] = (x_new * lax.rsqrt(ms_new + RMS_EPS)
                                 * gam_next_ref[...]).astype(h_next_ref.dtype)

    a_next = gated(0)
    y_prev = None
    for c in range(n_chunks):
        a_cur = a_next
        if c + 1 < n_chunks:
            a_next = gated(c + 1)
        y = jnp.dot(a_cur, w_ref[...], preferred_element_type=jnp.float32)
        if y_prev is not None:
            finish(c - 1, y_prev)
        y_prev = y
    finish(n_chunks - 1, y_prev)


def out_proj(a, proj, g_block, w, x, gamma, gamma_next=None, *, tm=512):
    ns, m, _ = a.shape
    e, d = w.shape
    assert ns * LANES == e
    tm = min(tm, m)
    assert m % tm == 0 and tm % OUT_PROJ_ROWS == 0
    row_spec = pl.BlockSpec((tm, d), lambda i: (i, 0))
    vec_spec = pl.BlockSpec((1, d), lambda i: (0, 0))
    in_specs = [pl.BlockSpec((ns, tm, LANES), lambda i: (0, i, 0)),
                pl.BlockSpec((ns, tm, LANES), lambda i: (g_block, i, 0)),
                pl.BlockSpec((e, d), lambda i: (0, 0), pipeline_mode=pl.Buffered(1)),
                row_spec, vec_spec]
    out_shape = [jax.ShapeDtypeStruct((m, d), jnp.float32)]
    out_specs = [row_spec]
    args = [a, proj, w, x, gamma.reshape(1, d)]
    if gamma_next is not None:
        in_specs.append(vec_spec)
        out_shape.append(jax.ShapeDtypeStruct((m, d), jnp.bfloat16))
        out_specs.append(row_spec)
        args.append(gamma_next.reshape(1, d))
    out = pl.pallas_call(
        functools.partial(_out_proj_body, has_next=gamma_next is not None),
        grid=(m // tm,),
        in_specs=in_specs, out_specs=out_specs, out_shape=out_shape,
        compiler_params=_params("parallel"),
        name="out_proj",
    )(*args)
    return out if gamma_next is not None else out[0]


def _chunk_attn_body(q_ref, k_ref, v_ref, bias_ref, o_ref, s_ref, p_ref, l_ref):
    hp, seq, _ = q_ref.shape
    n_tiles = seq // A_QTILE
    n_cut = A_LEFT // A_QTILE

    def window(t):
        q0 = t * A_QTILE
        k0 = max(q0 - A_LEFT, 0)
        n_keys = q0 + A_QTILE - k0
        return slice(q0, q0 + A_QTILE), slice(k0, k0 + n_keys), n_keys

    def scores(t, slot):
        q_rows, k_rows, n_keys = window(t)
        for h in range(hp):
            s = lax.dot_general(q_ref[h, q_rows, :], k_ref[h, k_rows, :], (((1,), (1,)), ((), ())),
                                preferred_element_type=jnp.float32)
            s_ref[slot, h, :, :n_keys] = s + bias_ref[h, min(t, n_cut), :, :n_keys]

    def softmax(t, slot):
        _, _, n_keys = window(t)
        for h in range(hp):
            s = s_ref[slot, h, :, :n_keys]
            p = jnp.exp2(s - jnp.max(s, axis=-1, keepdims=True))
            l_ref[slot, h] = jnp.broadcast_to(jnp.sum(p, axis=-1, keepdims=True), (A_QTILE, LANES))
            p_ref[slot, h, :, :n_keys] = p.astype(p_ref.dtype)

    def weighted_values(t, slot):
        q_rows, k_rows, n_keys = window(t)
        for h in range(hp):
            o = jnp.dot(p_ref[slot, h, :, :n_keys], v_ref[h, k_rows, :],
                        preferred_element_type=jnp.float32)
            o_ref[h, q_rows, :] = (o / l_ref[slot, h]).astype(o_ref.dtype)

    scores(0, 0)
    scores(1, 1)
    softmax(0, 0)
    for t in range(n_tiles):
        if t + 2 < n_tiles:
            scores(t + 2, t % 2)
        if t + 1 < n_tiles:
            softmax(t + 1, (t + 1) % 2)
        weighted_values(t, t % 2)


def _bias_tables_body(u_ref, o_ref):
    n_tab, qt, kt = o_ref.shape[1:]
    qi = lax.broadcasted_iota(jnp.int32, (qt, kt), 0)
    j = lax.broadcasted_iota(jnp.int32, (qt, kt), 1)
    chunk_start = (qi // A_CHUNK) * A_CHUNK
    for i in range(n_tab):
        diag = jnp.broadcast_to(u_ref[0, i:i + 1, :], (qt, u_ref.shape[2]))
        table = pltpu.roll(diag, 0, 1, stride=1, stride_axis=0)[:, :kt]
        band_start = chunk_start + (i * qt - A_LEFT)
        visible = (j >= band_start) & (j < band_start + (A_LEFT_CHUNKS + 1) * A_CHUNK)
        o_ref[0, i] = jnp.where(visible, LOG2E * table, NEG_INF)


def _band_bias_tables(rel_bias):
    heads = rel_bias.shape[0]
    n_tab = A_LEFT // A_QTILE + 1
    n_diag = pl.cdiv(A_QTILE + A_KTILE - 1, LANES) * LANES
    m = np.arange(n_diag)
    key_minus_query = np.where(m < A_KTILE, m, m - n_diag)
    offsets = A_QTILE * np.arange(n_tab)[:, None]
    idx = np.clip(offsets - key_minus_query[None, :], -A_REL_CLIP, A_REL_CLIP) + A_REL_CLIP
    diags = rel_bias.astype(jnp.float32)[:, idx]
    return pl.pallas_call(
        _bias_tables_body,
        grid=(heads,),
        in_specs=[pl.BlockSpec((1, n_tab, n_diag), lambda h: (h, 0, 0))],
        out_specs=pl.BlockSpec((1, n_tab, A_QTILE, A_KTILE), lambda h: (h, 0, 0, 0)),
        out_shape=jax.ShapeDtypeStruct((heads, n_tab, A_QTILE, A_KTILE), jnp.float32),
        compiler_params=_params("parallel"),
        name="bias_tables",
    )(diags)


def chunk_attention(proj, rel_bias, *, batch, seq):
    heads = A_HEADS
    hp = A_HEADS_PER_STEP
    assert proj.shape[0] == 4 * heads and seq % (2 * A_QTILE) == 0 and seq >= A_KTILE
    tables = _band_bias_tables(rel_bias)
    blk = (hp, seq, LANES)
    nhb = heads // hp
    return pl.pallas_call(
        _chunk_attn_body,
        grid=(nhb, batch),
        in_specs=[pl.BlockSpec(blk, lambda h, b: (h, b, 0)),
                  pl.BlockSpec(blk, lambda h, b: (nhb + h, b, 0)),
                  pl.BlockSpec(blk, lambda h, b: (2 * nhb + h, b, 0)),
                  pl.BlockSpec((hp,) + tables.shape[1:], lambda h, b: (h, 0, 0, 0))],
        out_specs=pl.BlockSpec(blk, lambda h, b: (h, b, 0)),
        out_shape=jax.ShapeDtypeStruct((heads, batch * seq, LANES), jnp.bfloat16),
        scratch_shapes=[pltpu.VMEM((2, hp, A_QTILE, A_KTILE), jnp.float32),
                        pltpu.VMEM((2, hp, A_QTILE, A_KTILE), jnp.bfloat16),
                        pltpu.VMEM((2, hp, A_QTILE, LANES), jnp.float32)],
        compiler_params=_params("parallel", "parallel"),
        name="chunk_attn",
    )(proj, proj, proj, tables)


def _lru_body(xr_ref, cw_ref, cb_ref, wg_ref, ba_ref, bx_ref, lam_ref, o_ref,
              xbuf_ref, y_ref, pre_ref, a_ref, u_ref, hs_ref, h_ref, *, groups, gw, pieces):
    ns, nb, ts, _ = o_ref.shape
    rows = ts * nb
    hist = CONV_WIDTH * nb
    sub = LRU_SUB_STEPS
    pr = sub * nb

    @pl.when(pl.program_id(0) == 0)
    def _():
        xbuf_ref[0:hist, :] = jnp.zeros((hist, xbuf_ref.shape[1]), jnp.float32)
        h_ref[...] = jnp.zeros(h_ref.shape, jnp.float32)

    r_idx = lax.broadcasted_iota(jnp.int32, (pr, pr), 0)
    k_idx = lax.broadcasted_iota(jnp.int32, (pr, pr), 1)
    perm = jnp.where(k_idx == (r_idx % nb) * sub + r_idx // nb, 1.0, 0.0).astype(jnp.bfloat16)

    for part in range(ts // sub):
        steps = slice(part * sub, (part + 1) * sub)
        x_bm = jnp.concatenate(
            [jnp.concatenate([xr_ref[c, b, steps, :] for b in range(nb)], axis=0) for c in range(ns)],
            axis=1)
        xbuf_ref[hist + part * pr:hist + (part + 1) * pr, :] = jnp.dot(
            perm, x_bm, preferred_element_type=jnp.float32)

    def conv_and_gate_matmul(gi, slot):
        cs = slice(gi * gw, (gi + 1) * gw)
        y = cb_ref[:, cs] + cw_ref[CONV_WIDTH - 1:CONV_WIDTH, cs] * xbuf_ref[hist:hist + rows, cs]
        for tap in range(CONV_WIDTH - 1):
            off = hist - (CONV_WIDTH - 1 - tap) * nb
            y = y + cw_ref[tap:tap + 1, cs] * xbuf_ref[off:off + rows, cs]
        y_ref[slot] = y
        y_bf = y.astype(jnp.bfloat16)
        for p, (c0, c1, k0, k1) in enumerate(pieces):
            w = wg_ref[gi, p, :k1 - k0, :2 * (c1 - c0)]
            pre = jnp.dot(y_bf[:, k0:k1], w, preferred_element_type=jnp.float32)
            pre_ref[slot, :, c0:c1] = pre[:, :c1 - c0]
            pre_ref[slot, :, gw + c0:gw + c1] = pre[:, c1 - c0:]

    def gates(gi, slot):
        cs = slice(gi * gw, (gi + 1) * gw)
        half_ba = 0.5 * ba_ref[:, cs]
        half_bx = 0.5 * bx_ref[:, cs]
        lam = lam_ref[:, cs]
        softplus_neg_lam = jnp.maximum(-lam, 0.0) + jnp.log1p(jnp.exp(-jnp.abs(lam)))
        half_coef = (-0.5 * RG_C) * softplus_neg_lam
        for k in range(rows // LRU_GATE_ROWS):
            rk = slice(k * LRU_GATE_ROWS, (k + 1) * LRU_GATE_ROWS)
            tanh_r = jnp.tanh(pre_ref[slot, rk, :gw] + half_ba)
            tanh_i = jnp.tanh(pre_ref[slot, rk, gw:] + half_bx)
            log_a = half_coef * tanh_r + half_coef
            a = jnp.exp(log_a)
            quarter = (-0.25 * jnp.tanh(log_a)) * (1.0 + a * a)
            half_mult = quarter * lax.rsqrt(jnp.maximum(quarter, F32_TINY))
            a_ref[rk, cs] = a
            u_ref[rk, cs] = half_mult * ((tanh_i + 1.0) * y_ref[slot, rk, :])

    conv_and_gate_matmul(0, 0)
    for gi in range(groups):
        if gi + 1 < groups:
            conv_and_gate_matmul(gi + 1, (gi + 1) % 2)
        gates(gi, gi % 2)

    xbuf_ref[0:hist, :] = xbuf_ref[rows:rows + hist, :]

    def time_step(t, h):
        rs = pl.ds(pl.multiple_of(t * nb, nb), nb)
        h = a_ref[rs, :] * h + u_ref[rs, :]
        hs_ref[rs, :] = h.astype(hs_ref.dtype)
        return h

    h_ref[...] = lax.fori_loop(0, ts, time_step, h_ref[...])

    for part in range(ts // sub):
        steps = slice(part * sub, (part + 1) * sub)
        hs_bm = jnp.dot(perm, hs_ref[part * pr:(part + 1) * pr, :],
                        preferred_element_type=jnp.float32).astype(o_ref.dtype)
        for c in range(ns):
            for b in range(nb):
                o_ref[c, b, steps, :] = hs_bm[b * sub:(b + 1) * sub, c * LANES:(c + 1) * LANES]


def _gate_pieces(gw, bs):
    pieces = []
    for c0 in range(0, gw, MXU_WIDTH):
        c1 = min(c0 + MXU_WIDTH, gw)
        k0 = (c0 // bs) * bs // LANES * LANES
        k1 = min(pl.cdiv(((c1 - 1) // bs + 1) * bs, LANES) * LANES, gw)
        pieces.append((c0, c1, k0, k1))
    return pieces


def _gate_group_weights(w_a, w_x):
    nb, bs, _ = w_a.shape
    groups = nb // RG_GROUP_BLOCKS
    gw = RG_GROUP_BLOCKS * bs
    pieces = _gate_pieces(gw, bs)
    k_max = max(k1 - k0 for _, _, k0, k1 in pieces)
    n_max = max(2 * (c1 - c0) for c0, c1, _, _ in pieces)

    def block_diag(w):
        w = (0.5 * w).astype(jnp.bfloat16).reshape(groups, RG_GROUP_BLOCKS, bs, bs)
        rows = [jnp.pad(w[:, n], ((0, 0), (0, 0), (n * bs, gw - (n + 1) * bs)))
                for n in range(RG_GROUP_BLOCKS)]
        return jnp.concatenate(rows, axis=1)

    full_a, full_x = block_diag(w_a), block_diag(w_x)
    out = []
    for c0, c1, k0, k1 in pieces:
        piece = jnp.concatenate([full_a[:, k0:k1, c0:c1], full_x[:, k0:k1, c0:c1]], axis=2)
        out.append(jnp.pad(piece, ((0, 0), (0, k_max - (k1 - k0)), (0, n_max - 2 * (c1 - c0)))))
    return jnp.stack(out, axis=1)


def rg_lru(proj, conv_w, conv_b, w_a, b_a, w_x, b_x, lam, *, batch, seq):
    ns = proj.shape[0] // 2
    width = ns * LANES
    gw = RG_GROUP_BLOCKS * (width // RG_BLOCKS)
    groups = width // gw
    ts = min(LRU_TS, seq)
    rows = ts * batch
    assert seq % ts == 0 and gw % LANES == 0 and ts >= CONV_WIDTH and ts % LRU_SUB_STEPS == 0
    assert batch == LRU_SUB_STEPS and rows % LRU_GATE_ROWS == 0
    wg = _gate_group_weights(w_a, w_x)
    vec = lambda v: v.reshape(1, width).astype(jnp.float32)
    const2 = lambda s: (0, 0)
    blk = pl.BlockSpec((ns, batch, ts, LANES), lambda s: (0, 0, s, 0))
    out = pl.pallas_call(
        functools.partial(_lru_body, groups=groups, gw=gw, pieces=_gate_pieces(gw, width // RG_BLOCKS)),
        grid=(seq // ts,),
        in_specs=[blk,
                  pl.BlockSpec((CONV_WIDTH, width), const2),
                  pl.BlockSpec((1, width), const2),
                  pl.BlockSpec(wg.shape, lambda s: (0, 0, 0, 0), pipeline_mode=pl.Buffered(1)),
                  pl.BlockSpec((1, width), const2),
                  pl.BlockSpec((1, width), const2),
                  pl.BlockSpec((1, width), const2)],
        out_specs=blk,
        out_shape=jax.ShapeDtypeStruct((ns, batch, seq, LANES), jnp.bfloat16),
        scratch_shapes=[pltpu.VMEM((rows + CONV_WIDTH * batch, width), jnp.float32),
                        pltpu.VMEM((2, rows, gw), jnp.float32),
                        pltpu.VMEM((2, rows, 2 * gw), jnp.float32),
                        pltpu.VMEM((rows, width), jnp.float32),
                        pltpu.VMEM((rows, width), jnp.float32),
                        pltpu.VMEM((rows, width), jnp.bfloat16),
                        pltpu.VMEM((batch, width), jnp.float32)],
        compiler_params=_params("arbitrary"),
        name="rg_lru",
    )(proj.reshape(2 * ns, batch, seq, LANES), conv_w.astype(jnp.float32), vec(conv_b), wg,
      vec(b_a), vec(b_x), vec(lam))
    return out.reshape(ns, batch * seq, LANES)


FOX_PIECE_LANES = 16 * N_SPLIT


def _split3(x):
    hi = x.astype(jnp.bfloat16).astype(jnp.float32)
    r1 = x - hi
    mid = r1.astype(jnp.bfloat16).astype(jnp.float32)
    lo = (r1 - mid).astype(jnp.bfloat16).astype(jnp.float32)
    return hi, mid, lo


def _forget_gate_body(f_ref, fb_ref, qx_ref, kx_ref, *, heads):
    z = f_ref[0] + fb_ref[...]
    c = jnp.minimum(z, 0.0) - jnp.log1p(jnp.exp(-jnp.abs(z)))
    n = c.shape[0]
    row = lax.broadcasted_iota(jnp.int32, c.shape, 0)
    d = 1
    while d < n:
        c = c + jnp.where(row >= d, pltpu.roll(c, d, 0), 0.0)
        d *= 2
    lane = lax.broadcasted_iota(jnp.int32, c.shape, 1)
    hi, mid, lo = _split3(LOG2E * c)
    pieces = jnp.where(lane < heads, hi,
                       jnp.where(lane < 2 * heads, pltpu.roll(mid, heads, 1),
                                 jnp.where(lane < 3 * heads, pltpu.roll(lo, 2 * heads, 1), 0.0)))
    qx_ref[...] = pieces.astype(qx_ref.dtype)
    kx_ref[...] = pltpu.roll(-pieces, FOX_PIECE_LANES, 1).astype(kx_ref.dtype)


def forget_gate_slabs(f_logit, f_bias_padded, *, batch, seq, heads):
    assert heads * N_SPLIT == FOX_PIECE_LANES and 2 * FOX_PIECE_LANES <= LANES
    f3 = f_logit.reshape(batch, seq, LANES)
    shape = jax.ShapeDtypeStruct((batch * seq, LANES), jnp.bfloat16)
    spec = pl.BlockSpec((seq, LANES), lambda b: (b, 0))
    return pl.pallas_call(
        functools.partial(_forget_gate_body, heads=heads),
        grid=(batch,),
        in_specs=[pl.BlockSpec((1, seq, LANES), lambda b: (b, 0, 0)),
                  pl.BlockSpec((1, LANES), lambda b: (0, 0))],
        out_specs=[spec, spec],
        out_shape=[shape, shape],
        compiler_params=_params("parallel"),
        name="forget_gate",
    )(f3, f_bias_padded)


def _fox_body(q_ref, qx_ref, k_ref, kx_ref, v_ref, o_ref, s_ref, m_ref, l_ref, acc_ref):
    hp, seq, dh = q_ref.shape
    t = FOX_T
    heads = FOX_PIECE_LANES // N_SPLIT
    steps = [(i, j) for i in range(seq // t) for j in range(i + 1)]
    lane = lax.broadcasted_iota(jnp.int32, (1, LANES), 1)

    def augment(x_ref, px_ref, h, r, first_lane):
        rel = lane - (first_lane + pl.program_id(1) * hp + h)
        select = (rel == 0) | (rel == heads) | (rel == 2 * heads)
        extra = jnp.where(select, jnp.ones((), px_ref.dtype), px_ref[r, :])
        return jnp.concatenate([x_ref[h, r, :], extra], axis=1)

    def parts(i, j):
        if i != j:
            return [(0, t, t)]
        return [(0, t // 2, t // 2), (t // 2, t, t)]

    def scores(n, slot):
        i, j = steps[n]
        for h in range(hp):
            for r0, r1, n_keys in parts(i, j):
                q_aug = augment(q_ref, qx_ref, h, slice(i * t + r0, i * t + r1), FOX_PIECE_LANES)
                k_aug = augment(k_ref, kx_ref, h, slice(j * t, j * t + n_keys), 0)
                s = lax.dot_general(q_aug, k_aug, (((1,), (1,)), ((), ())),
                                    preferred_element_type=jnp.float32)
                if i == j:
                    r = lax.broadcasted_iota(jnp.int32, s.shape, 0) + r0
                    c = lax.broadcasted_iota(jnp.int32, s.shape, 1)
                    s = jnp.where(c <= r, s, NEG_INF)
                s_ref[slot, h, r0:r1, :n_keys] = s

    def finish(n, slot):
        i, j = steps[n]
        work = [(h, rc, rc + FOX_ROWS, n_keys) for h in range(hp) for r0, r1, n_keys in parts(i, j)
                for rc in range(r0, r1, FOX_ROWS)]
        for h, r0, r1, n_keys in work:
            s = s_ref[slot, h, r0:r1, :n_keys]
            m_cur = jnp.max(s, axis=1, keepdims=True)
            if j == 0:
                m_next = jnp.broadcast_to(m_cur, (r1 - r0, LANES))
            else:
                m_prev = m_ref[h, r0:r1, :]
                m_next = jnp.maximum(m_prev, m_cur)
                alpha = jnp.exp2(m_prev - m_next)
            m_ref[h, r0:r1, :] = m_next
            p = jnp.exp2(s - jnp.tile(m_next, (1, n_keys // LANES)))
            l_cur = jnp.sum(p, axis=1, keepdims=True)
            pv = jnp.dot(p.astype(jnp.bfloat16), v_ref[h, j * t:j * t + n_keys, :],
                         preferred_element_type=jnp.float32)
            if j == 0:
                l_new = jnp.broadcast_to(l_cur, (r1 - r0, LANES))
                acc = pv
            else:
                l_new = l_cur + alpha * l_ref[h, r0:r1, :]
                acc = alpha * acc_ref[h, r0:r1, :] + pv
            if j == i:
                o_ref[h, i * t + r0:i * t + r1, :] = (acc / l_new).astype(o_ref.dtype)
            else:
                l_ref[h, r0:r1, :] = l_new
                acc_ref[h, r0:r1, :] = acc

    scores(0, 0)
    for n in range(len(steps)):
        if n + 1 < len(steps):
            scores(n + 1, (n + 1) % 2)
        finish(n, n % 2)


def forgetting_attention(proj, qx, kx, *, batch, seq):
    heads = C_HEADS
    hp = HEADS_PER_STEP
    assert proj.shape[0] == 4 * heads and seq % FOX_T == 0
    blk = (hp, seq, LANES)
    nhb = heads // hp
    return pl.pallas_call(
        _fox_body,
        grid=(batch, nhb),
        in_specs=[pl.BlockSpec(blk, lambda b, h: (h, b, 0)),
                  pl.BlockSpec((seq, LANES), lambda b, h: (b, 0)),
                  pl.BlockSpec(blk, lambda b, h: (nhb + h, b, 0)),
                  pl.BlockSpec((seq, LANES), lambda b, h: (b, 0)),
                  pl.BlockSpec(blk, lambda b, h: (2 * nhb + h, b, 0))],
        out_specs=pl.BlockSpec(blk, lambda b, h: (h, b, 0)),
        out_shape=jax.ShapeDtypeStruct((heads, batch * seq, LANES), jnp.bfloat16),
        scratch_shapes=[pltpu.VMEM((2, hp, FOX_T, FOX_T), jnp.float32),
                        pltpu.VMEM((hp, FOX_T, LANES), jnp.float32),
                        pltpu.VMEM((hp, FOX_T, LANES), jnp.float32),
                        pltpu.VMEM((hp, FOX_T, LANES), jnp.float32)],
        compiler_params=_params("parallel", "parallel"),
        name="fox_attn",
    )(proj, qx, proj, kx, proj)


def _q_column_scale(n, width, dh):
    return jnp.where(jnp.arange(n) < width, LOG2E * dh ** -0.5, 1.0).astype(jnp.float32)


def _split_cast_body(wt_ref, wt_tail_ref, scale_ref, main_ref, tail_ref):
    main_ref[...] = (wt_ref[0].T * scale_ref[...]).astype(main_ref.dtype)

    @pl.when(pl.program_id(0) == 0)
    def _():
        n_tail = wt_tail_ref.shape[1]
        tail_ref[...] = jnp.zeros(tail_ref.shape, tail_ref.dtype)
        tail_ref[:, :n_tail] = wt_tail_ref[0].T.astype(tail_ref.dtype)


def split_cast_weight(w_stack, j, n_main, col_scale, *, tn=256):
    _, d, n = w_stack.shape
    n_tail = n - n_main
    assert n_main % tn == 0 and n_main % n_tail == 0 and 0 < n_tail <= LANES and n_tail % 8 == 0
    wt = jnp.swapaxes(w_stack, 1, 2)
    return pl.pallas_call(
        _split_cast_body,
        grid=(n_main // tn,),
        in_specs=[pl.BlockSpec((1, tn, d), lambda i: (j, i, 0)),
                  pl.BlockSpec((1, n_tail, d), lambda i: (j, n_main // n_tail, 0)),
                  pl.BlockSpec((1, tn), lambda i: (0, i))],
        out_specs=[pl.BlockSpec((d, tn), lambda i: (0, i)),
                   pl.BlockSpec((d, LANES), lambda i: (0, 0))],
        out_shape=[jax.ShapeDtypeStruct((d, n_main), jnp.bfloat16),
                   jax.ShapeDtypeStruct((d, LANES), jnp.bfloat16)],
        compiler_params=_params("arbitrary"),
        name="split_cast",
    )(wt, wt, col_scale.reshape(1, n_main))


def _input_projection(xs, h, gamma, w, w2=None, *, tn=2048):
    if h is None:
        return norm_proj(xs, gamma, w, w2)
    return proj_slabs(h, w, w2, tn=tn)


def kernel(x, norm_pre, norm_post, a_w_in, a_rel_bias, a_w_out, b_w_in, b_conv_w, b_conv_b,
           b_gate_a_w, b_gate_a_b, b_gate_x_w, b_gate_x_b, b_lambda, b_w_out,
           c_w_in, c_f_bias, c_w_out):
    batch, seq, d = x.shape
    depth = norm_pre.shape[0]
    bf = lambda w: w.astype(jnp.bfloat16)
    xs = x.reshape(batch * seq, d)
    h = None
    for i in range(depth):
        m, j = i % 3, i // 3
        gamma_next = norm_pre[i + 1] if i + 1 < depth else None
        if m == 0:
            e = a_w_out.shape[1]
            w_in = bf(a_w_in[j] * _q_column_scale(4 * e, e, e // A_HEADS))
            proj = _input_projection(xs, h, norm_pre[i], w_in)
            o = chunk_attention(proj, a_rel_bias[j], batch=batch, seq=seq)
            res = out_proj(o, proj, 3, bf(a_w_out[j]), xs, norm_post[i], gamma_next)
        elif m == 1:
            proj = _input_projection(xs, h, norm_pre[i], bf(b_w_in[j]), tn=b_w_in.shape[2] // 2)
            hs = rg_lru(proj, b_conv_w[j], b_conv_b[j], b_gate_a_w[j], b_gate_a_b[j],
                        b_gate_x_w[j], b_gate_x_b[j], b_lambda[j], batch=batch, seq=seq)
            res = out_proj(hs, proj, 1, bf(b_w_out[j]), xs, norm_post[i], gamma_next)
        else:
            e = c_w_out.shape[1]
            n_f = c_w_in.shape[2] - 4 * e
            w_main, w_f = split_cast_weight(c_w_in, j, 4 * e, _q_column_scale(4 * e, e, e // C_HEADS))
            f_bias = jnp.pad(c_f_bias[j].astype(jnp.float32), (0, LANES - n_f)).reshape(1, LANES)
            proj, f_logit = _input_projection(xs, h, norm_pre[i], w_main, w_f)
            qx, kx = forget_gate_slabs(f_logit, f_bias, batch=batch, seq=seq, heads=n_f)
            o = forgetting_attention(proj, qx, kx, batch=batch, seq=seq)
            res = out_proj(o, proj, 3, bf(c_w_out[j]), xs, norm_post[i], gamma_next)
        xs, h = res if gamma_next is not None else (res, None)
    return xs.reshape(batch, seq, d)
```
